```python
import jax, jax.numpy as jnp
from jax import lax
import numpy as np

D_MODEL = 1024
BATCH = 16
SEQ = 256
DEPTH = 4
DEC_BATCH = 2
DEC_SEQ = 4096
PAST_LEN = 512

GRID_W = 64
N_MIXERS = 3
N_MLSTM = (DEPTH + 2) // 3
N_ATTN = (DEPTH + 1) // 3
N_HGRN = DEPTH // 3

ML_H = 4
ML_DK = 128
ML_DV = 256
ML_CHUNK = 128
ML_CONV = 3
ML_QK = ML_H * ML_DK
ML_V = ML_H * ML_DV
ML_IN = 2 * ML_QK + 2 * ML_V + 4 * ML_H

AT_H = 16
AT_KV = 4
AT_HD = 64
AT_G = AT_H // AT_KV
AT_QBLOCK = 128
AT_IN = (AT_H + 2 * AT_KV) * AT_HD
ROPE_THETA = 10000.0

HG_H = 8
HG_DK = 128
HG_DV = D_MODEL // HG_H
HG_CHUNK = 64
HG_K = HG_H * HG_DK
HG_V = HG_H * HG_DV
HG_IN = 3 * HG_K + 2 * HG_V

N_EXPERTS = 64
TOP_K = 8
D_EXPERT = 256
D_SHARED = 256
ROUTE_SCALE = 2.5
MOE_BLOCK = 128

DEEPNORM_ALPHA = (2 * DEPTH) ** 0.25
DEEPNORM_BETA = (8 * DEPTH) ** -0.25
LN_EPS = 1e-6

kernel_name = 'hybrid_mlstm_gqa_hgrn2_moe_diffusion_step'


def layer_norm(x, g, b):
    xf = x.astype(jnp.float32)
    mu = jnp.mean(xf, -1, keepdims=True)
    var = jnp.mean(jnp.square(xf - mu), -1, keepdims=True)
    return ((xf - mu) * lax.rsqrt(var + LN_EPS) * g + b).astype(x.dtype)


def rms_norm(x, g):
    xf = x.astype(jnp.float32)
    return (xf * lax.rsqrt(jnp.mean(jnp.square(xf), -1, keepdims=True) + LN_EPS) * g).astype(x.dtype)


def head_layer_norm(x, g):
    xf = x.astype(jnp.float32)
    mu = jnp.mean(xf, -1, keepdims=True)
    var = jnp.mean(jnp.square(xf - mu), -1, keepdims=True)
    return (xf - mu) * lax.rsqrt(var + LN_EPS) * g.reshape(x.shape[-2:])


def modulate(x, mod, slot):
    shift = mod[:, None, (3 * slot) * D_MODEL:(3 * slot + 1) * D_MODEL]
    scale = mod[:, None, (3 * slot + 1) * D_MODEL:(3 * slot + 2) * D_MODEL]
    return x * (1 + scale) + shift


def post_residual(x, y, mod, slot, g, b):
    gate = mod[:, None, (3 * slot + 2) * D_MODEL:(3 * slot + 3) * D_MODEL]
    return layer_norm(DEEPNORM_ALPHA * x + gate * y, g, b).astype(x.dtype)


def rev(a):
    return jnp.flip(a, axis=1)


def to_chunks(a, L):
    B, T = a.shape[:2]
    a = a.reshape((B, T // L, L) + a.shape[2:])
    return a.transpose((1, 0, 3, 2) + tuple(range(4, a.ndim)))


def from_chunks(a):
    nc, B, H, L, D = a.shape
    return a.transpose(1, 0, 3, 2, 4).reshape(B, nc * L, H, D)


def short_conv(x, w):
    xp = jnp.pad(x, ((0, 0), (1, 1), (0, 0)))
    return w[0] * xp[:, :-2] + w[1] * xp[:, 1:-1] + w[2] * xp[:, 2:]


def axial_rope(x):
    T = x.shape[1]
    rows = T // GRID_W
    row = jnp.repeat(jnp.arange(rows), GRID_W).astype(jnp.float32)
    col = jnp.tile(jnp.arange(GRID_W), rows).astype(jnp.float32)
    half = AT_HD // 2
    nf = half // 2
    inv = ROPE_THETA ** (-jnp.arange(nf, dtype=jnp.float32) / nf)

    def rot(xh, pos):
        ang = pos[:, None] * inv[None]
        cos = jnp.cos(ang)[None, :, None]
        sin = jnp.sin(ang)[None, :, None]
        x1, x2 = xh[..., :nf], xh[..., nf:]
        return jnp.concatenate([x1 * cos - x2 * sin, x1 * sin + x2 * cos], -1)

    return jnp.concatenate([rot(x[..., :half], row), rot(x[..., half:], col)], -1).astype(x.dtype)


def block_attention(q, k, v):
    B, Tq = q.shape[:2]
    nb = Tq // AT_QBLOCK
    qb = q.reshape(B, nb, AT_QBLOCK, AT_KV, AT_G, AT_HD).transpose(1, 0, 2, 3, 4, 5)
    scale = AT_HD ** -0.5

    def one_block(qblk):
        s = jnp.einsum('bqkgd,bskd->bkgqs', qblk, k).astype(jnp.float32) * scale
        p = jax.nn.softmax(s, axis=-1).astype(v.dtype)
        return jnp.einsum('bkgqs,bskd->bqkgd', p, v)

    o = lax.map(one_block, qb)
    return o.transpose(1, 0, 2, 3, 4, 5).reshape(B, Tq, AT_H * AT_HD)


def attn_project(h, w_in, q_norm, k_norm):
    B, T, _ = h.shape
    z = h @ w_in
    q = z[..., :AT_H * AT_HD].reshape(B, T, AT_H, AT_HD)
    k = z[..., AT_H * AT_HD:(AT_H + AT_KV) * AT_HD].reshape(B, T, AT_KV, AT_HD)
    v = z[..., (AT_H + AT_KV) * AT_HD:].reshape(B, T, AT_KV, AT_HD)
    return rms_norm(q, q_norm), rms_norm(k, k_norm), v


def attn_context(h, w_in, q_norm, k_norm, w_out):
    q, k, v = attn_project(h, w_in, q_norm, k_norm)
    return (block_attention(q, k, v) @ w_out).astype(h.dtype), k, v


def attn_latent(h, k_ctx, v_ctx, w_in, q_norm, k_norm, w_out):
    q, k, v = attn_project(h, w_in, q_norm, k_norm)
    q, k = axial_rope(q), axial_rope(k)
    kk = jnp.concatenate([k_ctx.astype(k.dtype), k], axis=1)
    vv = jnp.concatenate([v_ctx.astype(v.dtype), v], axis=1)
    return (block_attention(q, kk, vv) @ w_out).astype(h.dtype)


def mlstm_scan(q, k, v, ig, lf, C0, n0, m0):
    L = ML_CHUNK
    causal = jnp.tril(jnp.ones((L, L), dtype=bool))

    def step(carry, xs):
        C, n, m = carry
        qc, kc, vc, igc, lfc = xs
        b = jnp.cumsum(lfc, axis=-1)
        log_d = jnp.where(causal, b[..., :, None] - b[..., None, :] + igc[..., None, :], -jnp.inf)
        log_inter = b + m[..., None]
        m_t = jnp.maximum(log_inter, jnp.max(log_d, axis=-1))
        d = jnp.exp(log_d - m_t[..., None])
        w_inter = jnp.exp(log_inter - m_t)
        s = jnp.einsum('bhtd,bhsd->bhts', qc, kc) * d
        num = jnp.einsum('bhts,bhsv->bhtv', s, vc) + w_inter[..., None] * jnp.einsum('bhtd,bhdv->bhtv', qc, C)
        den = jnp.sum(s, axis=-1) + w_inter * jnp.einsum('bhtd,bhd->bht', qc, n)
        h = num / jnp.maximum(jnp.abs(den), jnp.exp(-m_t))[..., None]
        w_last = d[..., -1, :]
        decay = w_inter[..., -1]
        C = decay[..., None, None] * C + jnp.einsum('bhs,bhsd,bhsv->bhdv', w_last, kc, vc)
        n = decay[..., None] * n + jnp.einsum('bhs,bhsd->bhd', w_last, kc)
        return (C, n, m_t[..., -1]), h

    xs = tuple(to_chunks(a.astype(jnp.float32), L) for a in (q, k, v, ig, lf))
    init = (C0.astype(jnp.float32), n0.astype(jnp.float32), m0.astype(jnp.float32))
    state, h = lax.scan(step, init, xs)
    return from_chunks(h), state


def mlstm_mixer(h, C0, n0, m0, w_in, gate_b, conv_w, norm_g, w_out):
    B, T, _ = h.shape
    z = h @ w_in
    qk = jax.nn.silu(short_conv(z[..., :2 * ML_QK], conv_w))
    q = qk[..., :ML_QK].reshape(B, T, ML_H, ML_DK) * ML_DK ** -0.5
    k = qk[..., ML_QK:].reshape(B, T, ML_H, ML_DK)
    v = z[..., 2 * ML_QK:2 * ML_QK + ML_V].reshape(B, T, ML_H, ML_DV)
    o_gate = jax.nn.sigmoid(z[..., 2 * ML_QK + ML_V:2 * ML_QK + 2 * ML_V])
    gates = (z[..., 2 * ML_QK + 2 * ML_V:] + gate_b).astype(jnp.float32).reshape(B, T, 4, ML_H)
    ig_f, lf_f = gates[:, :, 0], jax.nn.log_sigmoid(gates[:, :, 1])
    ig_b, lf_b = gates[:, :, 2], jax.nn.log_sigmoid(gates[:, :, 3])
    h_f, (C_f, n_f, m_f) = mlstm_scan(q, k, v, ig_f, lf_f, C0[:, 0], n0[:, 0], m0[:, 0])
    h_b, (C_b, n_b, m_b) = mlstm_scan(rev(q), rev(k), rev(v), rev(ig_b), rev(lf_b), C0[:, 1], n0[:, 1], m0[:, 1])
    hh = head_layer_norm(h_f + rev(h_b), norm_g).reshape(B, T, ML_V)
    out = (hh * o_gate).astype(h.dtype) @ w_out
    return out.astype(h.dtype), (jnp.stack([C_f, C_b], 1), jnp.stack([n_f, n_b], 1), jnp.stack([m_f, m_b], 1))


def hgrn_scan(q, k, v, g, S0):
    L = HG_CHUNK
    ref = L // 2
    causal = jnp.tril(jnp.ones((L, L), dtype=bool))

    def step(S, xs):
        qc, kc, vc, gc = xs
        b = jnp.cumsum(gc, axis=2)
        b_ref = b[:, :, ref:ref + 1]
        a = jnp.einsum('bhtd,bhsd->bhts', qc * jnp.exp(b - b_ref), kc * jnp.exp(b_ref - b))
        a = jnp.where(causal, a, 0.0)
        o = jnp.einsum('bhts,bhsv->bhtv', a, vc) + jnp.einsum('bhtd,bhdv->bhtv', qc * jnp.exp(b), S)
        b_last = b[:, :, -1:]
        S = jnp.exp(b_last[:, :, 0])[..., None] * S + jnp.einsum('bhsd,bhsv->bhdv', kc * jnp.exp(b_last - b), vc)
        return S, o

    xs = tuple(to_chunks(a.astype(jnp.float32), L) for a in (q, k, v, g))
    S, o = lax.scan(step, S0.astype(jnp.float32), xs)
    return from_chunks(o), S


def hgrn_mixer(h, S0, w_in, f_b, lb, norm_g, w_out):
    B, T, _ = h.shape
    z = h @ w_in
    q = jax.nn.silu(z[..., :HG_K]).reshape(B, T, HG_H, HG_DK)
    i = z[..., HG_K:HG_K + HG_V].reshape(B, T, HG_H, HG_DV)
    f_raw = (z[..., HG_K + HG_V:3 * HG_K + HG_V] + f_b).astype(jnp.float32).reshape(B, T, 2, HG_K)
    f = (lb + (1.0 - lb) * jax.nn.sigmoid(f_raw)).reshape(B, T, 2, HG_H, HG_DK)
    k, g = 1.0 - f, jnp.log(f)
    o_gate = jax.nn.silu(z[..., 3 * HG_K + HG_V:])
    o_f, S_f = hgrn_scan(q, k[:, :, 0], i, g[:, :, 0], S0[:, 0])
    o_b, S_b = hgrn_scan(rev(q), rev(k[:, :, 1]), rev(i), rev(g[:, :, 1]), S0[:, 1])
    o = rms_norm(o_f + rev(o_b), norm_g.reshape(HG_H, HG_DV)).reshape(B, T, HG_V)
    out = (o * o_gate).astype(h.dtype) @ w_out
    return out.astype(h.dtype), jnp.stack([S_f, S_b], 1)


def routed_experts(x, idx, w, w_gate, w_up, w_down):
    N, D = x.shape
    nk = N * TOP_K
    flat_e = idx.reshape(nk)
    order = jnp.argsort(flat_e)
    e_sorted = flat_e[order]
    tok_sorted = (order // TOP_K).astype(jnp.int32)
    w_sorted = w.reshape(nk)[order]
    counts = jnp.zeros((N_EXPERTS,), jnp.int32).at[flat_e].add(1)
    starts = jnp.cumsum(counts) - counts
    padded = (counts + MOE_BLOCK - 1) // MOE_BLOCK * MOE_BLOCK
    pad_end = jnp.cumsum(padded)
    pad_start = pad_end - padded
    dest = pad_start[e_sorted] + jnp.arange(nk, dtype=jnp.int32) - starts[e_sorted]
    n_blocks = -(-nk // MOE_BLOCK) + N_EXPERTS
    rows = n_blocks * MOE_BLOCK
    row_tok = jnp.full((rows,), N, jnp.int32).at[dest].set(tok_sorted)
    block_e = jnp.minimum(jnp.searchsorted(pad_end, jnp.arange(n_blocks) * MOE_BLOCK, side='right'), N_EXPERTS - 1)
    x_rows = jnp.concatenate([x, jnp.zeros((1, D), x.dtype)], 0)[row_tok].reshape(n_blocks, MOE_BLOCK, D)

    def expert_block(args):
        xb, e = args
        return (jax.nn.silu(xb @ w_gate[e]) * (xb @ w_up[e])) @ w_down[e]

    y_rows = lax.map(expert_block, (x_rows, block_e)).reshape(rows, D)
    contrib = y_rows[dest] * w_sorted[:, None]
    return jax.ops.segment_sum(contrib, tok_sorted, num_segments=N)


def moe(h, router_w, router_b, w_gate, w_up, w_down, sh_gate, sh_up, sh_down):
    B, T, D = h.shape
    x = h.reshape(B * T, D)
    scores = jax.nn.sigmoid((x @ router_w).astype(jnp.float32))
    _, idx = lax.top_k(scores + router_b.astype(jnp.float32), TOP_K)
    w = jnp.take_along_axis(scores, idx, axis=-1)
    w = (w / jnp.sum(w, -1, keepdims=True) * ROUTE_SCALE).astype(x.dtype)
    routed = routed_experts(x, idx, w, w_gate, w_up, w_down)
    shared = (jax.nn.silu(x @ sh_gate) * (x @ sh_up)) @ sh_down
    return (routed + shared).reshape(B, T, D).astype(h.dtype)


def setup_inputs(seed: int = 0) -> dict:
    key = jax.random.key(seed)
    ks = iter(jax.random.split(key, 64))

    def nrm(shape, scale):
        return jax.random.normal(next(ks), shape, jnp.float32) * scale

    s = D_MODEL ** -0.5
    gb = jnp.stack([nrm((N_MLSTM, ML_H), 0.1),
                    jnp.linspace(3.0, 6.0, ML_H)[None] + nrm((N_MLSTM, ML_H), 0.1),
                    nrm((N_MLSTM, ML_H), 0.1),
                    jnp.linspace(3.0, 6.0, ML_H)[None] + nrm((N_MLSTM, ML_H), 0.1)], axis=1).reshape(N_MLSTM, 4 * ML_H)
    return {
        'x_prompt': nrm((BATCH, SEQ, D_MODEL), 1.0),
        'x_sample': nrm((DEC_BATCH, DEC_SEQ, D_MODEL), 1.0),
        'state_mlstm_C': nrm((DEC_BATCH, N_MLSTM, 2, ML_H, ML_DK, ML_DV), 0.1),
        'state_mlstm_n': nrm((DEC_BATCH, N_MLSTM, 2, ML_H, ML_DK), 0.1),
        'state_mlstm_m': nrm((DEC_BATCH, N_MLSTM, 2, ML_H), 1.0),
        'cache_attn_k': nrm((DEC_BATCH, N_ATTN, PAST_LEN, AT_KV, AT_HD), 1.0),
        'cache_attn_v': nrm((DEC_BATCH, N_ATTN, PAST_LEN, AT_KV, AT_HD), 1.0),
        'state_hgrn_S': nrm((DEC_BATCH, N_HGRN, 2, HG_H, HG_DK, HG_DV), 0.5),
        'c': nrm((DEC_BATCH, D_MODEL), 1.0),
        'c_ctx': nrm((D_MODEL,), 1.0),
        'mod_w': nrm((DEPTH, D_MODEL, 6 * D_MODEL), 0.5 * s),
        'mod_b': nrm((DEPTH, 6 * D_MODEL), 0.02),
        'ln_g': 1.0 + nrm((DEPTH, 2, D_MODEL), 0.02),
        'ln_b': nrm((DEPTH, 2, D_MODEL), 0.02),
        'mlstm_w_in': nrm((N_MLSTM, D_MODEL, ML_IN), s),
        'mlstm_gate_b': gb,
        'mlstm_conv': nrm((N_MLSTM, ML_CONV, 2 * ML_QK), ML_CONV ** -0.5),
        'mlstm_norm': 1.0 + nrm((N_MLSTM, ML_V), 0.02),
        'mlstm_w_out': nrm((N_MLSTM, ML_V, D_MODEL), ML_V ** -0.5 * DEEPNORM_BETA),
        'attn_w_in': nrm((N_ATTN, D_MODEL, AT_IN), s),
        'attn_q_norm': 1.0 + nrm((N_ATTN, AT_HD), 0.02),
        'attn_k_norm': 1.0 + nrm((N_ATTN, AT_HD), 0.02),
        'attn_w_out': nrm((N_ATTN, AT_H * AT_HD, D_MODEL), (AT_H * AT_HD) ** -0.5 * DEEPNORM_BETA),
        'hgrn_w_in': nrm((N_HGRN, D_MODEL, HG_IN), s),
        'hgrn_f_b': nrm((N_HGRN, 2 * HG_K), 0.1),
        'hgrn_lower_bounds': nrm((DEPTH, HG_K), 0.1),
        'hgrn_norm': 1.0 + nrm((N_HGRN, HG_V), 0.02),
        'hgrn_w_out': nrm((N_HGRN, HG_V, D_MODEL), HG_V ** -0.5 * DEEPNORM_BETA),
        'moe_router': nrm((DEPTH, D_MODEL, N_EXPERTS), s),
        'moe_router_b': nrm((DEPTH, N_EXPERTS), 0.01),
        'moe_w_gate': nrm((DEPTH, N_EXPERTS, D_MODEL, D_EXPERT), s),
        'moe_w_up': nrm((DEPTH, N_EXPERTS, D_MODEL, D_EXPERT), s),
        'moe_w_down': nrm((DEPTH, N_EXPERTS, D_EXPERT, D_MODEL), D_EXPERT ** -0.5 * DEEPNORM_BETA),
        'moe_sh_gate': nrm((DEPTH, D_MODEL, D_SHARED), s),
        'moe_sh_up': nrm((DEPTH, D_MODEL, D_SHARED), s),
        'moe_sh_down': nrm((DEPTH, D_SHARED, D_MODEL), D_SHARED ** -0.5 * DEEPNORM_BETA),
    }


def reference(x_prompt, x_sample, state_mlstm_C, state_mlstm_n, state_mlstm_m, cache_attn_k, cache_attn_v,
              state_hgrn_S, c, c_ctx, mod_w, mod_b, ln_g, ln_b, mlstm_w_in, mlstm_gate_b, mlstm_conv, mlstm_norm,
              mlstm_w_out, attn_w_in, attn_q_norm, attn_k_norm, attn_w_out, hgrn_w_in, hgrn_f_b, hgrn_lower_bounds,
              hgrn_norm, hgrn_w_out, moe_router, moe_router_b, moe_w_gate, moe_w_up, moe_w_down, moe_sh_gate,
              moe_sh_up, moe_sh_down):
    f32 = jnp.float32
    lb_all = jax.nn.softmax(hgrn_lower_bounds.astype(f32), axis=0)
    lb_all = jnp.cumsum(lb_all, axis=0) - lb_all[0]
    cond_ctx = jax.nn.silu(c_ctx)[None]
    cond_lat = jax.nn.silu(c)
    xp, xs = x_prompt, x_sample
    bp = xp.shape[0]
    new_C, new_n, new_m, new_k, new_v, new_S = [], [], [], [], [], []
    for l in range(DEPTH):
        kind, j = l % N_MIXERS, l // N_MIXERS
        mod_p = cond_ctx @ mod_w[l] + mod_b[l]
        mod_s = cond_lat @ mod_w[l] + mod_b[l]
        hp, hs = modulate(xp, mod_p, 0), modulate(xs, mod_s, 0)
        if kind == 0:
            zC = jnp.zeros((bp, 2, ML_H, ML_DK, ML_DV), f32)
            zn = jnp.zeros((bp, 2, ML_H, ML_DK), f32)
            zm = jnp.zeros((bp, 2, ML_H), f32)
            w = (mlstm_w_in[j], mlstm_gate_b[j], mlstm_conv[j], mlstm_norm[j], mlstm_w_out[j])
            yp, (Cn, nn_, mn) = mlstm_mixer(hp, zC, zn, zm, *w)
            ys, _ = mlstm_mixer(hs, state_mlstm_C[:, j], state_mlstm_n[:, j], state_mlstm_m[:, j], *w)
            new_C.append(Cn)
            new_n.append(nn_)
            new_m.append(mn)
        elif kind == 1:
            w = (attn_w_in[j], attn_q_norm[j], attn_k_norm[j], attn_w_out[j])
            yp, kn, vn = attn_context(hp, *w)
            ys = attn_latent(hs, cache_attn_k[:, j], cache_attn_v[:, j], *w)
            new_k.append(kn)
            new_v.append(vn)
        else:
            zS = jnp.zeros((bp, 2, HG_H, HG_DK, HG_DV), f32)
            yp, Sn = hgrn_mixer(hp, zS, hgrn_w_in[j], hgrn_f_b[j], lb_all[l], hgrn_norm[j], hgrn_w_out[j])
            ys, _ = hgrn_mixer(hs, state_hgrn_S[:, j], hgrn_w_in[j], hgrn_f_b[j], lb_all[l], hgrn_norm[j], hgrn_w_out[j])
            new_S.append(Sn)
        xp = post_residual(xp, yp, mod_p, 0, ln_g[l, 0], ln_b[l, 0])
        xs = post_residual(xs, ys, mod_s, 0, ln_g[l, 0], ln_b[l, 0])
        mw = (moe_router[l], moe_router_b[l], moe_w_gate[l], moe_w_up[l], moe_w_down[l],
              moe_sh_gate[l], moe_sh_up[l], moe_sh_down[l])
        xp = post_residual(xp, moe(modulate(xp, mod_p, 1), *mw), mod_p, 1, ln_g[l, 1], ln_b[l, 1])
        xs = post_residual(xs, moe(modulate(xs, mod_s, 1), *mw), mod_s, 1, ln_g[l, 1], ln_b[l, 1])
    return (xp, xs, jnp.stack(new_C, 1), jnp.stack(new_n, 1), jnp.stack(new_m, 1),
            jnp.stack(new_k, 1), jnp.stack(new_v, 1), jnp.stack(new_S, 1))
```

```python
import functools

import jax
import jax.numpy as jnp
from jax import lax
from jax.experimental import pallas as pl
from jax.experimental.pallas import tpu as pltpu

F32 = jnp.float32
BF16 = jnp.bfloat16
I32 = jnp.int32
HI = lax.Precision.HIGHEST

D = 1024
BATCH, SEQ = 16, 256
DEPTH = 4
DEC_BATCH, DEC_SEQ = 2, 4096
PAST = 512
GRID_W = 64
GROUP = 4096
N_GROUPS = 3
T_ALL = N_GROUPS * GROUP

ML_H, ML_DK, ML_DV, ML_L = 4, 128, 256, 128
ML_QK, ML_V = ML_H * ML_DK, ML_H * ML_DV
AT_H, AT_KV, AT_HD, AT_G = 16, 4, 64, 4
HG_H, HG_DK, HG_DV, HG_L = 8, 128, 128, 64
HG_K, HG_V = HG_H * HG_DK, HG_H * HG_DV
N_EXP, TOP_K, D_EXP, D_SH = 64, 8, 256, 256
ROUTE_SCALE = 2.5
ALPHA = (2 * DEPTH) ** 0.25
EPS = 1e-6
ROPE_THETA = 10000.0

VMEM_LIMIT = 56 * 1024 * 1024
LANES = 128

MOE_M = 256
MOE_NB = GROUP * TOP_K // MOE_M + N_EXP
MOE_ROWS = MOE_NB * MOE_M
MOE_S = MOE_M + 8
MOE_U = 8


def _cparams(sem):
    return pltpu.CompilerParams(dimension_semantics=sem, vmem_limit_bytes=VMEM_LIMIT)


def _sigmoid(x):
    return 1.0 / (1.0 + jnp.exp(-x))


def _silu(x):
    return x * _sigmoid(x)


def _log_sigmoid(x):
    return jnp.minimum(x, 0.0) - jnp.log(1.0 + jnp.exp(-jnp.abs(x)))


def _dot(a, b):
    return jnp.dot(a, b, preferred_element_type=F32)


def _dot_nt(a, b):
    return lax.dot_general(a, b, (((1,), (1,)), ((), ())), preferred_element_type=F32)


def _dot_tn(a, b):
    return lax.dot_general(a, b, (((0,), (0,)), ((), ())), preferred_element_type=F32)


def _dot_hi(a, b):
    return jnp.dot(a, b, preferred_element_type=F32, precision=HI)


def _mod_kernel(cond_ref, w_ref, b_ref, o_ref):
    o_ref[...] = _dot_hi(_silu(cond_ref[...]), w_ref[...]) + b_ref[...]


def _mod_all(cond8, mod_w, mod_b):
    tn = 1024
    out = pl.pallas_call(
        _mod_kernel,
        grid=(DEPTH, 6 * D // tn),
        in_specs=[pl.BlockSpec((8, D), lambda l, j: (0, 0)),
                  pl.BlockSpec((None, D, tn), lambda l, j: (l, 0, j)),
                  pl.BlockSpec((None, 1, tn), lambda l, j: (l, 0, j))],
        out_specs=pl.BlockSpec((None, 8, tn), lambda l, j: (l, 0, j)),
        out_shape=jax.ShapeDtypeStruct((DEPTH, 8, 6 * D), F32),
        compiler_params=_cparams(("arbitrary", "arbitrary")),
        name="mod_rows",
    )(cond8, mod_w, mod_b.reshape(DEPTH, 1, 6 * D))
    return out.reshape(DEPTH, 8, 6, D)


def _proj_kernel(x_ref, mod_ref, w_ref, o_ref, h_scr):
    @pl.when(pl.program_id(1) == 0)
    def _():
        h_scr[...] = (x_ref[...] * (1.0 + mod_ref[1:2, :]) + mod_ref[0:1, :]).astype(BF16)

    o_ref[...] = _dot(h_scr[...], w_ref[...].astype(BF16))


def _proj(x, mod4, layer, w3, widx, ncols, tm=1024, tn=512):
    T = x.shape[0]
    return pl.pallas_call(
        _proj_kernel,
        grid=(T // tm, ncols // tn),
        in_specs=[pl.BlockSpec((tm, D), lambda i, j: (i, 0)),
                  pl.BlockSpec((None, None, 6, D), lambda i, j: (layer, (i * tm) // GROUP, 0, 0)),
                  pl.BlockSpec((None, D, tn), lambda i, j: (widx, 0, j))],
        out_specs=pl.BlockSpec((tm, tn), lambda i, j: (i, j)),
        out_shape=jax.ShapeDtypeStruct((T, ncols), F32),
        scratch_shapes=[pltpu.VMEM((tm, D), BF16)],
        compiler_params=_cparams(("arbitrary", "arbitrary")),
        name="in_proj",
    )(x, mod4, w3)


def _proj_small_kernel(x_ref, mod_ref, w_ref, b_ref, o_ref):
    h = x_ref[...] * (1.0 + mod_ref[1:2, :]) + mod_ref[0:1, :]
    o_ref[...] = _dot_hi(h, w_ref[...]) + b_ref[...]


def _proj_small(x, mod4, layer, w, b, tm=1024):
    T, n = x.shape[0], w.shape[1]
    return pl.pallas_call(
        _proj_small_kernel,
        grid=(T // tm,),
        in_specs=[pl.BlockSpec((tm, D), lambda i: (i, 0)),
                  pl.BlockSpec((None, None, 6, D), lambda i: (layer, (i * tm) // GROUP, 0, 0)),
                  pl.BlockSpec((D, n), lambda i: (0, 0)),
                  pl.BlockSpec((1, n), lambda i: (0, 0))],
        out_specs=pl.BlockSpec((tm, n), lambda i: (i, 0)),
        out_shape=jax.ShapeDtypeStruct((T, n), F32),
        compiler_params=_cparams(("arbitrary",)),
        name="gate_proj",
    )(x, mod4, w, b.reshape(1, n))


def _layer_norm_rows(r, g, b):
    mu = jnp.mean(r, axis=-1, keepdims=True)
    c = r - mu
    var = jnp.mean(c * c, axis=-1, keepdims=True)
    return c * lax.rsqrt(var + EPS) * g + b


def _pack_bf16_pairs(lo, hi):
    lo_b = lax.bitcast_convert_type(lo.astype(BF16).astype(F32), I32)
    hi_b = lax.bitcast_convert_type(hi.astype(BF16).astype(F32), I32)
    return lax.shift_right_logical(lo_b, 16) | (hi_b & jnp.int32(-65536))


def _unpack_bf16_pairs(v):
    lo = lax.bitcast_convert_type(lax.shift_left(v, 16), F32)
    hi = lax.bitcast_convert_type(v & jnp.int32(-65536), F32)
    return lo.astype(BF16), hi.astype(BF16)


def _unpack_rows(ref, rows, stride, offset=0):
    chunks = []
    for r in range(4):
        if stride == 4:
            v = ref[pl.ds(r, rows, stride=4), :]
        else:
            v = ref[pl.ds(offset + r * stride, rows), :]
        chunks.extend(_unpack_bf16_pairs(v))
    return jnp.concatenate(chunks, axis=1)


def _post_kernel(a_ref, x_ref, mod_ref, w_ref, lng_ref, lnb_ref, rw_ref, rb_ref,
                 xo_ref, hp_ref, idx_ref, wt_ref, wb_scr, *, tm):
    @pl.when(pl.program_id(0) == 0)
    def _():
        wb_scr[...] = w_ref[...].astype(BF16)

    y = _dot(a_ref[...], wb_scr[...])
    r = ALPHA * x_ref[...] + mod_ref[2:3, :] * y
    xn = _layer_norm_rows(r, lng_ref[...], lnb_ref[...])
    xo_ref[...] = xn
    h1 = xn * (1.0 + mod_ref[4:5, :]) + mod_ref[3:4, :]
    for p in range(4):
        lo = h1[:, (2 * p) * LANES:(2 * p + 1) * LANES]
        hi = h1[:, (2 * p + 1) * LANES:(2 * p + 2) * LANES]
        hp_ref[pl.ds(p, tm, stride=4), :] = _pack_bf16_pairs(lo, hi)

    scores = _sigmoid(_dot_hi(h1, rw_ref[...]))
    sel = scores + rb_ref[...]
    e_iota = lax.broadcasted_iota(I32, (tm, N_EXP), 1)
    lane = lax.broadcasted_iota(I32, (tm, LANES), 1)
    idx_out = jnp.zeros((tm, LANES), I32)
    w_out = jnp.zeros((tm, LANES), F32)
    total = jnp.zeros((tm, 1), F32)
    for k in range(TOP_K):
        mx = jnp.max(sel, axis=-1, keepdims=True)
        ik = jnp.min(jnp.where(sel == mx, e_iota, N_EXP), axis=-1, keepdims=True)
        hit = e_iota == ik
        wk = jnp.sum(jnp.where(hit, scores, 0.0), axis=-1, keepdims=True)
        sel = jnp.where(hit, -jnp.inf, sel)
        total = total + wk
        idx_out = jnp.where(lane == k, ik, idx_out)
        w_out = jnp.where(lane == k, wk, w_out)
    idx_ref[...] = idx_out
    wt_ref[...] = w_out / total * ROUTE_SCALE


def _post(a, x, mod4, layer, w_out, ln_g, ln_b, router_w, router_b, tm=512):
    T = x.shape[0]
    return pl.pallas_call(
        functools.partial(_post_kernel, tm=tm),
        grid=(T // tm,),
        in_specs=[pl.BlockSpec((tm, D), lambda i: (i, 0)),
                  pl.BlockSpec((tm, D), lambda i: (i, 0)),
                  pl.BlockSpec((None, None, 6, D), lambda i: (layer, (i * tm) // GROUP, 0, 0)),
                  pl.BlockSpec((D, D), lambda i: (0, 0)),
                  pl.BlockSpec((1, D), lambda i: (0, 0)),
                  pl.BlockSpec((1, D), lambda i: (0, 0)),
                  pl.BlockSpec((D, N_EXP), lambda i: (0, 0)),
                  pl.BlockSpec((1, N_EXP), lambda i: (0, 0))],
        out_specs=[pl.BlockSpec((tm, D), lambda i: (i, 0)),
                   pl.BlockSpec((tm * 4, LANES), lambda i: (i, 0)),
                   pl.BlockSpec((tm, LANES), lambda i: (i, 0)),
                   pl.BlockSpec((tm, LANES), lambda i: (i, 0))],
        out_shape=[jax.ShapeDtypeStruct((T, D), F32),
                   jax.ShapeDtypeStruct((T * 4, LANES), I32),
                   jax.ShapeDtypeStruct((T, LANES), I32),
                   jax.ShapeDtypeStruct((T, LANES), F32)],
        scratch_shapes=[pltpu.VMEM((D, D), BF16)],
        compiler_params=_cparams(("arbitrary",)),
        name="out_proj_ln_router",
    )(a, x, mod4, w_out, ln_g.reshape(1, D), ln_b.reshape(1, D), router_w, router_b.reshape(1, N_EXP))


def _route_tables(idx128, w128):
    idx = idx128[:, :TOP_K].reshape(N_GROUPS, GROUP, TOP_K)
    w = w128[:, :TOP_K].reshape(N_GROUPS, GROUP, TOP_K)
    onehot = idx[..., None] == jnp.arange(N_EXP, dtype=I32)
    mask = jnp.any(onehot, axis=2).astype(I32)
    counts = jnp.sum(mask, axis=1)
    padded = (counts + MOE_M - 1) // MOE_M * MOE_M
    pad_end = jnp.cumsum(padded, axis=-1)
    pad_start = pad_end - padded
    rank = jnp.cumsum(mask, axis=1) - mask
    dest_e = pad_start[:, None, :] + rank
    dest = jnp.sum(jnp.where(onehot, dest_e[:, :, None, :], 0), axis=-1)
    flat = (dest + jnp.arange(N_GROUPS, dtype=I32)[:, None, None] * MOE_ROWS).reshape(-1)
    tok = jnp.broadcast_to(jnp.arange(GROUP, dtype=I32)[None, :, None], dest.shape).reshape(-1)
    row_tok = jnp.full((N_GROUPS * MOE_ROWS,), GROUP, I32).at[flat].set(tok, unique_indices=True)
    row_w = jnp.zeros((N_GROUPS * MOE_ROWS,), F32).at[flat].set(w.reshape(-1), unique_indices=True)
    n_used = pad_end[:, -1] // MOE_M
    starts = jnp.arange(MOE_NB, dtype=I32) * MOE_M
    block_e = jnp.sum((starts[None, :, None] >= pad_end[:, None, :]).astype(I32), axis=-1)
    block_e = jnp.minimum(block_e, N_EXP - 1)
    last_e = jnp.take_along_axis(block_e, jnp.maximum(n_used - 1, 0)[:, None], axis=1)
    block_e = jnp.where(jnp.arange(MOE_NB, dtype=I32)[None, :] < n_used[:, None], block_e, last_e)
    return block_e.reshape(-1).astype(I32), n_used.astype(I32), row_tok, row_w.reshape(-1, 1)


def _moe_kernel(be_ref, nu_ref, tok_ref, x_hbm, rw_ref, wg_ref, wu_ref, wd_ref, out_hbm,
                x_v, acc, tile, ys, wgb, wub, wdb, sem):
    s = pl.program_id(0)
    b = pl.program_id(1)
    blk = s * MOE_NB + b

    @pl.when(b == 0)
    def _():
        cp = pltpu.make_async_copy(x_hbm.at[s], x_v, sem.at[0])
        cp.start()
        cp.wait()

        def clear(i, carry):
            acc[pl.ds(pl.multiple_of(i * 8, 8), 8), :] = jnp.zeros((8, LANES), F32)
            return carry

        lax.fori_loop(0, GROUP + 1, clear, 0)

    @pl.when(b < nu_ref[s])
    def _():
        prev = be_ref[jnp.maximum(blk - 1, 0)]

        @pl.when((b == 0) | (be_ref[blk] != prev))
        def _():
            wgb[...] = wg_ref[...].astype(BF16)
            wub[...] = wu_ref[...].astype(BF16)
            wdb[...] = wd_ref[...].astype(BF16)

        for mi in range(MOE_M):
            t = jnp.minimum(tok_ref[0, mi], GROUP - 1)
            tile[pl.ds(mi, 4, stride=MOE_S), :] = x_v[pl.ds(pl.multiple_of(t * 4, 4), 4), :]
        xb = _unpack_rows(tile, MOE_M, MOE_S)
        g = _dot(xb, wgb[...])
        u = _dot(xb, wub[...])
        a = (_silu(g) * u) * rw_ref[...]
        y = _dot(a.astype(BF16), wdb[...])
        for c in range(D // LANES):
            ys[pl.ds(c * MOE_S, MOE_M), :] = y[:, c * LANES:(c + 1) * LANES]
        for m0 in range(0, MOE_M, MOE_U):
            offs = [pl.multiple_of(tok_ref[0, m0 + j] * 8, 8) for j in range(MOE_U)]
            vals = [acc[pl.ds(offs[j], 8), :] + ys[pl.ds(m0 + j, 8, stride=MOE_S), :] for j in range(MOE_U)]
            for j in range(MOE_U):
                acc[pl.ds(offs[j], 8), :] = vals[j]

    @pl.when(b == MOE_NB - 1)
    def _():
        cp = pltpu.make_async_copy(acc.at[pl.ds(0, GROUP * 8)], out_hbm.at[s], sem.at[1])
        cp.start()
        cp.wait()


def _moe_routed(hp, tables, layer, w_gate, w_up, w_down):
    block_e, n_used, row_tok, row_w = tables
    x3 = hp.reshape(N_GROUPS, GROUP * 4, LANES)
    wspec = lambda shape: pl.BlockSpec(
        (None, None) + shape, lambda s, b, be, nu: (layer, be[s * MOE_NB + b], 0, 0))
    out = pl.pallas_call(
        _moe_kernel,
        grid_spec=pltpu.PrefetchScalarGridSpec(
            num_scalar_prefetch=2,
            grid=(N_GROUPS, MOE_NB),
            in_specs=[pl.BlockSpec((None, 1, MOE_M), lambda s, b, be, nu: (s * MOE_NB + b, 0, 0),
                                   memory_space=pltpu.SMEM),
                      pl.BlockSpec(memory_space=pl.ANY),
                      pl.BlockSpec((MOE_M, 1), lambda s, b, be, nu: (s * MOE_NB + b, 0)),
                      wspec((D, D_EXP)), wspec((D, D_EXP)), wspec((D_EXP, D))],
            out_specs=pl.BlockSpec(memory_space=pl.ANY),
            scratch_shapes=[pltpu.VMEM((GROUP * 4, LANES), I32),
                            pltpu.VMEM(((GROUP + 1) * 8, LANES), F32),
                            pltpu.VMEM((4 * MOE_S, LANES), I32),
                            pltpu.VMEM((8 * MOE_S, LANES), F32),
                            pltpu.VMEM((D, D_EXP), BF16),
                            pltpu.VMEM((D, D_EXP), BF16),
                            pltpu.VMEM((D_EXP, D), BF16),
                            pltpu.SemaphoreType.DMA((2,))]),
        out_shape=jax.ShapeDtypeStruct((N_GROUPS, GROUP * 8, LANES), F32),
        compiler_params=_cparams(("arbitrary", "arbitrary")),
        name="moe_routed",
    )(block_e, n_used, row_tok.reshape(N_GROUPS * MOE_NB, 1, MOE_M), x3, row_w, w_gate, w_up, w_down)
    return out.reshape(T_ALL * 8, LANES)


def _moe_post_kernel(x_ref, hp_ref, r_ref, mod_ref, sg_ref, su_ref, sd_ref, lng_ref, lnb_ref, o_ref,
                     sgb, sub, sdb, *, tm):
    @pl.when(pl.program_id(0) == 0)
    def _():
        sgb[...] = sg_ref[...].astype(BF16)
        sub[...] = su_ref[...].astype(BF16)
        sdb[...] = sd_ref[...].astype(BF16)

    hb = _unpack_rows(hp_ref, tm, 4)
    g = _dot(hb, sgb[...])
    u = _dot(hb, sub[...])
    sh = _dot((_silu(g) * u).astype(BF16), sdb[...])
    routed = jnp.concatenate([r_ref[pl.ds(c, tm, stride=8), :] for c in range(D // LANES)], axis=1)
    r = ALPHA * x_ref[...] + mod_ref[5:6, :] * (routed + sh)
    o_ref[...] = _layer_norm_rows(r, lng_ref[...], lnb_ref[...])


def _moe_post(x, hp, routed, mod4, layer, sg, su, sd, ln_g, ln_b, tm=512):
    T = x.shape[0]
    return pl.pallas_call(
        functools.partial(_moe_post_kernel, tm=tm),
        grid=(T // tm,),
        in_specs=[pl.BlockSpec((tm, D), lambda i: (i, 0)),
                  pl.BlockSpec((tm * 4, LANES), lambda i: (i, 0)),
                  pl.BlockSpec((tm * 8, LANES), lambda i: (i, 0)),
                  pl.BlockSpec((None, None, 6, D), lambda i: (layer, (i * tm) // GROUP, 0, 0)),
                  pl.BlockSpec((D, D_SH), lambda i: (0, 0)),
                  pl.BlockSpec((D, D_SH), lambda i: (0, 0)),
                  pl.BlockSpec((D_SH, D), lambda i: (0, 0)),
                  pl.BlockSpec((1, D), lambda i: (0, 0)),
                  pl.BlockSpec((1, D), lambda i: (0, 0))],
        out_specs=pl.BlockSpec((tm, D), lambda i: (i, 0)),
        out_shape=jax.ShapeDtypeStruct((T, D), F32),
        scratch_shapes=[pltpu.VMEM((D, D_SH), BF16), pltpu.VMEM((D, D_SH), BF16), pltpu.VMEM((D_SH, D), BF16)],
        compiler_params=_cparams(("arbitrary",)),
        name="shared_expert_ln",
    )(x, hp, routed, mod4, sg, su, sd, ln_g.reshape(1, D), ln_b.reshape(1, D))


def _conv_silu(src_ref, w_ref, dst_ref, T, scale):
    L = ML_L
    nc = T // L
    w0, w1, w2 = w_ref[0:1, :], w_ref[1:2, :], w_ref[2:3, :]
    row = lax.broadcasted_iota(I32, (L, ML_DK), 0)

    def body(c, carry):
        r0 = pl.multiple_of(c * L, L)
        cur = src_ref[pl.ds(r0, L), :]
        p0 = pl.multiple_of(jnp.maximum(r0 - 8, 0), 8)
        n0 = pl.multiple_of(jnp.minimum(r0 + L, T - 8), 8)
        prev_row = src_ref[pl.ds(p0, 8), :][7:8, :] * jnp.where(c > 0, 1.0, 0.0).astype(F32)
        next_row = src_ref[pl.ds(n0, 8), :][0:1, :] * jnp.where(c < nc - 1, 1.0, 0.0).astype(F32)
        prev = jnp.where(row == 0, prev_row, pltpu.roll(cur, 1, axis=0))
        nxt = jnp.where(row == L - 1, next_row, pltpu.roll(cur, L - 1, axis=0))
        dst_ref[pl.ds(r0, L), :] = _silu(w0 * prev + w1 * cur + w2 * nxt) * scale
        return carry

    lax.fori_loop(0, nc, body, 0)


def _mlstm_chunk(q, k, v, G, GT, C, n, m, backward):
    L = ML_L
    ri = lax.broadcasted_iota(I32, (L, L), 0)
    ci = lax.broadcasted_iota(I32, (L, L), 1)
    keep = (ci >= ri) if backward else (ci <= ri)
    A = keep.astype(F32)
    AT = ((ri >= ci) if backward else (ri <= ci)).astype(F32)
    ic, fc = (2, 3) if backward else (0, 1)
    last = 0 if backward else L - 1
    b_col = _dot_hi(A, _log_sigmoid(G))[:, fc:fc + 1]
    b_row = _dot_hi(_log_sigmoid(GT), AT)[fc:fc + 1, :]
    ig_col = G[:, ic:ic + 1]
    ig_row = GT[ic:ic + 1, :]
    log_d = jnp.where(keep, b_col - b_row + ig_row, -jnp.inf)
    log_inter = b_col + m
    m_t = jnp.maximum(log_inter, jnp.max(log_d, axis=-1, keepdims=True))
    d = jnp.exp(log_d - m_t)
    w_inter = jnp.exp(log_inter - m_t)
    qb, kb, vb = q.astype(BF16), k.astype(BF16), v.astype(BF16)
    s = _dot_nt(qb, kb) * d
    num = _dot(s.astype(BF16), vb) + w_inter * _dot(qb, C.astype(BF16))
    den = jnp.sum(s, axis=-1, keepdims=True) + w_inter * jnp.sum(q * n, axis=-1, keepdims=True)
    h = num / jnp.maximum(jnp.abs(den), jnp.exp(-m_t))
    m_new = m_t[last:last + 1, :]
    w_last = jnp.exp(b_col[last:last + 1, :] - b_col + ig_col - m_new)
    decay = w_inter[last:last + 1, :]
    kw = k * w_last
    C_new = decay * C + _dot_tn(kw.astype(BF16), vb)
    n_new = decay * n + jnp.sum(kw, axis=0, keepdims=True)
    return h, C_new, n_new, m_new


def _mlstm_kernel(q_ref, k_ref, v_ref, og_ref, g_ref, gt_ref, cq_ref, ck_ref, ng_ref, c0_ref, n0_ref, m0_ref,
                  a_ref, c_out, n_out, m_out, qs, ks, hf, hb, cst, nst, mst, *, T):
    L = ML_L
    nc = T // L
    _conv_silu(q_ref, cq_ref, qs, T, ML_DK ** -0.5)
    _conv_silu(k_ref, ck_ref, ks, T, 1.0)
    cst[...] = c0_ref[...]
    nst[...] = n0_ref[...]
    mst[...] = m0_ref[...]

    def body(i, carry):
        for direction, out in ((0, hf), (1, hb)):
            c = (nc - 1 - i) if direction else i
            r0 = pl.multiple_of(c * L, L)
            h, C_new, n_new, m_new = _mlstm_chunk(
                qs[pl.ds(r0, L), :], ks[pl.ds(r0, L), :], v_ref[pl.ds(r0, L), :],
                g_ref[pl.ds(r0, L), :], gt_ref[c], cst[direction], nst[direction], mst[direction],
                backward=bool(direction))
            out[pl.ds(r0, L), :] = h
            cst[direction] = C_new
            nst[direction] = n_new
            mst[direction] = m_new
        return carry

    lax.fori_loop(0, nc, body, 0)
    c_out[...] = cst[...]
    n_out[...] = nst[...]
    m_out[...] = mst[...]

    def finish(c, carry):
        r0 = pl.multiple_of(c * L, L)
        tot = hf[pl.ds(r0, L), :] + hb[pl.ds(r0, L), :]
        mu = jnp.mean(tot, axis=-1, keepdims=True)
        cen = tot - mu
        var = jnp.mean(cen * cen, axis=-1, keepdims=True)
        hn = cen * lax.rsqrt(var + EPS) * ng_ref[...]
        a_ref[pl.ds(r0, L), :] = (hn * _sigmoid(og_ref[pl.ds(r0, L), :])).astype(BF16)
        return carry

    lax.fori_loop(0, nc, finish, 0)


def _mlstm_scan(z, gh, ght, conv_w, norm_g, C0, n0, m0, T, row_blk0, nseq):
    nc = T // ML_L
    qcol, kcol = 0, ML_QK // ML_DK
    vcol, ocol = 2 * ML_QK // ML_DV, (2 * ML_QK + ML_V) // ML_DV
    rb = lambda s: row_blk0 + s
    return pl.pallas_call(
        functools.partial(_mlstm_kernel, T=T),
        grid=(nseq, ML_H),
        in_specs=[pl.BlockSpec((T, ML_DK), lambda s, h: (rb(s), qcol + h)),
                  pl.BlockSpec((T, ML_DK), lambda s, h: (rb(s), kcol + h)),
                  pl.BlockSpec((T, ML_DV), lambda s, h: (rb(s), vcol + h)),
                  pl.BlockSpec((T, ML_DV), lambda s, h: (rb(s), ocol + h)),
                  pl.BlockSpec((None, T, 4), lambda s, h: (h, rb(s), 0)),
                  pl.BlockSpec((None, nc, 4, ML_L), lambda s, h: (h, rb(s), 0, 0)),
                  pl.BlockSpec((3, ML_DK), lambda s, h: (0, qcol + h)),
                  pl.BlockSpec((3, ML_DK), lambda s, h: (0, kcol + h)),
                  pl.BlockSpec((1, ML_DV), lambda s, h: (0, h)),
                  pl.BlockSpec((None, 2, None, ML_DK, ML_DV), lambda s, h: (s, 0, h, 0, 0)),
                  pl.BlockSpec((None, 2, None, 1, ML_DK), lambda s, h: (s, 0, h, 0, 0)),
                  pl.BlockSpec((None, 2, None, 1, 1), lambda s, h: (s, 0, h, 0, 0))],
        out_specs=[pl.BlockSpec((T, ML_DV), lambda s, h: (s, h)),
                   pl.BlockSpec((None, 2, None, ML_DK, ML_DV), lambda s, h: (s, 0, h, 0, 0)),
                   pl.BlockSpec((None, 2, None, 1, ML_DK), lambda s, h: (s, 0, h, 0, 0)),
                   pl.BlockSpec((None, 2, None, 1, 1), lambda s, h: (s, 0, h, 0, 0))],
        out_shape=[jax.ShapeDtypeStruct((nseq * T, ML_V), BF16),
                   jax.ShapeDtypeStruct((nseq, 2, ML_H, ML_DK, ML_DV), F32),
                   jax.ShapeDtypeStruct((nseq, 2, ML_H, 1, ML_DK), F32),
                   jax.ShapeDtypeStruct((nseq, 2, ML_H, 1, 1), F32)],
        scratch_shapes=[pltpu.VMEM((T, ML_DK), F32), pltpu.VMEM((T, ML_DK), F32),
                        pltpu.VMEM((T, ML_DV), F32), pltpu.VMEM((T, ML_DV), F32),
                        pltpu.VMEM((2, ML_DK, ML_DV), F32), pltpu.VMEM((2, 1, ML_DK), F32),
                        pltpu.VMEM((2, 1, 1), F32)],
        compiler_params=_cparams(("arbitrary", "arbitrary")),
        name="mlstm_scan",
    )(z, z, z, z, gh, ght, conv_w, conv_w, norm_g.reshape(1, ML_V), C0, n0, m0)


def _mlstm_layer(x, mod4, layer, j, w_in, gate_b, conv_w, norm_g, C_lat, n_lat, m_lat):
    z = _proj(x, mod4, layer, w_in, j, 2 * ML_QK + 2 * ML_V)
    gates = _proj_small(x, mod4, layer, w_in[j][:, 2 * ML_QK + 2 * ML_V:], gate_b[j])
    gh = gates.reshape(T_ALL, 4, ML_H).transpose(2, 0, 1)
    ght = gh.reshape(ML_H, T_ALL // ML_L, ML_L, 4).transpose(0, 1, 3, 2)
    zC = jnp.zeros((BATCH, 2, ML_H, ML_DK, ML_DV), F32)
    zn = jnp.zeros((BATCH, 2, ML_H, 1, ML_DK), F32)
    zm = jnp.zeros((BATCH, 2, ML_H, 1, 1), F32)
    a_p, Cn, nn, mn = _mlstm_scan(z, gh, ght, conv_w[j], norm_g[j], zC, zn, zm, SEQ, 0, BATCH)
    a_s, _, _, _ = _mlstm_scan(z, gh, ght, conv_w[j], norm_g[j], C_lat[:, j],
                               n_lat[:, j].reshape(DEC_BATCH, 2, ML_H, 1, ML_DK),
                               m_lat[:, j].reshape(DEC_BATCH, 2, ML_H, 1, 1),
                               DEC_SEQ, BATCH * SEQ // DEC_SEQ, DEC_BATCH)
    a = jnp.concatenate([a_p, a_s], axis=0)
    return a, (Cn, nn.reshape(BATCH, 2, ML_H, ML_DK), mn.reshape(BATCH, 2, ML_H))


def _hgrn_chunk(q, k, v, g, St, backward):
    L = HG_L
    ri = lax.broadcasted_iota(I32, (L, L), 0)
    ci = lax.broadcasted_iota(I32, (L, L), 1)
    keep = (ci >= ri) if backward else (ci <= ri)
    ref = L - 1 - L // 2 if backward else L // 2
    last = 0 if backward else L - 1
    b = _dot_hi(keep.astype(F32), g)
    b_ref = b[ref:ref + 1, :]
    b_last = b[last:last + 1, :]
    qe = (q * jnp.exp(b - b_ref)).astype(BF16)
    ke = (k * jnp.exp(b_ref - b)).astype(BF16)
    vb = v.astype(BF16)
    a = jnp.where(keep, _dot_nt(qe, ke), 0.0)
    o = _dot(a.astype(BF16), vb) + _dot_nt((q * jnp.exp(b)).astype(BF16), St.astype(BF16))
    kd = (k * jnp.exp(b_last - b)).astype(BF16)
    St_new = jnp.exp(b_last) * St + _dot_tn(vb, kd)
    return o, St_new


def _hgrn_kernel(q_ref, i_ref, ff_ref, fb_ref, og_ref, fbf_ref, fbb_ref, lbr_ref, ng_ref, s0_ref,
                 a_ref, s_out, of, ob, st, *, T, lb_layer):
    L = HG_L
    nc = T // L
    raw = lbr_ref[...]
    e = jnp.exp(raw - jnp.max(raw, axis=0, keepdims=True))
    p = e / jnp.sum(e, axis=0, keepdims=True)
    lb = jnp.sum(p[0:lb_layer + 1, :], axis=0, keepdims=True) - p[0:1, :]
    st[0] = s0_ref[0].T
    st[1] = s0_ref[1].T

    def body(i, carry):
        for direction, out, f_ref, b_ref in ((0, of, ff_ref, fbf_ref), (1, ob, fb_ref, fbb_ref)):
            c = (nc - 1 - i) if direction else i
            r0 = pl.multiple_of(c * L, L)
            f = lb + (1.0 - lb) * _sigmoid(f_ref[pl.ds(r0, L), :] + b_ref[...])
            o, St_new = _hgrn_chunk(_silu(q_ref[pl.ds(r0, L), :]), 1.0 - f, i_ref[pl.ds(r0, L), :],
                                    jnp.log(f), st[direction], backward=bool(direction))
            out[pl.ds(r0, L), :] = o
            st[direction] = St_new
        return carry

    lax.fori_loop(0, nc, body, 0)
    s_out[0] = st[0].T
    s_out[1] = st[1].T

    def finish(c, carry):
        r0 = pl.multiple_of(c * L, L)
        tot = of[pl.ds(r0, L), :] + ob[pl.ds(r0, L), :]
        on = tot * lax.rsqrt(jnp.mean(tot * tot, axis=-1, keepdims=True) + EPS) * ng_ref[...]
        a_ref[pl.ds(r0, L), :] = (on * _silu(og_ref[pl.ds(r0, L), :])).astype(BF16)
        return carry

    lax.fori_loop(0, nc, finish, 0)


def _hgrn_scan(z, f_b, lb_raw, norm_g, S0, T, row_blk0, nseq, lb_layer):
    nh = HG_H
    rb = lambda s: row_blk0 + s
    zspec = lambda cb: pl.BlockSpec((T, HG_DK), lambda s, h: (rb(s), cb * nh + h))
    return pl.pallas_call(
        functools.partial(_hgrn_kernel, T=T, lb_layer=lb_layer),
        grid=(nseq, nh),
        in_specs=[zspec(0), zspec(1), zspec(2), zspec(3), zspec(4),
                  pl.BlockSpec((1, HG_DK), lambda s, h: (0, h)),
                  pl.BlockSpec((1, HG_DK), lambda s, h: (0, nh + h)),
                  pl.BlockSpec((DEPTH, HG_DK), lambda s, h: (0, h)),
                  pl.BlockSpec((1, HG_DV), lambda s, h: (0, h)),
                  pl.BlockSpec((None, 2, None, HG_DK, HG_DV), lambda s, h: (s, 0, h, 0, 0))],
        out_specs=[pl.BlockSpec((T, HG_DV), lambda s, h: (s, h)),
                   pl.BlockSpec((None, 2, None, HG_DK, HG_DV), lambda s, h: (s, 0, h, 0, 0))],
        out_shape=[jax.ShapeDtypeStruct((nseq * T, HG_V), BF16),
                   jax.ShapeDtypeStruct((nseq, 2, nh, HG_DK, HG_DV), F32)],
        scratch_shapes=[pltpu.VMEM((T, HG_DV), F32), pltpu.VMEM((T, HG_DV), F32),
                        pltpu.VMEM((2, HG_DV, HG_DK), F32)],
        compiler_params=_cparams(("arbitrary", "arbitrary")),
        name="hgrn_scan",
    )(z, z, z, z, z, f_b.reshape(1, 2 * HG_K), f_b.reshape(1, 2 * HG_K), lb_raw, norm_g.reshape(1, HG_V), S0)


def _hgrn_layer(x, mod4, layer, j, w_in, f_b, lb_raw, norm_g, S_lat):
    z = _proj(x, mod4, layer, w_in, j, 3 * HG_K + 2 * HG_V)
    zS = jnp.zeros((BATCH, 2, HG_H, HG_DK, HG_DV), F32)
    a_p, Sn = _hgrn_scan(z, f_b[j], lb_raw, norm_g[j], zS, SEQ, 0, BATCH, layer)
    a_s, _ = _hgrn_scan(z, f_b[j], lb_raw, norm_g[j], S_lat[:, j], DEC_SEQ, BATCH * SEQ // DEC_SEQ, DEC_BATCH, layer)
    return jnp.concatenate([a_p, a_s], axis=0), Sn


def _head_rms(x, g_tiled, nheads):
    lane = lax.broadcasted_iota(I32, x.shape, 1)
    sq = x * x
    ms = jnp.zeros_like(x)
    for h in range(nheads):
        in_h = (lane >= h * AT_HD) & (lane < (h + 1) * AT_HD)
        tot = jnp.sum(jnp.where(in_h, sq, 0.0), axis=-1, keepdims=True)
        ms = jnp.where(in_h, tot, ms)
    return x * lax.rsqrt(ms * (1.0 / AT_HD) + EPS) * g_tiled


def _rope(x, cos, sin):
    w = x.shape[1]
    lane = lax.broadcasted_iota(I32, x.shape, 1)
    up = pltpu.roll(x, w - 16, axis=1)
    down = pltpu.roll(x, 16, axis=1)
    swapped = jnp.where((lane % 32) < 16, up, down)
    return x * cos + swapped * sin


def _attn_ctx_kernel(q_ref, k_ref, v_ref, qn_ref, kn_ref, o_ref, ko_ref, vo_ref):
    k = _head_rms(k_ref[...], kn_ref[...], AT_KV)
    ko_ref[...] = k
    v = v_ref[...]
    vo_ref[...] = v
    kb, vb = k.astype(BF16), v.astype(BF16)
    outs = []
    for kv in range(AT_KV):
        q = _head_rms(q_ref[:, kv * 256:(kv + 1) * 256], qn_ref[...], AT_G) * (AT_HD ** -0.5)
        kh = kb[:, kv * AT_HD:(kv + 1) * AT_HD]
        vh = vb[:, kv * AT_HD:(kv + 1) * AT_HD]
        for g in range(AT_G):
            s = _dot_nt(q[:, g * AT_HD:(g + 1) * AT_HD].astype(BF16), kh)
            p = jnp.exp(s - jnp.max(s, axis=-1, keepdims=True))
            p = p / jnp.sum(p, axis=-1, keepdims=True)
            outs.append(_dot(p.astype(BF16), vh))
    o_ref[...] = jnp.concatenate(outs, axis=1).astype(BF16)


def _attn_ctx(z, q_norm, k_norm):
    qn = jnp.tile(q_norm, AT_G).reshape(1, 256)
    kn = jnp.tile(k_norm, AT_KV).reshape(1, 256)
    return pl.pallas_call(
        _attn_ctx_kernel,
        grid=(BATCH,),
        in_specs=[pl.BlockSpec((SEQ, D), lambda b: (b, 0)),
                  pl.BlockSpec((SEQ, 256), lambda b: (b, 4)),
                  pl.BlockSpec((SEQ, 256), lambda b: (b, 5)),
                  pl.BlockSpec((1, 256), lambda b: (0, 0)),
                  pl.BlockSpec((1, 256), lambda b: (0, 0))],
        out_specs=[pl.BlockSpec((SEQ, D), lambda b: (b, 0)),
                   pl.BlockSpec((SEQ, 256), lambda b: (b, 0)),
                   pl.BlockSpec((SEQ, 256), lambda b: (b, 0))],
        out_shape=[jax.ShapeDtypeStruct((BATCH * SEQ, D), BF16),
                   jax.ShapeDtypeStruct((BATCH * SEQ, 256), F32),
                   jax.ShapeDtypeStruct((BATCH * SEQ, 256), F32)],
        compiler_params=_cparams(("arbitrary",)),
        name="attn_context",
    )(z, z, z, qn, kn)


def _attn_kv_kernel(k_ref, v_ref, kn_ref, cos_ref, sin_ref, ko_ref, vo_ref):
    k = _rope(_head_rms(k_ref[...], kn_ref[...], AT_KV), cos_ref[...], sin_ref[...])
    v = v_ref[...]
    for h in range(AT_KV):
        ko_ref[h] = k[:, h * AT_HD:(h + 1) * AT_HD].astype(BF16)
        vo_ref[h] = v[:, h * AT_HD:(h + 1) * AT_HD].astype(BF16)


def _attn_kv(z, k_norm, cos4, sin4, tt=512):
    kn = jnp.tile(k_norm, AT_KV).reshape(1, 256)
    nt = DEC_SEQ // tt
    row0 = BATCH * SEQ // tt
    return pl.pallas_call(
        _attn_kv_kernel,
        grid=(DEC_BATCH, nt),
        in_specs=[pl.BlockSpec((tt, 256), lambda b, i: (row0 + b * nt + i, 4)),
                  pl.BlockSpec((tt, 256), lambda b, i: (row0 + b * nt + i, 5)),
                  pl.BlockSpec((1, 256), lambda b, i: (0, 0)),
                  pl.BlockSpec((tt, 256), lambda b, i: (i, 0)),
                  pl.BlockSpec((tt, 256), lambda b, i: (i, 0))],
        out_specs=[pl.BlockSpec((None, AT_KV, tt, AT_HD), lambda b, i: (b, 0, i, 0)),
                   pl.BlockSpec((None, AT_KV, tt, AT_HD), lambda b, i: (b, 0, i, 0))],
        out_shape=[jax.ShapeDtypeStruct((DEC_BATCH, AT_KV, DEC_SEQ, AT_HD), BF16),
                   jax.ShapeDtypeStruct((DEC_BATCH, AT_KV, DEC_SEQ, AT_HD), BF16)],
        compiler_params=_cparams(("arbitrary", "arbitrary")),
        name="attn_kv_prep",
    )(z, z, kn, cos4, sin4)


def _attn_lat_kernel(q_ref, qn_ref, cos_ref, sin_ref, k_ref, v_ref, o_ref, *, tq, tk):
    q = _rope(_head_rms(q_ref[...], qn_ref[...], AT_G), cos_ref[...], sin_ref[...]) * (AT_HD ** -0.5)
    qs = jnp.concatenate([q[:, g * AT_HD:(g + 1) * AT_HD] for g in range(AT_G)], axis=0).astype(BF16)
    rows = AT_G * tq
    nk = k_ref.shape[0] // tk

    def body(j, carry):
        m, l, acc = carry
        k0 = pl.multiple_of(j * tk, tk)
        s = _dot_nt(qs, k_ref[pl.ds(k0, tk), :])
        m_new = jnp.maximum(m, jnp.max(s, axis=-1, keepdims=True))
        alpha = jnp.exp(m - m_new)
        p = jnp.exp(s - m_new)
        l = alpha * l + jnp.sum(p, axis=-1, keepdims=True)
        acc = alpha * acc + _dot(p.astype(BF16), v_ref[pl.ds(k0, tk), :])
        return m_new, l, acc

    init = (jnp.full((rows, 1), -jnp.inf, F32), jnp.zeros((rows, 1), F32), jnp.zeros((rows, AT_HD), F32))
    _, l, acc = lax.fori_loop(0, nk, body, init)
    o = acc / l
    o_ref[...] = jnp.concatenate([o[g * tq:(g + 1) * tq, :] for g in range(AT_G)], axis=1).astype(BF16)


def _attn_lat(z, q_norm, cos4, sin4, kk, vv, tq=256, tk=512):
    qn = jnp.tile(q_norm, AT_G).reshape(1, 256)
    nq = DEC_SEQ // tq
    row0 = BATCH * SEQ // tq
    skv = kk.shape[2]
    return pl.pallas_call(
        functools.partial(_attn_lat_kernel, tq=tq, tk=tk),
        grid=(DEC_BATCH, AT_KV, nq),
        in_specs=[pl.BlockSpec((tq, 256), lambda b, h, i: (row0 + b * nq + i, h)),
                  pl.BlockSpec((1, 256), lambda b, h, i: (0, 0)),
                  pl.BlockSpec((tq, 256), lambda b, h, i: (i, 0)),
                  pl.BlockSpec((tq, 256), lambda b, h, i: (i, 0)),
                  pl.BlockSpec((None, None, skv, AT_HD), lambda b, h, i: (b, h, 0, 0)),
                  pl.BlockSpec((None, None, skv, AT_HD), lambda b, h, i: (b, h, 0, 0))],
        out_specs=pl.BlockSpec((tq, 256), lambda b, h, i: (b * nq + i, h)),
        out_shape=jax.ShapeDtypeStruct((DEC_BATCH * DEC_SEQ, D), BF16),
        compiler_params=_cparams(("arbitrary", "arbitrary", "arbitrary")),
        name="attn_latent",
    )(z, qn, cos4, sin4, kk, vv)


def _rope_tables():
    t = jnp.arange(DEC_SEQ)
    row = (t // GRID_W).astype(F32)
    col = (t % GRID_W).astype(F32)
    nf = AT_HD // 4
    inv = ROPE_THETA ** (-jnp.arange(nf, dtype=F32) / nf)
    ar, ac = row[:, None] * inv[None], col[:, None] * inv[None]
    cos = jnp.concatenate([jnp.cos(ar), jnp.cos(ar), jnp.cos(ac), jnp.cos(ac)], axis=1)
    sin = jnp.concatenate([-jnp.sin(ar), jnp.sin(ar), -jnp.sin(ac), jnp.sin(ac)], axis=1)
    return jnp.tile(cos, (1, 4)), jnp.tile(sin, (1, 4))


def _attn_layer(x, mod4, layer, j, w_in, q_norm, k_norm, cache_k, cache_v):
    z = _proj(x, mod4, layer, w_in, j, (AT_H + 2 * AT_KV) * AT_HD)
    a_p, k_new, v_new = _attn_ctx(z, q_norm[j], k_norm[j])
    cos4, sin4 = _rope_tables()
    k_lat, v_lat = _attn_kv(z, k_norm[j], cos4, sin4)
    kk = jnp.concatenate([cache_k[:, j].transpose(0, 2, 1, 3).astype(BF16), k_lat], axis=2)
    vv = jnp.concatenate([cache_v[:, j].transpose(0, 2, 1, 3).astype(BF16), v_lat], axis=2)
    a_s = _attn_lat(z, q_norm[j], cos4, sin4, kk, vv)
    a = jnp.concatenate([a_p, a_s], axis=0)
    return a, (k_new.reshape(BATCH, SEQ, AT_KV, AT_HD), v_new.reshape(BATCH, SEQ, AT_KV, AT_HD))


def kernel(x_prompt, x_sample, state_mlstm_C, state_mlstm_n, state_mlstm_m, cache_attn_k, cache_attn_v, state_hgrn_S, c, c_ctx, mod_w, mod_b, ln_g, ln_b, mlstm_w_in, mlstm_gate_b, mlstm_conv, mlstm_norm, mlstm_w_out, attn_w_in, attn_q_norm, attn_k_norm, attn_w_out, hgrn_w_in, hgrn_f_b, hgrn_lower_bounds, hgrn_norm, hgrn_w_out, moe_router, moe_router_b, moe_w_gate, moe_w_up, moe_w_down, moe_sh_gate, moe_sh_up, moe_sh_down):
    x = jnp.concatenate([x_prompt.reshape(BATCH * SEQ, D), x_sample.reshape(DEC_BATCH * DEC_SEQ, D)], axis=0)
    cond8 = jnp.concatenate([c_ctx[None], c, jnp.zeros((8 - 1 - DEC_BATCH, D), F32)], axis=0)
    mod4 = _mod_all(cond8, mod_w, mod_b)

    new_C, new_n, new_m, new_k, new_v, new_S = [], [], [], [], [], []
    for l in range(DEPTH):
        kind, j = l % 3, l // 3
        if kind == 0:
            a, (Cn, nn, mn) = _mlstm_layer(x, mod4, l, j, mlstm_w_in, mlstm_gate_b, mlstm_conv, mlstm_norm,
                                           state_mlstm_C, state_mlstm_n, state_mlstm_m)
            new_C.append(Cn)
            new_n.append(nn)
            new_m.append(mn)
            w_out = mlstm_w_out[j]
        elif kind == 1:
            a, (kn, vn) = _attn_layer(x, mod4, l, j, attn_w_in, attn_q_norm, attn_k_norm, cache_attn_k, cache_attn_v)
            new_k.append(kn)
            new_v.append(vn)
            w_out = attn_w_out[j]
        else:
            a, Sn = _hgrn_layer(x, mod4, l, j, hgrn_w_in, hgrn_f_b, hgrn_lower_bounds, hgrn_norm, state_hgrn_S)
            new_S.append(Sn)
            w_out = hgrn_w_out[j]
        x, hp, idx128, w128 = _post(a, x, mod4, l, w_out, ln_g[l, 0], ln_b[l, 0], moe_router[l], moe_router_b[l])
        routed = _moe_routed(hp, _route_tables(idx128, w128), l, moe_w_gate, moe_w_up, moe_w_down)
        x = _moe_post(x, hp, routed, mod4, l, moe_sh_gate[l], moe_sh_up[l], moe_sh_down[l], ln_g[l, 1], ln_b[l, 1])

    xp = x[:BATCH * SEQ].reshape(BATCH, SEQ, D)
    xs = x[BATCH * SEQ:].reshape(DEC_BATCH, DEC_SEQ, D)
    return (xp, xs, jnp.stack(new_C, 1), jnp.stack(new_n, 1), jnp.stack(new_m, 1),
            jnp.stack(new_k, 1), jnp.stack(new_v, 1), jnp.stack(new_S, 1))
```

```python
import functools

import jax
import jax.numpy as jnp
from jax import lax
from jax.experimental import pallas as pl
from jax.experimental.pallas import tpu as pltpu

F32 = jnp.float32
BF16 = jnp.bfloat16
I32 = jnp.int32
HI = lax.Precision.HIGHEST

D = 1024
BATCH, SEQ = 16, 256
DEPTH = 4
DEC_BATCH, DEC_SEQ = 2, 4096
PAST = 512
GRID_W = 64
GROUP = 4096
N_GROUPS = 3
T_ALL = N_GROUPS * GROUP

ML_H, ML_DK, ML_DV, ML_L = 4, 128, 256, 128
ML_QK, ML_V = ML_H * ML_DK, ML_H * ML_DV
AT_H, AT_KV, AT_HD, AT_G = 16, 4, 64, 4
HG_H, HG_DK, HG_DV, HG_L = 8, 128, 128, 64
HG_K, HG_V = HG_H * HG_DK, HG_H * HG_DV
N_EXP, TOP_K, D_EXP, D_SH = 64, 8, 256, 256
ROUTE_SCALE = 2.5
ALPHA = (2 * DEPTH) ** 0.25
EPS = 1e-6
ROPE_THETA = 10000.0

VMEM_LIMIT = 56 * 1024 * 1024
LANES = 128

MOE_M = 256
MOE_NB = GROUP * TOP_K // MOE_M + N_EXP
MOE_ROWS = MOE_NB * MOE_M
MOE_S = MOE_M + 8
MOE_U = 8
SCAN_UNROLL = 2


def _cparams(sem):
    return pltpu.CompilerParams(dimension_semantics=sem, vmem_limit_bytes=VMEM_LIMIT)


def _sigmoid(x):
    return 1.0 / (1.0 + jnp.exp(-x))


def _silu(x):
    return x * _sigmoid(x)


def _log_sigmoid(x):
    return jnp.minimum(x, 0.0) - jnp.log(1.0 + jnp.exp(-jnp.abs(x)))


def _dot(a, b):
    return jnp.dot(a, b, preferred_element_type=F32)


def _dot_nt(a, b):
    return lax.dot_general(a, b, (((1,), (1,)), ((), ())), preferred_element_type=F32)


def _dot_tn(a, b):
    return lax.dot_general(a, b, (((0,), (0,)), ((), ())), preferred_element_type=F32)


def _dot_hi(a, b):
    return jnp.dot(a, b, preferred_element_type=F32, precision=HI)


def _split3(x):
    p0 = x.astype(BF16)
    r1 = x - p0.astype(F32)
    p1 = r1.astype(BF16)
    p2 = (r1 - p1.astype(F32)).astype(BF16)
    return p0, p1, p2


def _mask_dot(mask_bf16, x):
    p0, p1, p2 = _split3(x)
    return _dot(mask_bf16, p0) + _dot(mask_bf16, p1) + _dot(mask_bf16, p2)


def _dot_mask(x, mask_bf16):
    p0, p1, p2 = _split3(x)
    return _dot(p0, mask_bf16) + _dot(p1, mask_bf16) + _dot(p2, mask_bf16)


def _dot_3x(a, b):
    a0 = a.astype(BF16)
    a1 = (a - a0.astype(F32)).astype(BF16)
    b0 = b.astype(BF16)
    b1 = (b - b0.astype(F32)).astype(BF16)
    return _dot(a0, b0) + _dot(a0, b1) + _dot(a1, b0)


def _mod_kernel(cond_ref, w_ref, b_ref, o_ref):
    o_ref[...] = _dot_hi(_silu(cond_ref[...]), w_ref[...]) + b_ref[...]


def _mod_all(cond8, mod_w, mod_b):
    tn = 1024
    out = pl.pallas_call(
        _mod_kernel,
        grid=(DEPTH, 6 * D // tn),
        in_specs=[pl.BlockSpec((8, D), lambda l, j: (0, 0)),
                  pl.BlockSpec((None, D, tn), lambda l, j: (l, 0, j)),
                  pl.BlockSpec((None, 1, tn), lambda l, j: (l, 0, j))],
        out_specs=pl.BlockSpec((None, 8, tn), lambda l, j: (l, 0, j)),
        out_shape=jax.ShapeDtypeStruct((DEPTH, 8, 6 * D), F32),
        compiler_params=_cparams(("arbitrary", "arbitrary")),
        name="mod_rows",
    )(cond8, mod_w, mod_b.reshape(DEPTH, 1, 6 * D))
    return out.reshape(DEPTH, 8, 6, D)


def _proj_kernel(x_ref, mod_ref, w_ref, o_ref, h_scr):
    @pl.when(pl.program_id(1) == 0)
    def _():
        h_scr[...] = (x_ref[...] * (1.0 + mod_ref[1:2, :]) + mod_ref[0:1, :]).astype(BF16)

    o_ref[...] = _dot(h_scr[...], w_ref[...].astype(BF16))


def _proj(x, mod4, layer, w3, widx, ncols, tm=1024, tn=512):
    T = x.shape[0]
    return pl.pallas_call(
        _proj_kernel,
        grid=(T // tm, ncols // tn),
        in_specs=[pl.BlockSpec((tm, D), lambda i, j: (i, 0)),
                  pl.BlockSpec((None, None, 6, D), lambda i, j: (layer, (i * tm) // GROUP, 0, 0)),
                  pl.BlockSpec((None, D, tn), lambda i, j: (widx, 0, j))],
        out_specs=pl.BlockSpec((tm, tn), lambda i, j: (i, j)),
        out_shape=jax.ShapeDtypeStruct((T, ncols), F32),
        scratch_shapes=[pltpu.VMEM((tm, D), BF16)],
        compiler_params=_cparams(("arbitrary", "arbitrary")),
        name="in_proj",
    )(x, mod4, w3)


def _proj_small_kernel(x_ref, mod_ref, w_ref, b_ref, o_ref):
    h = x_ref[...] * (1.0 + mod_ref[1:2, :]) + mod_ref[0:1, :]
    o_ref[...] = _dot_hi(h, w_ref[...]) + b_ref[...]


def _proj_small(x, mod4, layer, w, b, tm=1024):
    T, n = x.shape[0], w.shape[1]
    return pl.pallas_call(
        _proj_small_kernel,
        grid=(T // tm,),
        in_specs=[pl.BlockSpec((tm, D), lambda i: (i, 0)),
                  pl.BlockSpec((None, None, 6, D), lambda i: (layer, (i * tm) // GROUP, 0, 0)),
                  pl.BlockSpec((D, n), lambda i: (0, 0)),
                  pl.BlockSpec((1, n), lambda i: (0, 0))],
        out_specs=pl.BlockSpec((tm, n), lambda i: (i, 0)),
        out_shape=jax.ShapeDtypeStruct((T, n), F32),
        compiler_params=_cparams(("arbitrary",)),
        name="gate_proj",
    )(x, mod4, w, b.reshape(1, n))


def _layer_norm_rows(r, g, b):
    mu = jnp.mean(r, axis=-1, keepdims=True)
    c = r - mu
    var = jnp.mean(c * c, axis=-1, keepdims=True)
    return c * lax.rsqrt(var + EPS) * g + b


def _pack_bf16_pairs(lo, hi):
    lo_b = lax.bitcast_convert_type(lo.astype(BF16).astype(F32), I32)
    hi_b = lax.bitcast_convert_type(hi.astype(BF16).astype(F32), I32)
    return lax.shift_right_logical(lo_b, 16) | (hi_b & jnp.int32(-65536))


def _unpack_bf16_pairs(v):
    lo = lax.bitcast_convert_type(lax.shift_left(v, 16), F32)
    hi = lax.bitcast_convert_type(v & jnp.int32(-65536), F32)
    return lo.astype(BF16), hi.astype(BF16)


def _unpack_rows(ref, rows, stride, offset=0):
    chunks = []
    for r in range(4):
        if stride == 4:
            v = ref[pl.ds(r, rows, stride=4), :]
        else:
            v = ref[pl.ds(offset + r * stride, rows), :]
        chunks.extend(_unpack_bf16_pairs(v))
    return jnp.concatenate(chunks, axis=1)


def _post_kernel(a_ref, x_ref, mod_ref, w_ref, lng_ref, lnb_ref, rw_ref, rb_ref,
                 xo_ref, hp_ref, idx_ref, wt_ref, wb_scr, *, tm):
    @pl.when(pl.program_id(0) == 0)
    def _():
        wb_scr[...] = w_ref[...].astype(BF16)

    y = _dot(a_ref[...], wb_scr[...])
    r = ALPHA * x_ref[...] + mod_ref[2:3, :] * y
    xn = _layer_norm_rows(r, lng_ref[...], lnb_ref[...])
    xo_ref[...] = xn
    h1 = xn * (1.0 + mod_ref[4:5, :]) + mod_ref[3:4, :]
    for p in range(4):
        lo = h1[:, (2 * p) * LANES:(2 * p + 1) * LANES]
        hi = h1[:, (2 * p + 1) * LANES:(2 * p + 2) * LANES]
        hp_ref[pl.ds(p, tm, stride=4), :] = _pack_bf16_pairs(lo, hi)

    scores = _sigmoid(_dot_3x(h1, rw_ref[...]))
    sel = scores + rb_ref[...]
    e_iota = lax.broadcasted_iota(I32, (tm, N_EXP), 1)
    lane = lax.broadcasted_iota(I32, (tm, LANES), 1)
    idx_out = jnp.zeros((tm, LANES), I32)
    w_out = jnp.zeros((tm, LANES), F32)
    total = jnp.zeros((tm, 1), F32)
    for k in range(TOP_K):
        mx = jnp.max(sel, axis=-1, keepdims=True)
        ik = jnp.min(jnp.where(sel == mx, e_iota, N_EXP), axis=-1, keepdims=True)
        hit = e_iota == ik
        wk = jnp.sum(jnp.where(hit, scores, 0.0), axis=-1, keepdims=True)
        sel = jnp.where(hit, -jnp.inf, sel)
        total = total + wk
        idx_out = jnp.where(lane == k, ik, idx_out)
        w_out = jnp.where(lane == k, wk, w_out)
    idx_ref[...] = idx_out
    wt_ref[...] = w_out / total * ROUTE_SCALE


def _post(a, x, mod4, layer, w_out, ln_g, ln_b, router_w, router_b, tm=512):
    T = x.shape[0]
    return pl.pallas_call(
        functools.partial(_post_kernel, tm=tm),
        grid=(T // tm,),
        in_specs=[pl.BlockSpec((tm, D), lambda i: (i, 0)),
                  pl.BlockSpec((tm, D), lambda i: (i, 0)),
                  pl.BlockSpec((None, None, 6, D), lambda i: (layer, (i * tm) // GROUP, 0, 0)),
                  pl.BlockSpec((D, D), lambda i: (0, 0)),
                  pl.BlockSpec((1, D), lambda i: (0, 0)),
                  pl.BlockSpec((1, D), lambda i: (0, 0)),
                  pl.BlockSpec((D, N_EXP), lambda i: (0, 0)),
                  pl.BlockSpec((1, N_EXP), lambda i: (0, 0))],
        out_specs=[pl.BlockSpec((tm, D), lambda i: (i, 0)),
                   pl.BlockSpec((tm * 4, LANES), lambda i: (i, 0)),
                   pl.BlockSpec((tm, LANES), lambda i: (i, 0)),
                   pl.BlockSpec((tm, LANES), lambda i: (i, 0))],
        out_shape=[jax.ShapeDtypeStruct((T, D), F32),
                   jax.ShapeDtypeStruct((T * 4, LANES), I32),
                   jax.ShapeDtypeStruct((T, LANES), I32),
                   jax.ShapeDtypeStruct((T, LANES), F32)],
        scratch_shapes=[pltpu.VMEM((D, D), BF16)],
        compiler_params=_cparams(("arbitrary",)),
        name="out_proj_ln_router",
    )(a, x, mod4, w_out, ln_g.reshape(1, D), ln_b.reshape(1, D), router_w, router_b.reshape(1, N_EXP))


def _route_tables(idx128, w128):
    idx = idx128[:, :TOP_K].reshape(N_GROUPS, GROUP, TOP_K)
    w = w128[:, :TOP_K].reshape(N_GROUPS, GROUP, TOP_K)
    onehot = idx[..., None] == jnp.arange(N_EXP, dtype=I32)
    counts = jnp.sum(onehot.astype(I32), axis=(1, 2))
    padded = (counts + MOE_M - 1) // MOE_M * MOE_M
    pad_end = jnp.cumsum(padded, axis=-1)
    tok = jnp.arange(GROUP, dtype=I32)[None, :, None]
    real_keys = (idx * (2 * GROUP) + tok).reshape(N_GROUPS, GROUP * TOP_K)
    fill = jnp.arange(MOE_M, dtype=I32)[None, None, :]
    e_ids = jnp.arange(N_EXP, dtype=I32)[None, :, None]
    fill_keys = jnp.where(fill < (padded - counts)[:, :, None], e_ids * (2 * GROUP) + GROUP + fill,
                          N_EXP * 2 * GROUP + e_ids * MOE_M + fill).reshape(N_GROUPS, N_EXP * MOE_M)
    keys = jnp.concatenate([real_keys, fill_keys], axis=1)
    vals = jnp.concatenate([w.reshape(N_GROUPS, GROUP * TOP_K), jnp.zeros((N_GROUPS, N_EXP * MOE_M), F32)], axis=1)
    keys, row_w = lax.sort((keys, vals), dimension=1, num_keys=1)
    row_tok = jnp.where(keys < N_EXP * 2 * GROUP, jnp.minimum(keys & (2 * GROUP - 1), GROUP), GROUP)
    row_tok = row_tok.reshape(-1)
    n_used = pad_end[:, -1] // MOE_M
    starts = jnp.arange(MOE_NB, dtype=I32) * MOE_M
    block_e = jnp.sum((starts[None, :, None] >= pad_end[:, None, :]).astype(I32), axis=-1)
    block_e = jnp.minimum(block_e, N_EXP - 1)
    last_e = jnp.take_along_axis(block_e, jnp.maximum(n_used - 1, 0)[:, None], axis=1)
    block_e = jnp.where(jnp.arange(MOE_NB, dtype=I32)[None, :] < n_used[:, None], block_e, last_e)
    return block_e.reshape(-1).astype(I32), n_used.astype(I32), row_tok, row_w.reshape(-1, 1)


def _moe_stage(tg_ref, ts_ref, x_v, acc, rw_ref, wg_ref, wu_ref, wd_ref, tile_g, tile_c, ys_c, ys_s):
    for mi in range(MOE_M):
        tile_g[pl.ds(mi, 4, stride=MOE_S), :] = x_v[pl.ds(pl.multiple_of(tg_ref[0, mi] * 4, 4), 4), :]
    xb = _unpack_rows(tile_c, MOE_M, MOE_S)
    g = _dot(xb, wg_ref[...])
    u = _dot(xb, wu_ref[...])
    a = (_silu(g) * u) * rw_ref[...]
    y = _dot(a.astype(BF16), wd_ref[...])
    for c in range(D // LANES):
        ys_c[pl.ds(c * MOE_S, MOE_M), :] = y[:, c * LANES:(c + 1) * LANES]
    for m0 in range(0, MOE_M, MOE_U):
        offs = [pl.multiple_of(ts_ref[0, m0 + j] * 8, 8) for j in range(MOE_U)]
        vals = [acc[pl.ds(offs[j], 8), :] + ys_s[pl.ds(m0 + j, 8, stride=MOE_S), :] for j in range(MOE_U)]
        for j in range(MOE_U):
            acc[pl.ds(offs[j], 8), :] = vals[j]


def _moe_kernel(be_ref, nu_ref, tg_ref, ts_ref, x_hbm, rw_ref, wg_ref, wu_ref, wd_ref, out_hbm,
                x_v, acc, tile_a, tile_b, ys_a, ys_b, sem):
    s = pl.program_id(0)
    j = pl.program_id(1)

    @pl.when(j == 0)
    def _():
        cp = pltpu.make_async_copy(x_hbm.at[s], x_v.at[pl.ds(0, GROUP * 4)], sem.at[0])
        cp.start()
        cp.wait()
        x_v[pl.ds(GROUP * 4, 8), :] = jnp.zeros((8, LANES), I32)

        def clear(i, carry):
            acc[pl.ds(pl.multiple_of(i * 8, 8), 8), :] = jnp.zeros((8, LANES), F32)
            return carry

        lax.fori_loop(0, GROUP + 1, clear, 0)
        for t in (tile_a, tile_b):
            t[...] = jnp.zeros(t.shape, I32)
        for y in (ys_a, ys_b):
            y[...] = jnp.zeros(y.shape, F32)

    live = j < nu_ref[s] + 2
    common = (tg_ref, ts_ref, x_v, acc, rw_ref, wg_ref, wu_ref, wd_ref)

    @pl.when(live & (j % 2 == 0))
    def _():
        _moe_stage(*common, tile_a, tile_b, ys_b, ys_a)

    @pl.when(live & (j % 2 == 1))
    def _():
        _moe_stage(*common, tile_b, tile_a, ys_a, ys_b)

    @pl.when(j == MOE_NB + 1)
    def _():
        cp = pltpu.make_async_copy(acc.at[pl.ds(0, GROUP * 8)], out_hbm.at[s], sem.at[1])
        cp.start()
        cp.wait()


def _moe_routed(hp, tables, layer, w_gate, w_up, w_down):
    block_e, n_used, row_tok, row_w = tables
    x3 = hp.reshape(N_GROUPS, GROUP * 4, LANES)
    blk = lambda s, j, d: s * MOE_NB + jnp.clip(j - d, 0, MOE_NB - 1)
    wspec = lambda shape: pl.BlockSpec((None,) + shape, lambda s, j, be, nu: (be[blk(s, j, 1)], 0, 0))
    tspec = lambda d: pl.BlockSpec((None, 1, MOE_M), lambda s, j, be, nu: (blk(s, j, d), 0, 0),
                                   memory_space=pltpu.SMEM)
    tok3 = row_tok.reshape(N_GROUPS * MOE_NB, 1, MOE_M)
    out = pl.pallas_call(
        _moe_kernel,
        grid_spec=pltpu.PrefetchScalarGridSpec(
            num_scalar_prefetch=2,
            grid=(N_GROUPS, MOE_NB + 2),
            in_specs=[tspec(0), tspec(2),
                      pl.BlockSpec(memory_space=pl.ANY),
                      pl.BlockSpec((MOE_M, 1), lambda s, j, be, nu: (blk(s, j, 1), 0)),
                      wspec((D, D_EXP)), wspec((D, D_EXP)), wspec((D_EXP, D))],
            out_specs=pl.BlockSpec(memory_space=pl.ANY),
            scratch_shapes=[pltpu.VMEM((GROUP * 4 + 8, LANES), I32),
                            pltpu.VMEM(((GROUP + 1) * 8, LANES), F32),
                            pltpu.VMEM((4 * MOE_S, LANES), I32),
                            pltpu.VMEM((4 * MOE_S, LANES), I32),
                            pltpu.VMEM((8 * MOE_S, LANES), F32),
                            pltpu.VMEM((8 * MOE_S, LANES), F32),
                            pltpu.SemaphoreType.DMA((2,))]),
        out_shape=jax.ShapeDtypeStruct((N_GROUPS, GROUP * 8, LANES), F32),
        compiler_params=_cparams(("arbitrary", "arbitrary")),
        name="moe_routed",
    )(block_e, n_used, tok3, tok3, x3, row_w,
      w_gate[layer].astype(BF16), w_up[layer].astype(BF16), w_down[layer].astype(BF16))
    return out.reshape(T_ALL * 8, LANES)


def _moe_post_kernel(x_ref, hp_ref, r_ref, mod_ref, sg_ref, su_ref, sd_ref, lng_ref, lnb_ref, o_ref,
                     sgb, sub, sdb, *, tm):
    @pl.when(pl.program_id(0) == 0)
    def _():
        sgb[...] = sg_ref[...].astype(BF16)
        sub[...] = su_ref[...].astype(BF16)
        sdb[...] = sd_ref[...].astype(BF16)

    hb = _unpack_rows(hp_ref, tm, 4)
    g = _dot(hb, sgb[...])
    u = _dot(hb, sub[...])
    sh = _dot((_silu(g) * u).astype(BF16), sdb[...])
    routed = jnp.concatenate([r_ref[pl.ds(c, tm, stride=8), :] for c in range(D // LANES)], axis=1)
    r = ALPHA * x_ref[...] + mod_ref[5:6, :] * (routed + sh)
    o_ref[...] = _layer_norm_rows(r, lng_ref[...], lnb_ref[...])


def _moe_post(x, hp, routed, mod4, layer, sg, su, sd, ln_g, ln_b, tm=512):
    T = x.shape[0]
    return pl.pallas_call(
        functools.partial(_moe_post_kernel, tm=tm),
        grid=(T // tm,),
        in_specs=[pl.BlockSpec((tm, D), lambda i: (i, 0)),
                  pl.BlockSpec((tm * 4, LANES), lambda i: (i, 0)),
                  pl.BlockSpec((tm * 8, LANES), lambda i: (i, 0)),
                  pl.BlockSpec((None, None, 6, D), lambda i: (layer, (i * tm) // GROUP, 0, 0)),
                  pl.BlockSpec((D, D_SH), lambda i: (0, 0)),
                  pl.BlockSpec((D, D_SH), lambda i: (0, 0)),
                  pl.BlockSpec((D_SH, D), lambda i: (0, 0)),
                  pl.BlockSpec((1, D), lambda i: (0, 0)),
                  pl.BlockSpec((1, D), lambda i: (0, 0))],
        out_specs=pl.BlockSpec((tm, D), lambda i: (i, 0)),
        out_shape=jax.ShapeDtypeStruct((T, D), F32),
        scratch_shapes=[pltpu.VMEM((D, D_SH), BF16), pltpu.VMEM((D, D_SH), BF16), pltpu.VMEM((D_SH, D), BF16)],
        compiler_params=_cparams(("arbitrary",)),
        name="shared_expert_ln",
    )(x, hp, routed, mod4, sg, su, sd, ln_g.reshape(1, D), ln_b.reshape(1, D))


def _conv_silu(src_ref, w_ref, dst_ref, T, scale):
    L = ML_L
    nc = T // L
    w0, w1, w2 = w_ref[0:1, :], w_ref[1:2, :], w_ref[2:3, :]
    row = lax.broadcasted_iota(I32, (L, ML_DK), 0)

    def body(c, carry):
        r0 = pl.multiple_of(c * L, L)
        cur = src_ref[pl.ds(r0, L), :]
        p0 = pl.multiple_of(jnp.maximum(r0 - 8, 0), 8)
        n0 = pl.multiple_of(jnp.minimum(r0 + L, T - 8), 8)
        prev_row = src_ref[pl.ds(p0, 8), :][7:8, :] * jnp.where(c > 0, 1.0, 0.0).astype(F32)
        next_row = src_ref[pl.ds(n0, 8), :][0:1, :] * jnp.where(c < nc - 1, 1.0, 0.0).astype(F32)
        prev = jnp.where(row == 0, prev_row, pltpu.roll(cur, 1, axis=0))
        nxt = jnp.where(row == L - 1, next_row, pltpu.roll(cur, L - 1, axis=0))
        dst_ref[pl.ds(r0, L), :] = _silu(w0 * prev + w1 * cur + w2 * nxt) * scale
        return carry

    lax.fori_loop(0, nc, body, 0)


def _mlstm_chunk(q, k, v, G, GT, C, n, m, backward):
    L = ML_L
    ri = lax.broadcasted_iota(I32, (L, L), 0)
    ci = lax.broadcasted_iota(I32, (L, L), 1)
    keep = (ci >= ri) if backward else (ci <= ri)
    A = keep.astype(BF16)
    AT = ((ri >= ci) if backward else (ri <= ci)).astype(BF16)
    ic, fc = (2, 3) if backward else (0, 1)
    last = 0 if backward else L - 1
    b_col = _mask_dot(A, _log_sigmoid(G))[:, fc:fc + 1]
    b_row = _dot_mask(_log_sigmoid(GT), AT)[fc:fc + 1, :]
    ig_col = G[:, ic:ic + 1]
    ig_row = GT[ic:ic + 1, :]
    log_d = jnp.where(keep, b_col - b_row + ig_row, -jnp.inf)
    log_inter = b_col + m
    m_t = jnp.maximum(log_inter, jnp.max(log_d, axis=-1, keepdims=True))
    d = jnp.exp(log_d - m_t)
    w_inter = jnp.exp(log_inter - m_t)
    qb, kb, vb = q.astype(BF16), k.astype(BF16), v.astype(BF16)
    s = _dot_nt(qb, kb) * d
    num = _dot(s.astype(BF16), vb) + w_inter * _dot(qb, C.astype(BF16))
    den = jnp.sum(s, axis=-1, keepdims=True) + w_inter * jnp.sum(q * n, axis=-1, keepdims=True)
    h = num / jnp.maximum(jnp.abs(den), jnp.exp(-m_t))
    m_new = m_t[last:last + 1, :]
    w_last = jnp.exp(b_col[last:last + 1, :] - b_col + ig_col - m_new)
    decay = w_inter[last:last + 1, :]
    kw = k * w_last
    C_new = decay * C + _dot_tn(kw.astype(BF16), vb)
    n_new = decay * n + jnp.sum(kw, axis=0, keepdims=True)
    return h, C_new, n_new, m_new


def _mlstm_kernel(q_ref, k_ref, v_ref, og_ref, g_ref, gt_ref, cq_ref, ck_ref, ng_ref, c0_ref, n0_ref, m0_ref,
                  a_ref, c_out, n_out, m_out, qs, ks, hf, hb, cst, nst, mst, *, T):
    L = ML_L
    nc = T // L
    _conv_silu(q_ref, cq_ref, qs, T, ML_DK ** -0.5)
    _conv_silu(k_ref, ck_ref, ks, T, 1.0)
    cst[...] = c0_ref[...]
    nst[...] = n0_ref[...]
    mst[...] = m0_ref[...]

    def body(i, carry):
        for direction, out in ((0, hf), (1, hb)):
            C, n, m = cst[direction], nst[direction], mst[direction]
            for u in range(SCAN_UNROLL):
                ci = i * SCAN_UNROLL + u
                c = (nc - 1 - ci) if direction else ci
                r0 = pl.multiple_of(c * L, L)
                h, C, n, m = _mlstm_chunk(
                    qs[pl.ds(r0, L), :], ks[pl.ds(r0, L), :], v_ref[pl.ds(r0, L), :],
                    g_ref[pl.ds(r0, L), :], gt_ref[c], C, n, m, backward=bool(direction))
                out[pl.ds(r0, L), :] = h
            cst[direction] = C
            nst[direction] = n
            mst[direction] = m
        return carry

    lax.fori_loop(0, nc // SCAN_UNROLL, body, 0)
    c_out[...] = cst[...]
    n_out[...] = nst[...]
    m_out[...] = mst[...]

    def finish(c, carry):
        r0 = pl.multiple_of(c * L, L)
        tot = hf[pl.ds(r0, L), :] + hb[pl.ds(r0, L), :]
        mu = jnp.mean(tot, axis=-1, keepdims=True)
        cen = tot - mu
        var = jnp.mean(cen * cen, axis=-1, keepdims=True)
        hn = cen * lax.rsqrt(var + EPS) * ng_ref[...]
        a_ref[pl.ds(r0, L), :] = (hn * _sigmoid(og_ref[pl.ds(r0, L), :])).astype(BF16)
        return carry

    lax.fori_loop(0, nc, finish, 0)


def _mlstm_scan(z, gh, ght, conv_w, norm_g, C0, n0, m0, T, row_blk0, nseq):
    nc = T // ML_L
    qcol, kcol = 0, ML_QK // ML_DK
    vcol, ocol = 2 * ML_QK // ML_DV, (2 * ML_QK + ML_V) // ML_DV
    rb = lambda s: row_blk0 + s
    return pl.pallas_call(
        functools.partial(_mlstm_kernel, T=T),
        grid=(nseq, ML_H),
        in_specs=[pl.BlockSpec((T, ML_DK), lambda s, h: (rb(s), qcol + h)),
                  pl.BlockSpec((T, ML_DK), lambda s, h: (rb(s), kcol + h)),
                  pl.BlockSpec((T, ML_DV), lambda s, h: (rb(s), vcol + h)),
                  pl.BlockSpec((T, ML_DV), lambda s, h: (rb(s), ocol + h)),
                  pl.BlockSpec((None, T, 4), lambda s, h: (h, rb(s), 0)),
                  pl.BlockSpec((None, nc, 4, ML_L), lambda s, h: (h, rb(s), 0, 0)),
                  pl.BlockSpec((3, ML_DK), lambda s, h: (0, qcol + h)),
                  pl.BlockSpec((3, ML_DK), lambda s, h: (0, kcol + h)),
                  pl.BlockSpec((1, ML_DV), lambda s, h: (0, h)),
                  pl.BlockSpec((None, 2, None, ML_DK, ML_DV), lambda s, h: (s, 0, h, 0, 0)),
                  pl.BlockSpec((None, 2, None, 1, ML_DK), lambda s, h: (s, 0, h, 0, 0)),
                  pl.BlockSpec((None, 2, None, 1, 1), lambda s, h: (s, 0, h, 0, 0))],
        out_specs=[pl.BlockSpec((T, ML_DV), lambda s, h: (s, h)),
                   pl.BlockSpec((None, 2, None, ML_DK, ML_DV), lambda s, h: (s, 0, h, 0, 0)),
                   pl.BlockSpec((None, 2, None, 1, ML_DK), lambda s, h: (s, 0, h, 0, 0)),
                   pl.BlockSpec((None, 2, None, 1, 1), lambda s, h: (s, 0, h, 0, 0))],
        out_shape=[jax.ShapeDtypeStruct((nseq * T, ML_V), BF16),
                   jax.ShapeDtypeStruct((nseq, 2, ML_H, ML_DK, ML_DV), F32),
                   jax.ShapeDtypeStruct((nseq, 2, ML_H, 1, ML_DK), F32),
                   jax.ShapeDtypeStruct((nseq, 2, ML_H, 1, 1), F32)],
        scratch_shapes=[pltpu.VMEM((T, ML_DK), F32), pltpu.VMEM((T, ML_DK), F32),
                        pltpu.VMEM((T, ML_DV), F32), pltpu.VMEM((T, ML_DV), F32),
                        pltpu.VMEM((2, ML_DK, ML_DV), F32), pltpu.VMEM((2, 1, ML_DK), F32),
                        pltpu.VMEM((2, 1, 1), F32)],
        compiler_params=_cparams(("arbitrary", "arbitrary")),
        name="mlstm_scan",
    )(z, z, z, z, gh, ght, conv_w, conv_w, norm_g.reshape(1, ML_V), C0, n0, m0)


def _mlstm_layer(x, mod4, layer, j, w_in, gate_b, conv_w, norm_g, C_lat, n_lat, m_lat):
    z = _proj(x, mod4, layer, w_in, j, 2 * ML_QK + 2 * ML_V)
    gates = _proj_small(x, mod4, layer, w_in[j][:, 2 * ML_QK + 2 * ML_V:], gate_b[j])
    gh = gates.reshape(T_ALL, 4, ML_H).transpose(2, 0, 1)
    ght = gh.reshape(ML_H, T_ALL // ML_L, ML_L, 4).transpose(0, 1, 3, 2)
    zC = jnp.zeros((BATCH, 2, ML_H, ML_DK, ML_DV), F32)
    zn = jnp.zeros((BATCH, 2, ML_H, 1, ML_DK), F32)
    zm = jnp.zeros((BATCH, 2, ML_H, 1, 1), F32)
    a_p, Cn, nn, mn = _mlstm_scan(z, gh, ght, conv_w[j], norm_g[j], zC, zn, zm, SEQ, 0, BATCH)
    a_s, _, _, _ = _mlstm_scan(z, gh, ght, conv_w[j], norm_g[j], C_lat[:, j],
                               n_lat[:, j].reshape(DEC_BATCH, 2, ML_H, 1, ML_DK),
                               m_lat[:, j].reshape(DEC_BATCH, 2, ML_H, 1, 1),
                               DEC_SEQ, BATCH * SEQ // DEC_SEQ, DEC_BATCH)
    a = jnp.concatenate([a_p, a_s], axis=0)
    return a, (Cn, nn.reshape(BATCH, 2, ML_H, ML_DK), mn.reshape(BATCH, 2, ML_H))


def _hgrn_chunk(q, k, v, g, St, backward):
    L = HG_L
    ri = lax.broadcasted_iota(I32, (L, L), 0)
    ci = lax.broadcasted_iota(I32, (L, L), 1)
    keep = (ci >= ri) if backward else (ci <= ri)
    ref = L - 1 - L // 2 if backward else L // 2
    last = 0 if backward else L - 1
    b = _mask_dot(keep.astype(BF16), g)
    b_ref = b[ref:ref + 1, :]
    b_last = b[last:last + 1, :]
    qe = (q * jnp.exp(b - b_ref)).astype(BF16)
    ke = (k * jnp.exp(b_ref - b)).astype(BF16)
    vb = v.astype(BF16)
    a = jnp.where(keep, _dot_nt(qe, ke), 0.0)
    o = _dot(a.astype(BF16), vb) + _dot_nt((q * jnp.exp(b)).astype(BF16), St.astype(BF16))
    kd = (k * jnp.exp(b_last - b)).astype(BF16)
    St_new = jnp.exp(b_last) * St + _dot_tn(vb, kd)
    return o, St_new


def _hgrn_kernel(q_ref, i_ref, ff_ref, fb_ref, og_ref, fbf_ref, fbb_ref, lbr_ref, ng_ref, s0_ref,
                 a_ref, s_out, of, ob, st, *, T, lb_layer):
    L = HG_L
    nc = T // L
    raw = lbr_ref[...]
    e = jnp.exp(raw - jnp.max(raw, axis=0, keepdims=True))
    p = e / jnp.sum(e, axis=0, keepdims=True)
    lb = jnp.sum(p[0:lb_layer + 1, :], axis=0, keepdims=True) - p[0:1, :]
    st[0] = s0_ref[0].T
    st[1] = s0_ref[1].T

    def body(i, carry):
        for direction, out, f_ref, b_ref in ((0, of, ff_ref, fbf_ref), (1, ob, fb_ref, fbb_ref)):
            St = st[direction]
            for u in range(SCAN_UNROLL):
                ci = i * SCAN_UNROLL + u
                c = (nc - 1 - ci) if direction else ci
                r0 = pl.multiple_of(c * L, L)
                f = lb + (1.0 - lb) * _sigmoid(f_ref[pl.ds(r0, L), :] + b_ref[...])
                o, St = _hgrn_chunk(_silu(q_ref[pl.ds(r0, L), :]), 1.0 - f, i_ref[pl.ds(r0, L), :],
                                    jnp.log(f), St, backward=bool(direction))
                out[pl.ds(r0, L), :] = o
            st[direction] = St
        return carry

    lax.fori_loop(0, nc // SCAN_UNROLL, body, 0)
    s_out[0] = st[0].T
    s_out[1] = st[1].T

    def finish(c, carry):
        r0 = pl.multiple_of(c * L, L)
        tot = of[pl.ds(r0, L), :] + ob[pl.ds(r0, L), :]
        on = tot * lax.rsqrt(jnp.mean(tot * tot, axis=-1, keepdims=True) + EPS) * ng_ref[...]
        a_ref[pl.ds(r0, L), :] = (on * _silu(og_ref[pl.ds(r0, L), :])).astype(BF16)
        return carry

    lax.fori_loop(0, nc, finish, 0)


def _hgrn_scan(z, f_b, lb_raw, norm_g, S0, T, row_blk0, nseq, lb_layer):
    nh = HG_H
    rb = lambda s: row_blk0 + s
    zspec = lambda cb: pl.BlockSpec((T, HG_DK), lambda s, h: (rb(s), cb * nh + h))
    return pl.pallas_call(
        functools.partial(_hgrn_kernel, T=T, lb_layer=lb_layer),
        grid=(nseq, nh),
        in_specs=[zspec(0), zspec(1), zspec(2), zspec(3), zspec(4),
                  pl.BlockSpec((1, HG_DK), lambda s, h: (0, h)),
                  pl.BlockSpec((1, HG_DK), lambda s, h: (0, nh + h)),
                  pl.BlockSpec((DEPTH, HG_DK), lambda s, h: (0, h)),
                  pl.BlockSpec((1, HG_DV), lambda s, h: (0, h)),
                  pl.BlockSpec((None, 2, None, HG_DK, HG_DV), lambda s, h: (s, 0, h, 0, 0))],
        out_specs=[pl.BlockSpec((T, HG_DV), lambda s, h: (s, h)),
                   pl.BlockSpec((None, 2, None, HG_DK, HG_DV), lambda s, h: (s, 0, h, 0, 0))],
        out_shape=[jax.ShapeDtypeStruct((nseq * T, HG_V), BF16),
                   jax.ShapeDtypeStruct((nseq, 2, nh, HG_DK, HG_DV), F32)],
        scratch_shapes=[pltpu.VMEM((T, HG_DV), F32), pltpu.VMEM((T, HG_DV), F32),
                        pltpu.VMEM((2, HG_DV, HG_DK), F32)],
        compiler_params=_cparams(("arbitrary", "arbitrary")),
        name="hgrn_scan",
    )(z, z, z, z, z, f_b.reshape(1, 2 * HG_K), f_b.reshape(1, 2 * HG_K), lb_raw, norm_g.reshape(1, HG_V), S0)


def _hgrn_layer(x, mod4, layer, j, w_in, f_b, lb_raw, norm_g, S_lat):
    z = _proj(x, mod4, layer, w_in, j, 3 * HG_K + 2 * HG_V)
    zS = jnp.zeros((BATCH, 2, HG_H, HG_DK, HG_DV), F32)
    a_p, Sn = _hgrn_scan(z, f_b[j], lb_raw, norm_g[j], zS, SEQ, 0, BATCH, layer)
    a_s, _ = _hgrn_scan(z, f_b[j], lb_raw, norm_g[j], S_lat[:, j], DEC_SEQ, BATCH * SEQ // DEC_SEQ, DEC_BATCH, layer)
    return jnp.concatenate([a_p, a_s], axis=0), Sn


def _head_rms(x, g_tiled, nheads):
    lane = lax.broadcasted_iota(I32, x.shape, 1)
    sq = x * x
    ms = jnp.zeros_like(x)
    for h in range(nheads):
        in_h = (lane >= h * AT_HD) & (lane < (h + 1) * AT_HD)
        tot = jnp.sum(jnp.where(in_h, sq, 0.0), axis=-1, keepdims=True)
        ms = jnp.where(in_h, tot, ms)
    return x * lax.rsqrt(ms * (1.0 / AT_HD) + EPS) * g_tiled


def _rope(x, cos, sin):
    w = x.shape[1]
    lane = lax.broadcasted_iota(I32, x.shape, 1)
    up = pltpu.roll(x, w - 16, axis=1)
    down = pltpu.roll(x, 16, axis=1)
    swapped = jnp.where((lane % 32) < 16, up, down)
    return x * cos + swapped * sin


def _attn_ctx_kernel(q_ref, k_ref, v_ref, qn_ref, kn_ref, o_ref, ko_ref, vo_ref):
    k = _head_rms(k_ref[...], kn_ref[...], AT_KV)
    ko_ref[...] = k
    v = v_ref[...]
    vo_ref[...] = v
    kb, vb = k.astype(BF16), v.astype(BF16)
    outs = []
    for kv in range(AT_KV):
        q = _head_rms(q_ref[:, kv * 256:(kv + 1) * 256], qn_ref[...], AT_G) * (AT_HD ** -0.5)
        kh = kb[:, kv * AT_HD:(kv + 1) * AT_HD]
        vh = vb[:, kv * AT_HD:(kv + 1) * AT_HD]
        for g in range(AT_G):
            s = _dot_nt(q[:, g * AT_HD:(g + 1) * AT_HD].astype(BF16), kh)
            p = jnp.exp(s - jnp.max(s, axis=-1, keepdims=True))
            p = p / jnp.sum(p, axis=-1, keepdims=True)
            outs.append(_dot(p.astype(BF16), vh))
    o_ref[...] = jnp.concatenate(outs, axis=1).astype(BF16)


def _attn_ctx(z, q_norm, k_norm):
    qn = jnp.tile(q_norm, AT_G).reshape(1, 256)
    kn = jnp.tile(k_norm, AT_KV).reshape(1, 256)
    return pl.pallas_call(
        _attn_ctx_kernel,
        grid=(BATCH,),
        in_specs=[pl.BlockSpec((SEQ, D), lambda b: (b, 0)),
                  pl.BlockSpec((SEQ, 256), lambda b: (b, 4)),
                  pl.BlockSpec((SEQ, 256), lambda b: (b, 5)),
                  pl.BlockSpec((1, 256), lambda b: (0, 0)),
                  pl.BlockSpec((1, 256), lambda b: (0, 0))],
        out_specs=[pl.BlockSpec((SEQ, D), lambda b: (b, 0)),
                   pl.BlockSpec((SEQ, 256), lambda b: (b, 0)),
                   pl.BlockSpec((SEQ, 256), lambda b: (b, 0))],
        out_shape=[jax.ShapeDtypeStruct((BATCH * SEQ, D), BF16),
                   jax.ShapeDtypeStruct((BATCH * SEQ, 256), F32),
                   jax.ShapeDtypeStruct((BATCH * SEQ, 256), F32)],
        compiler_params=_cparams(("arbitrary",)),
        name="attn_context",
    )(z, z, z, qn, kn)


def _attn_kv_kernel(k_ref, v_ref, kn_ref, cos_ref, sin_ref, ko_ref, vo_ref):
    k = _rope(_head_rms(k_ref[...], kn_ref[...], AT_KV), cos_ref[...], sin_ref[...])
    v = v_ref[...]
    for h in range(AT_KV):
        ko_ref[h] = k[:, h * AT_HD:(h + 1) * AT_HD].astype(BF16)
        vo_ref[h] = v[:, h * AT_HD:(h + 1) * AT_HD].astype(BF16)


def _attn_kv(z, k_norm, cos4, sin4, tt=512):
    kn = jnp.tile(k_norm, AT_KV).reshape(1, 256)
    nt = DEC_SEQ // tt
    row0 = BATCH * SEQ // tt
    return pl.pallas_call(
        _attn_kv_kernel,
        grid=(DEC_BATCH, nt),
        in_specs=[pl.BlockSpec((tt, 256), lambda b, i: (row0 + b * nt + i, 4)),
                  pl.BlockSpec((tt, 256), lambda b, i: (row0 + b * nt + i, 5)),
                  pl.BlockSpec((1, 256), lambda b, i: (0, 0)),
                  pl.BlockSpec((tt, 256), lambda b, i: (i, 0)),
                  pl.BlockSpec((tt, 256), lambda b, i: (i, 0))],
        out_specs=[pl.BlockSpec((None, AT_KV, tt, AT_HD), lambda b, i: (b, 0, i, 0)),
                   pl.BlockSpec((None, AT_KV, tt, AT_HD), lambda b, i: (b, 0, i, 0))],
        out_shape=[jax.ShapeDtypeStruct((DEC_BATCH, AT_KV, DEC_SEQ, AT_HD), BF16),
                   jax.ShapeDtypeStruct((DEC_BATCH, AT_KV, DEC_SEQ, AT_HD), BF16)],
        compiler_params=_cparams(("arbitrary", "arbitrary")),
        name="attn_kv_prep",
    )(z, z, kn, cos4, sin4)


def _attn_lat_kernel(q_ref, qn_ref, cos_ref, sin_ref, k_ref, v_ref, o_ref, *, tq, tk):
    q = _rope(_head_rms(q_ref[...], qn_ref[...], AT_G), cos_ref[...], sin_ref[...]) * (AT_HD ** -0.5)
    qs = jnp.concatenate([q[:, g * AT_HD:(g + 1) * AT_HD] for g in range(AT_G)], axis=0).astype(BF16)
    rows = AT_G * tq
    nk = k_ref.shape[0] // tk

    def body(j, carry):
        m, l, acc = carry
        k0 = pl.multiple_of(j * tk, tk)
        s = _dot_nt(qs, k_ref[pl.ds(k0, tk), :])
        m_new = jnp.maximum(m, jnp.max(s, axis=-1, keepdims=True))
        alpha = jnp.exp(m - m_new)
        p = jnp.exp(s - m_new)
        l = alpha * l + jnp.sum(p, axis=-1, keepdims=True)
        acc = alpha * acc + _dot(p.astype(BF16), v_ref[pl.ds(k0, tk), :])
        return m_new, l, acc

    init = (jnp.full((rows, 1), -jnp.inf, F32), jnp.zeros((rows, 1), F32), jnp.zeros((rows, AT_HD), F32))
    _, l, acc = lax.fori_loop(0, nk, body, init)
    o = acc / l
    o_ref[...] = jnp.concatenate([o[g * tq:(g + 1) * tq, :] for g in range(AT_G)], axis=1).astype(BF16)


def _attn_lat(z, q_norm, cos4, sin4, kk, vv, tq=256, tk=512):
    qn = jnp.tile(q_norm, AT_G).reshape(1, 256)
    nq = DEC_SEQ // tq
    row0 = BATCH * SEQ // tq
    skv = kk.shape[2]
    return pl.pallas_call(
        functools.partial(_attn_lat_kernel, tq=tq, tk=tk),
        grid=(DEC_BATCH, AT_KV, nq),
        in_specs=[pl.BlockSpec((tq, 256), lambda b, h, i: (row0 + b * nq + i, h)),
                  pl.BlockSpec((1, 256), lambda b, h, i: (0, 0)),
                  pl.BlockSpec((tq, 256), lambda b, h, i: (i, 0)),
                  pl.BlockSpec((tq, 256), lambda b, h, i: (i, 0)),
                  pl.BlockSpec((None, None, skv, AT_HD), lambda b, h, i: (b, h, 0, 0)),
                  pl.BlockSpec((None, None, skv, AT_HD), lambda b, h, i: (b, h, 0, 0))],
        out_specs=pl.BlockSpec((tq, 256), lambda b, h, i: (b * nq + i, h)),
        out_shape=jax.ShapeDtypeStruct((DEC_BATCH * DEC_SEQ, D), BF16),
        compiler_params=_cparams(("arbitrary", "arbitrary", "arbitrary")),
        name="attn_latent",
    )(z, qn, cos4, sin4, kk, vv)


def _rope_tables():
    t = jnp.arange(DEC_SEQ)
    row = (t // GRID_W).astype(F32)
    col = (t % GRID_W).astype(F32)
    nf = AT_HD // 4
    inv = ROPE_THETA ** (-jnp.arange(nf, dtype=F32) / nf)
    ar, ac = row[:, None] * inv[None], col[:, None] * inv[None]
    cos = jnp.concatenate([jnp.cos(ar), jnp.cos(ar), jnp.cos(ac), jnp.cos(ac)], axis=1)
    sin = jnp.concatenate([-jnp.sin(ar), jnp.sin(ar), -jnp.sin(ac), jnp.sin(ac)], axis=1)
    return jnp.tile(cos, (1, 4)), jnp.tile(sin, (1, 4))


def _attn_layer(x, mod4, layer, j, w_in, q_norm, k_norm, cache_k, cache_v):
    z = _proj(x, mod4, layer, w_in, j, (AT_H + 2 * AT_KV) * AT_HD)
    a_p, k_new, v_new = _attn_ctx(z, q_norm[j], k_norm[j])
    cos4, sin4 = _rope_tables()
    k_lat, v_lat = _attn_kv(z, k_norm[j], cos4, sin4)
    kk = jnp.concatenate([cache_k[:, j].transpose(0, 2, 1, 3).astype(BF16), k_lat], axis=2)
    vv = jnp.concatenate([cache_v[:, j].transpose(0, 2, 1, 3).astype(BF16), v_lat], axis=2)
    a_s = _attn_lat(z, q_norm[j], cos4, sin4, kk, vv)
    a = jnp.concatenate([a_p, a_s], axis=0)
    return a, (k_new.reshape(BATCH, SEQ, AT_KV, AT_HD), v_new.reshape(BATCH, SEQ, AT_KV, AT_HD))


def kernel(x_prompt, x_sample, state_mlstm_C, state_mlstm_n, state_mlstm_m, cache_attn_k, cache_attn_v, state_hgrn_S, c, c_ctx, mod_w, mod_b, ln_g, ln_b, mlstm_w_in, mlstm_gate_b, mlstm_conv, mlstm_norm, mlstm_w_out, attn_w_in, attn_q_norm, attn_k_norm, attn_w_out, hgrn_w_in, hgrn_f_b, hgrn_lower_bounds, hgrn_norm, hgrn_w_out, moe_router, moe_router_b, moe_w_gate, moe_w_up, moe_w_down, moe_sh_gate, moe_sh_up, moe_sh_down):
    x = jnp.concatenate([x_prompt.reshape(BATCH * SEQ, D), x_sample.reshape(DEC_BATCH * DEC_SEQ, D)], axis=0)
    cond8 = jnp.concatenate([c_ctx[None], c, jnp.zeros((8 - 1 - DEC_BATCH, D), F32)], axis=0)
    mod4 = _mod_all(cond8, mod_w, mod_b)

    new_C, new_n, new_m, new_k, new_v, new_S = [], [], [], [], [], []
    for l in range(DEPTH):
        kind, j = l % 3, l // 3
        if kind == 0:
            a, (Cn, nn, mn) = _mlstm_layer(x, mod4, l, j, mlstm_w_in, mlstm_gate_b, mlstm_conv, mlstm_norm,
                                           state_mlstm_C, state_mlstm_n, state_mlstm_m)
            new_C.append(Cn)
            new_n.append(nn)
            new_m.append(mn)
            w_out = mlstm_w_out[j]
        elif kind == 1:
            a, (kn, vn) = _attn_layer(x, mod4, l, j, attn_w_in, attn_q_norm, attn_k_norm, cache_attn_k, cache_attn_v)
            new_k.append(kn)
            new_v.append(vn)
            w_out = attn_w_out[j]
        else:
            a, Sn = _hgrn_layer(x, mod4, l, j, hgrn_w_in, hgrn_f_b, hgrn_lower_bounds, hgrn_norm, state_hgrn_S)
            new_S.append(Sn)
            w_out = hgrn_w_out[j]
        x, hp, idx128, w128 = _post(a, x, mod4, l, w_out, ln_g[l, 0], ln_b[l, 0], moe_router[l], moe_router_b[l])
        routed = _moe_routed(hp, _route_tables(idx128, w128), l, moe_w_gate, moe_w_up, moe_w_down)
        x = _moe_post(x, hp, routed, mod4, l, moe_sh_gate[l], moe_sh_up[l], moe_sh_down[l], ln_g[l, 1], ln_b[l, 1])

    xp = x[:BATCH * SEQ].reshape(BATCH, SEQ, D)
    xs = x[BATCH * SEQ:].reshape(DEC_BATCH, DEC_SEQ, D)
    return (xp, xs, jnp.stack(new_C, 1), jnp.stack(new_n, 1), jnp.stack(new_m, 1),
            jnp.stack(new_k, 1), jnp.stack(new_v, 1), jnp.stack(new_S, 1))
```

```python
import functools

import jax
import jax.numpy as jnp
from jax import lax
from jax.experimental import pallas as pl
from jax.experimental.pallas import tpu as pltpu

F32 = jnp.float32
BF16 = jnp.bfloat16
I32 = jnp.int32
HI = lax.Precision.HIGHEST

D = 1024
BATCH, SEQ = 16, 256
DEPTH = 4
DEC_BATCH, DEC_SEQ = 2, 4096
PAST = 512
GRID_W = 64
GROUP = 4096
N_GROUPS = 3
T_ALL = N_GROUPS * GROUP

ML_H, ML_DK, ML_DV, ML_L = 4, 128, 256, 128
ML_QK, ML_V = ML_H * ML_DK, ML_H * ML_DV
AT_H, AT_KV, AT_HD, AT_G = 16, 4, 64, 4
HG_H, HG_DK, HG_DV, HG_L = 8, 128, 128, 64
HG_K, HG_V = HG_H * HG_DK, HG_H * HG_DV
N_EXP, TOP_K, D_EXP, D_SH = 64, 8, 256, 256
ROUTE_SCALE = 2.5
ALPHA = (2 * DEPTH) ** 0.25
EPS = 1e-6
ROPE_THETA = 10000.0

VMEM_LIMIT = 56 * 1024 * 1024
LANES = 128

MOE_M = 256
MOE_NB = GROUP * TOP_K // MOE_M + N_EXP
MOE_ROWS = MOE_NB * MOE_M
MOE_SORT_LEN = 65536
MOE_S = MOE_M + 8
MOE_U = 8
SCAN_UNROLL = 4


def _cparams(sem):
    return pltpu.CompilerParams(dimension_semantics=sem, vmem_limit_bytes=VMEM_LIMIT)


def _sigmoid(x):
    return 1.0 / (1.0 + jnp.exp(-x))


def _silu(x):
    return x * _sigmoid(x)


def _log_sigmoid(x):
    return jnp.minimum(x, 0.0) - jnp.log(1.0 + jnp.exp(-jnp.abs(x)))


def _dot(a, b):
    return jnp.dot(a, b, preferred_element_type=F32)


def _dot_nt(a, b):
    return lax.dot_general(a, b, (((1,), (1,)), ((), ())), preferred_element_type=F32)


def _dot_tn(a, b):
    return lax.dot_general(a, b, (((0,), (0,)), ((), ())), preferred_element_type=F32)


def _dot_hi(a, b):
    return jnp.dot(a, b, preferred_element_type=F32, precision=HI)


def _split3(x):
    p0 = x.astype(BF16)
    r1 = x - p0.astype(F32)
    p1 = r1.astype(BF16)
    p2 = (r1 - p1.astype(F32)).astype(BF16)
    return p0, p1, p2


def _mask_dot(mask_bf16, x):
    p0, p1, p2 = _split3(x)
    return _dot(mask_bf16, p0) + _dot(mask_bf16, p1) + _dot(mask_bf16, p2)


def _dot_mask(x, mask_bf16):
    p0, p1, p2 = _split3(x)
    return _dot(p0, mask_bf16) + _dot(p1, mask_bf16) + _dot(p2, mask_bf16)


def _dot_3x(a, b):
    a0 = a.astype(BF16)
    a1 = (a - a0.astype(F32)).astype(BF16)
    b0 = b.astype(BF16)
    b1 = (b - b0.astype(F32)).astype(BF16)
    return _dot(a0, b0) + _dot(a0, b1) + _dot(a1, b0)


def _mod_kernel(cond_ref, w_ref, b_ref, o_ref):
    o_ref[...] = _dot_hi(_silu(cond_ref[...]), w_ref[...]) + b_ref[...]


def _mod_all(cond8, mod_w, mod_b):
    tn = 1024
    out = pl.pallas_call(
        _mod_kernel,
        grid=(DEPTH, 6 * D // tn),
        in_specs=[pl.BlockSpec((8, D), lambda l, j: (0, 0)),
                  pl.BlockSpec((None, D, tn), lambda l, j: (l, 0, j)),
                  pl.BlockSpec((None, 1, tn), lambda l, j: (l, 0, j))],
        out_specs=pl.BlockSpec((None, 8, tn), lambda l, j: (l, 0, j)),
        out_shape=jax.ShapeDtypeStruct((DEPTH, 8, 6 * D), F32),
        compiler_params=_cparams(("arbitrary", "arbitrary")),
        name="mod_rows",
    )(cond8, mod_w, mod_b.reshape(DEPTH, 1, 6 * D))
    return out.reshape(DEPTH, 8, 6, D)


def _proj_kernel(x_ref, mod_ref, w_ref, o_ref, h_scr):
    @pl.when(pl.program_id(1) == 0)
    def _():
        h_scr[...] = (x_ref[...] * (1.0 + mod_ref[1:2, :]) + mod_ref[0:1, :]).astype(BF16)

    o_ref[...] = _dot(h_scr[...], w_ref[...].astype(BF16))


def _proj(x, mod4, layer, w3, widx, ncols, tm=1024, tn=512):
    T = x.shape[0]
    return pl.pallas_call(
        _proj_kernel,
        grid=(T // tm, ncols // tn),
        in_specs=[pl.BlockSpec((tm, D), lambda i, j: (i, 0)),
                  pl.BlockSpec((None, None, 6, D), lambda i, j: (layer, (i * tm) // GROUP, 0, 0)),
                  pl.BlockSpec((None, D, tn), lambda i, j: (widx, 0, j))],
        out_specs=pl.BlockSpec((tm, tn), lambda i, j: (i, j)),
        out_shape=jax.ShapeDtypeStruct((T, ncols), F32),
        scratch_shapes=[pltpu.VMEM((tm, D), BF16)],
        compiler_params=_cparams(("arbitrary", "arbitrary")),
        name="in_proj",
    )(x, mod4, w3)


def _proj_small_kernel(x_ref, mod_ref, w_ref, b_ref, o_ref):
    h = x_ref[...] * (1.0 + mod_ref[1:2, :]) + mod_ref[0:1, :]
    o_ref[...] = _dot_hi(h, w_ref[...]) + b_ref[...]


def _proj_small(x, mod4, layer, w, b, tm=1024):
    T, n = x.shape[0], w.shape[1]
    return pl.pallas_call(
        _proj_small_kernel,
        grid=(T // tm,),
        in_specs=[pl.BlockSpec((tm, D), lambda i: (i, 0)),
                  pl.BlockSpec((None, None, 6, D), lambda i: (layer, (i * tm) // GROUP, 0, 0)),
                  pl.BlockSpec((D, n), lambda i: (0, 0)),
                  pl.BlockSpec((1, n), lambda i: (0, 0))],
        out_specs=pl.BlockSpec((tm, n), lambda i: (i, 0)),
        out_shape=jax.ShapeDtypeStruct((T, n), F32),
        compiler_params=_cparams(("arbitrary",)),
        name="gate_proj",
    )(x, mod4, w, b.reshape(1, n))


def _layer_norm_rows(r, g, b):
    mu = jnp.mean(r, axis=-1, keepdims=True)
    c = r - mu
    var = jnp.mean(c * c, axis=-1, keepdims=True)
    return c * lax.rsqrt(var + EPS) * g + b


def _pack_bf16_pairs(lo, hi):
    lo_b = lax.bitcast_convert_type(lo.astype(BF16).astype(F32), I32)
    hi_b = lax.bitcast_convert_type(hi.astype(BF16).astype(F32), I32)
    return lax.shift_right_logical(lo_b, 16) | (hi_b & jnp.int32(-65536))


def _unpack_bf16_pairs(v):
    lo = lax.bitcast_convert_type(lax.shift_left(v, 16), F32)
    hi = lax.bitcast_convert_type(v & jnp.int32(-65536), F32)
    return lo.astype(BF16), hi.astype(BF16)


def _unpack_rows(ref, rows, stride, offset=0):
    chunks = []
    for r in range(4):
        if stride == 4:
            v = ref[pl.ds(r, rows, stride=4), :]
        else:
            v = ref[pl.ds(offset + r * stride, rows), :]
        chunks.extend(_unpack_bf16_pairs(v))
    return jnp.concatenate(chunks, axis=1)


def _post_kernel(a_ref, x_ref, mod_ref, w_ref, lng_ref, lnb_ref, rw_ref, rb_ref,
                 xo_ref, hp_ref, idx_ref, wt_ref, wb_scr, *, tm):
    @pl.when(pl.program_id(0) == 0)
    def _():
        wb_scr[...] = w_ref[...].astype(BF16)

    y = _dot(a_ref[...], wb_scr[...])
    r = ALPHA * x_ref[...] + mod_ref[2:3, :] * y
    xn = _layer_norm_rows(r, lng_ref[...], lnb_ref[...])
    xo_ref[...] = xn
    h1 = xn * (1.0 + mod_ref[4:5, :]) + mod_ref[3:4, :]
    for p in range(4):
        lo = h1[:, (2 * p) * LANES:(2 * p + 1) * LANES]
        hi = h1[:, (2 * p + 1) * LANES:(2 * p + 2) * LANES]
        hp_ref[pl.ds(p, tm, stride=4), :] = _pack_bf16_pairs(lo, hi)

    scores = _sigmoid(_dot_3x(h1, rw_ref[...]))
    sel = scores + rb_ref[...]
    e_iota = lax.broadcasted_iota(I32, (tm, N_EXP), 1)
    lane = lax.broadcasted_iota(I32, (tm, LANES), 1)
    idx_out = jnp.zeros((tm, LANES), I32)
    w_out = jnp.zeros((tm, LANES), F32)
    total = jnp.zeros((tm, 1), F32)
    for k in range(TOP_K):
        mx = jnp.max(sel, axis=-1, keepdims=True)
        ik = jnp.min(jnp.where(sel == mx, e_iota, N_EXP), axis=-1, keepdims=True)
        hit = e_iota == ik
        wk = jnp.sum(jnp.where(hit, scores, 0.0), axis=-1, keepdims=True)
        sel = jnp.where(hit, -jnp.inf, sel)
        total = total + wk
        idx_out = jnp.where(lane == k, ik, idx_out)
        w_out = jnp.where(lane == k, wk, w_out)
    idx_ref[...] = idx_out
    wt_ref[...] = w_out / total * ROUTE_SCALE


def _post(a, x, mod4, layer, w_out, ln_g, ln_b, router_w, router_b, tm=512):
    T = x.shape[0]
    return pl.pallas_call(
        functools.partial(_post_kernel, tm=tm),
        grid=(T // tm,),
        in_specs=[pl.BlockSpec((tm, D), lambda i: (i, 0)),
                  pl.BlockSpec((tm, D), lambda i: (i, 0)),
                  pl.BlockSpec((None, None, 6, D), lambda i: (layer, (i * tm) // GROUP, 0, 0)),
                  pl.BlockSpec((D, D), lambda i: (0, 0)),
                  pl.BlockSpec((1, D), lambda i: (0, 0)),
                  pl.BlockSpec((1, D), lambda i: (0, 0)),
                  pl.BlockSpec((D, N_EXP), lambda i: (0, 0)),
                  pl.BlockSpec((1, N_EXP), lambda i: (0, 0))],
        out_specs=[pl.BlockSpec((tm, D), lambda i: (i, 0)),
                   pl.BlockSpec((tm * 4, LANES), lambda i: (i, 0)),
                   pl.BlockSpec((tm, LANES), lambda i: (i, 0)),
                   pl.BlockSpec((tm, LANES), lambda i: (i, 0))],
        out_shape=[jax.ShapeDtypeStruct((T, D), F32),
                   jax.ShapeDtypeStruct((T * 4, LANES), I32),
                   jax.ShapeDtypeStruct((T, LANES), I32),
                   jax.ShapeDtypeStruct((T, LANES), F32)],
        scratch_shapes=[pltpu.VMEM((D, D), BF16)],
        compiler_params=_cparams(("arbitrary",)),
        name="out_proj_ln_router",
    )(a, x, mod4, w_out, ln_g.reshape(1, D), ln_b.reshape(1, D), router_w, router_b.reshape(1, N_EXP))


def _route_tables(idx128, w128):
    idx = idx128[:, :TOP_K].reshape(N_GROUPS, GROUP, TOP_K)
    w = w128[:, :TOP_K].reshape(N_GROUPS, GROUP, TOP_K)
    onehot = idx[..., None] == jnp.arange(N_EXP, dtype=I32)
    counts = jnp.sum(onehot.astype(I32), axis=(1, 2))
    padded = (counts + MOE_M - 1) // MOE_M * MOE_M
    pad_end = jnp.cumsum(padded, axis=-1)
    tok = jnp.arange(GROUP, dtype=I32)[None, :, None]
    real_keys = (idx * (2 * GROUP) + tok).reshape(N_GROUPS, GROUP * TOP_K)
    fill = jnp.arange(MOE_M, dtype=I32)[None, None, :]
    e_ids = jnp.arange(N_EXP, dtype=I32)[None, :, None]
    fill_keys = jnp.where(fill < (padded - counts)[:, :, None], e_ids * (2 * GROUP) + GROUP + fill,
                          N_EXP * 2 * GROUP + e_ids * MOE_M + fill).reshape(N_GROUPS, N_EXP * MOE_M)
    tail = MOE_SORT_LEN - MOE_ROWS
    tail_keys = jnp.broadcast_to(N_EXP * 2 * GROUP + N_EXP * MOE_M + jnp.arange(tail, dtype=I32), (N_GROUPS, tail))
    keys = jnp.concatenate([real_keys, fill_keys, tail_keys], axis=1)
    vals = jnp.concatenate([w.reshape(N_GROUPS, GROUP * TOP_K),
                            jnp.zeros((N_GROUPS, MOE_SORT_LEN - GROUP * TOP_K), F32)], axis=1)
    sorted_rows = [lax.sort((keys[s], vals[s]), dimension=0, num_keys=1) for s in range(N_GROUPS)]
    keys = jnp.stack([k[:MOE_ROWS] for k, _ in sorted_rows])
    row_w = jnp.stack([v[:MOE_ROWS] for _, v in sorted_rows])
    row_tok = jnp.where(keys < N_EXP * 2 * GROUP, jnp.minimum(keys & (2 * GROUP - 1), GROUP), GROUP)
    row_tok = row_tok.reshape(-1)
    n_used = pad_end[:, -1] // MOE_M
    starts = jnp.arange(MOE_NB, dtype=I32) * MOE_M
    block_e = jnp.sum((starts[None, :, None] >= pad_end[:, None, :]).astype(I32), axis=-1)
    block_e = jnp.minimum(block_e, N_EXP - 1)
    last_e = jnp.take_along_axis(block_e, jnp.maximum(n_used - 1, 0)[:, None], axis=1)
    block_e = jnp.where(jnp.arange(MOE_NB, dtype=I32)[None, :] < n_used[:, None], block_e, last_e)
    return (block_e.reshape(-1).astype(I32), n_used.astype(I32), row_tok,
            row_w.reshape(N_GROUPS * MOE_NB, 1, MOE_M))


def _cast_kernel(x_ref, o_ref):
    o_ref[...] = x_ref[...].astype(BF16)


def _cast_experts(w4, layer, eb=4):
    _, n_e, a, b = w4.shape
    return pl.pallas_call(
        _cast_kernel,
        grid=(n_e // eb,),
        in_specs=[pl.BlockSpec((None, eb, a, b), lambda i: (layer, i, 0, 0))],
        out_specs=pl.BlockSpec((eb, a, b), lambda i: (i, 0, 0)),
        out_shape=jax.ShapeDtypeStruct((n_e, a, b), BF16),
        compiler_params=_cparams(("arbitrary",)),
        name="cast_experts",
    )(w4)


def _moe_stage(tg_ref, ts_ref, x_v, acc, rw_ref, wg_ref, wu_ref, wd_ref, tile_g, tile_c, ys_c, ys_s):
    for mi in range(MOE_M):
        tile_g[pl.ds(mi, 4, stride=MOE_S), :] = x_v[pl.ds(pl.multiple_of(tg_ref[0, mi] * 4, 4), 4), :]
    xb = _unpack_rows(tile_c, MOE_M, MOE_S)
    g = _dot(xb, wg_ref[...])
    u = _dot(xb, wu_ref[...])
    ri = lax.broadcasted_iota(I32, (MOE_M, MOE_M), 0)
    ci = lax.broadcasted_iota(I32, (MOE_M, MOE_M), 1)
    rw_col = jnp.sum(jnp.where(ri == ci, jnp.broadcast_to(rw_ref[...], (MOE_M, MOE_M)), 0.0),
                     axis=1, keepdims=True)
    a = (_silu(g) * u) * rw_col
    y = _dot(a.astype(BF16), wd_ref[...])
    for c in range(D // LANES):
        ys_c[pl.ds(c * MOE_S, MOE_M), :] = y[:, c * LANES:(c + 1) * LANES]
    for m0 in range(0, MOE_M, MOE_U):
        offs = [pl.multiple_of(ts_ref[0, m0 + j] * 8, 8) for j in range(MOE_U)]
        vals = [acc[pl.ds(offs[j], 8), :] + ys_s[pl.ds(m0 + j, 8, stride=MOE_S), :] for j in range(MOE_U)]
        for j in range(MOE_U):
            acc[pl.ds(offs[j], 8), :] = vals[j]


def _moe_kernel(be_ref, nu_ref, tg_ref, ts_ref, x_hbm, rw_ref, wg_ref, wu_ref, wd_ref, out_hbm,
                x_v, acc, tile_a, tile_b, ys_a, ys_b, sem):
    s = pl.program_id(0)
    j = pl.program_id(1)

    @pl.when(j == 0)
    def _():
        cp = pltpu.make_async_copy(x_hbm.at[s], x_v.at[pl.ds(0, GROUP * 4)], sem.at[0])
        cp.start()
        cp.wait()
        x_v[pl.ds(GROUP * 4, 8), :] = jnp.zeros((8, LANES), I32)

        def clear(i, carry):
            acc[pl.ds(pl.multiple_of(i * 8, 8), 8), :] = jnp.zeros((8, LANES), F32)
            return carry

        lax.fori_loop(0, GROUP + 1, clear, 0)
        for t in (tile_a, tile_b):
            t[...] = jnp.zeros(t.shape, I32)
        for y in (ys_a, ys_b):
            y[...] = jnp.zeros(y.shape, F32)

    live = j < nu_ref[s] + 2
    common = (tg_ref, ts_ref, x_v, acc, rw_ref, wg_ref, wu_ref, wd_ref)

    @pl.when(live & (j % 2 == 0))
    def _():
        _moe_stage(*common, tile_a, tile_b, ys_b, ys_a)

    @pl.when(live & (j % 2 == 1))
    def _():
        _moe_stage(*common, tile_b, tile_a, ys_a, ys_b)

    @pl.when(j == MOE_NB + 1)
    def _():
        cp = pltpu.make_async_copy(acc.at[pl.ds(0, GROUP * 8)], out_hbm.at[s], sem.at[1])
        cp.start()
        cp.wait()


def _moe_routed(hp, tables, layer, w_gate, w_up, w_down):
    block_e, n_used, row_tok, row_w = tables
    x3 = hp.reshape(N_GROUPS, GROUP * 4, LANES)
    blk = lambda s, j, d: s * MOE_NB + jnp.clip(j - d, 0, MOE_NB - 1)
    wspec = lambda shape: pl.BlockSpec((None,) + shape, lambda s, j, be, nu: (be[blk(s, j, 1)], 0, 0))
    tspec = lambda d: pl.BlockSpec((None, 1, MOE_M), lambda s, j, be, nu: (blk(s, j, d), 0, 0),
                                   memory_space=pltpu.SMEM)
    tok3 = row_tok.reshape(N_GROUPS * MOE_NB, 1, MOE_M)
    out = pl.pallas_call(
        _moe_kernel,
        grid_spec=pltpu.PrefetchScalarGridSpec(
            num_scalar_prefetch=2,
            grid=(N_GROUPS, MOE_NB + 2),
            in_specs=[tspec(0), tspec(2),
                      pl.BlockSpec(memory_space=pl.ANY),
                      pl.BlockSpec((None, 1, MOE_M), lambda s, j, be, nu: (blk(s, j, 1), 0, 0)),
                      wspec((D, D_EXP)), wspec((D, D_EXP)), wspec((D_EXP, D))],
            out_specs=pl.BlockSpec(memory_space=pl.ANY),
            scratch_shapes=[pltpu.VMEM((GROUP * 4 + 8, LANES), I32),
                            pltpu.VMEM(((GROUP + 1) * 8, LANES), F32),
                            pltpu.VMEM((4 * MOE_S, LANES), I32),
                            pltpu.VMEM((4 * MOE_S, LANES), I32),
                            pltpu.VMEM((8 * MOE_S, LANES), F32),
                            pltpu.VMEM((8 * MOE_S, LANES), F32),
                            pltpu.SemaphoreType.DMA((2,))]),
        out_shape=jax.ShapeDtypeStruct((N_GROUPS, GROUP * 8, LANES), F32),
        compiler_params=_cparams(("arbitrary", "arbitrary")),
        name="moe_routed",
    )(block_e, n_used, tok3, tok3, x3, row_w,
      _cast_experts(w_gate, layer), _cast_experts(w_up, layer), _cast_experts(w_down, layer))
    return out.reshape(T_ALL * 8, LANES)


def _moe_post_kernel(x_ref, hp_ref, r_ref, mod_ref, sg_ref, su_ref, sd_ref, lng_ref, lnb_ref, o_ref,
                     sgb, sub, sdb, *, tm):
    @pl.when(pl.program_id(0) == 0)
    def _():
        sgb[...] = sg_ref[...].astype(BF16)
        sub[...] = su_ref[...].astype(BF16)
        sdb[...] = sd_ref[...].astype(BF16)

    hb = _unpack_rows(hp_ref, tm, 4)
    g = _dot(hb, sgb[...])
    u = _dot(hb, sub[...])
    sh = _dot((_silu(g) * u).astype(BF16), sdb[...])
    routed = jnp.concatenate([r_ref[pl.ds(c, tm, stride=8), :] for c in range(D // LANES)], axis=1)
    r = ALPHA * x_ref[...] + mod_ref[5:6, :] * (routed + sh)
    o_ref[...] = _layer_norm_rows(r, lng_ref[...], lnb_ref[...])


def _moe_post(x, hp, routed, mod4, layer, sg, su, sd, ln_g, ln_b, tm=512):
    T = x.shape[0]
    return pl.pallas_call(
        functools.partial(_moe_post_kernel, tm=tm),
        grid=(T // tm,),
        in_specs=[pl.BlockSpec((tm, D), lambda i: (i, 0)),
                  pl.BlockSpec((tm * 4, LANES), lambda i: (i, 0)),
                  pl.BlockSpec((tm * 8, LANES), lambda i: (i, 0)),
                  pl.BlockSpec((None, None, 6, D), lambda i: (layer, (i * tm) // GROUP, 0, 0)),
                  pl.BlockSpec((D, D_SH), lambda i: (0, 0)),
                  pl.BlockSpec((D, D_SH), lambda i: (0, 0)),
                  pl.BlockSpec((D_SH, D), lambda i: (0, 0)),
                  pl.BlockSpec((1, D), lambda i: (0, 0)),
                  pl.BlockSpec((1, D), lambda i: (0, 0))],
        out_specs=pl.BlockSpec((tm, D), lambda i: (i, 0)),
        out_shape=jax.ShapeDtypeStruct((T, D), F32),
        scratch_shapes=[pltpu.VMEM((D, D_SH), BF16), pltpu.VMEM((D, D_SH), BF16), pltpu.VMEM((D_SH, D), BF16)],
        compiler_params=_cparams(("arbitrary",)),
        name="shared_expert_ln",
    )(x, hp, routed, mod4, sg, su, sd, ln_g.reshape(1, D), ln_b.reshape(1, D))


def _conv_silu(src_ref, w_ref, dst_ref, T, scale):
    L = ML_L
    nc = T // L
    w0, w1, w2 = w_ref[0:1, :], w_ref[1:2, :], w_ref[2:3, :]
    row = lax.broadcasted_iota(I32, (L, ML_DK), 0)

    def body(c, carry):
        r0 = pl.multiple_of(c * L, L)
        cur = src_ref[pl.ds(r0, L), :]
        p0 = pl.multiple_of(jnp.maximum(r0 - 8, 0), 8)
        n0 = pl.multiple_of(jnp.minimum(r0 + L, T - 8), 8)
        prev_row = src_ref[pl.ds(p0, 8), :][7:8, :] * jnp.where(c > 0, 1.0, 0.0).astype(F32)
        next_row = src_ref[pl.ds(n0, 8), :][0:1, :] * jnp.where(c < nc - 1, 1.0, 0.0).astype(F32)
        prev = jnp.where(row == 0, prev_row, pltpu.roll(cur, 1, axis=0))
        nxt = jnp.where(row == L - 1, next_row, pltpu.roll(cur, L - 1, axis=0))
        dst_ref[pl.ds(r0, L), :] = _silu(w0 * prev + w1 * cur + w2 * nxt) * scale
        return carry

    lax.fori_loop(0, nc, body, 0)


def _mlstm_chunk(q, k, v, G, GT, C, n, m, backward):
    L = ML_L
    ri = lax.broadcasted_iota(I32, (L, L), 0)
    ci = lax.broadcasted_iota(I32, (L, L), 1)
    keep = (ci >= ri) if backward else (ci <= ri)
    A = keep.astype(BF16)
    AT = ((ri >= ci) if backward else (ri <= ci)).astype(BF16)
    ic, fc = (2, 3) if backward else (0, 1)
    last = 0 if backward else L - 1
    b_col = _mask_dot(A, _log_sigmoid(G))[:, fc:fc + 1]
    b_row = _dot_mask(_log_sigmoid(GT), AT)[fc:fc + 1, :]
    ig_col = G[:, ic:ic + 1]
    ig_row = GT[ic:ic + 1, :]
    log_d = jnp.where(keep, b_col - b_row + ig_row, -jnp.inf)
    log_inter = b_col + m
    m_t = jnp.maximum(log_inter, jnp.max(log_d, axis=-1, keepdims=True))
    d = jnp.exp(log_d - m_t)
    w_inter = jnp.exp(log_inter - m_t)
    qb, kb, vb = q.astype(BF16), k.astype(BF16), v.astype(BF16)
    s = _dot_nt(qb, kb) * d
    num = _dot(s.astype(BF16), vb) + w_inter * _dot(qb, C.astype(BF16))
    den = jnp.sum(s, axis=-1, keepdims=True) + w_inter * jnp.sum(q * n, axis=-1, keepdims=True)
    h = num / jnp.maximum(jnp.abs(den), jnp.exp(-m_t))
    m_new = m_t[last:last + 1, :]
    w_last = jnp.exp(b_col[last:last + 1, :] - b_col + ig_col - m_new)
    decay = w_inter[last:last + 1, :]
    kw = k * w_last
    C_new = decay * C + _dot_tn(kw.astype(BF16), vb)
    n_new = decay * n + jnp.sum(kw, axis=0, keepdims=True)
    return h, C_new, n_new, m_new


def _mlstm_kernel(q_ref, k_ref, v_ref, og_ref, g_ref, gt_ref, cq_ref, ck_ref, ng_ref, c0_ref, n0_ref, m0_ref,
                  a_ref, c_out, n_out, m_out, qs, ks, hf, hb, cst, nst, mst, *, T):
    L = ML_L
    nc = T // L
    _conv_silu(q_ref, cq_ref, qs, T, ML_DK ** -0.5)
    _conv_silu(k_ref, ck_ref, ks, T, 1.0)
    cst[...] = c0_ref[...]
    nst[...] = n0_ref[...]
    mst[...] = m0_ref[...]
    un = min(SCAN_UNROLL, nc)

    def body(i, carry):
        for direction, out in ((0, hf), (1, hb)):
            C, n, m = cst[direction], nst[direction], mst[direction]
            for u in range(un):
                ci = i * un + u
                c = (nc - 1 - ci) if direction else ci
                r0 = pl.multiple_of(c * L, L)
                h, C, n, m = _mlstm_chunk(
                    qs[pl.ds(r0, L), :], ks[pl.ds(r0, L), :], v_ref[pl.ds(r0, L), :],
                    g_ref[pl.ds(r0, L), :], gt_ref[c], C, n, m, backward=bool(direction))
                out[pl.ds(r0, L), :] = h
            cst[direction] = C
            nst[direction] = n
            mst[direction] = m
        return carry

    lax.fori_loop(0, nc // un, body, 0)
    c_out[...] = cst[...]
    n_out[...] = nst[...]
    m_out[...] = mst[...]

    def finish(c, carry):
        r0 = pl.multiple_of(c * L, L)
        tot = hf[pl.ds(r0, L), :] + hb[pl.ds(r0, L), :]
        mu = jnp.mean(tot, axis=-1, keepdims=True)
        cen = tot - mu
        var = jnp.mean(cen * cen, axis=-1, keepdims=True)
        hn = cen * lax.rsqrt(var + EPS) * ng_ref[...]
        a_ref[pl.ds(r0, L), :] = (hn * _sigmoid(og_ref[pl.ds(r0, L), :])).astype(BF16)
        return carry

    lax.fori_loop(0, nc, finish, 0)


def _mlstm_scan(z, gh, ght, conv_w, norm_g, C0, n0, m0, T, row_blk0, nseq):
    nc = T // ML_L
    qcol, kcol = 0, ML_QK // ML_DK
    vcol, ocol = 2 * ML_QK // ML_DV, (2 * ML_QK + ML_V) // ML_DV
    rb = lambda s: row_blk0 + s
    return pl.pallas_call(
        functools.partial(_mlstm_kernel, T=T),
        grid=(nseq, ML_H),
        in_specs=[pl.BlockSpec((T, ML_DK), lambda s, h: (rb(s), qcol + h)),
                  pl.BlockSpec((T, ML_DK), lambda s, h: (rb(s), kcol + h)),
                  pl.BlockSpec((T, ML_DV), lambda s, h: (rb(s), vcol + h)),
                  pl.BlockSpec((T, ML_DV), lambda s, h: (rb(s), ocol + h)),
                  pl.BlockSpec((None, T, 4), lambda s, h: (h, rb(s), 0)),
                  pl.BlockSpec((None, nc, 4, ML_L), lambda s, h: (h, rb(s), 0, 0)),
                  pl.BlockSpec((3, ML_DK), lambda s, h: (0, qcol + h)),
                  pl.BlockSpec((3, ML_DK), lambda s, h: (0, kcol + h)),
                  pl.BlockSpec((1, ML_DV), lambda s, h: (0, h)),
                  pl.BlockSpec((None, 2, None, ML_DK, ML_DV), lambda s, h: (s, 0, h, 0, 0)),
                  pl.BlockSpec((None, 2, None, 1, ML_DK), lambda s, h: (s, 0, h, 0, 0)),
                  pl.BlockSpec((None, 2, None, 1, 1), lambda s, h: (s, 0, h, 0, 0))],
        out_specs=[pl.BlockSpec((T, ML_DV), lambda s, h: (s, h)),
                   pl.BlockSpec((None, 2, None, ML_DK, ML_DV), lambda s, h: (s, 0, h, 0, 0)),
                   pl.BlockSpec((None, 2, None, 1, ML_DK), lambda s, h: (s, 0, h, 0, 0)),
                   pl.BlockSpec((None, 2, None, 1, 1), lambda s, h: (s, 0, h, 0, 0))],
        out_shape=[jax.ShapeDtypeStruct((nseq * T, ML_V), BF16),
                   jax.ShapeDtypeStruct((nseq, 2, ML_H, ML_DK, ML_DV), F32),
                   jax.ShapeDtypeStruct((nseq, 2, ML_H, 1, ML_DK), F32),
                   jax.ShapeDtypeStruct((nseq, 2, ML_H, 1, 1), F32)],
        scratch_shapes=[pltpu.VMEM((T, ML_DK), F32), pltpu.VMEM((T, ML_DK), F32),
                        pltpu.VMEM((T, ML_DV), F32), pltpu.VMEM((T, ML_DV), F32),
                        pltpu.VMEM((2, ML_DK, ML_DV), F32), pltpu.VMEM((2, 1, ML_DK), F32),
                        pltpu.VMEM((2, 1, 1), F32)],
        compiler_params=_cparams(("arbitrary", "arbitrary")),
        name="mlstm_scan",
    )(z, z, z, z, gh, ght, conv_w, conv_w, norm_g.reshape(1, ML_V), C0, n0, m0)


def _mlstm_layer(x, mod4, layer, j, w_in, gate_b, conv_w, norm_g, C_lat, n_lat, m_lat):
    z = _proj(x, mod4, layer, w_in, j, 2 * ML_QK + 2 * ML_V)
    gates = _proj_small(x, mod4, layer, w_in[j][:, 2 * ML_QK + 2 * ML_V:], gate_b[j])
    gh = gates.reshape(T_ALL, 4, ML_H).transpose(2, 0, 1)
    ght = gh.reshape(ML_H, T_ALL // ML_L, ML_L, 4).transpose(0, 1, 3, 2)
    zC = jnp.zeros((BATCH, 2, ML_H, ML_DK, ML_DV), F32)
    zn = jnp.zeros((BATCH, 2, ML_H, 1, ML_DK), F32)
    zm = jnp.zeros((BATCH, 2, ML_H, 1, 1), F32)
    a_p, Cn, nn, mn = _mlstm_scan(z, gh, ght, conv_w[j], norm_g[j], zC, zn, zm, SEQ, 0, BATCH)
    a_s, _, _, _ = _mlstm_scan(z, gh, ght, conv_w[j], norm_g[j], C_lat[:, j],
                               n_lat[:, j].reshape(DEC_BATCH, 2, ML_H, 1, ML_DK),
                               m_lat[:, j].reshape(DEC_BATCH, 2, ML_H, 1, 1),
                               DEC_SEQ, BATCH * SEQ // DEC_SEQ, DEC_BATCH)
    a = jnp.concatenate([a_p, a_s], axis=0)
    return a, (Cn, nn.reshape(BATCH, 2, ML_H, ML_DK), mn.reshape(BATCH, 2, ML_H))


def _hgrn_chunk(q, k, v, g, St, backward):
    L = HG_L
    ri = lax.broadcasted_iota(I32, (L, L), 0)
    ci = lax.broadcasted_iota(I32, (L, L), 1)
    keep = (ci >= ri) if backward else (ci <= ri)
    ref = L - 1 - L // 2 if backward else L // 2
    last = 0 if backward else L - 1
    b = _mask_dot(keep.astype(BF16), g)
    b_ref = b[ref:ref + 1, :]
    b_last = b[last:last + 1, :]
    qe = (q * jnp.exp(b - b_ref)).astype(BF16)
    ke = (k * jnp.exp(b_ref - b)).astype(BF16)
    vb = v.astype(BF16)
    a = jnp.where(keep, _dot_nt(qe, ke), 0.0)
    o = _dot(a.astype(BF16), vb) + _dot_nt((q * jnp.exp(b)).astype(BF16), St.astype(BF16))
    kd = (k * jnp.exp(b_last - b)).astype(BF16)
    St_new = jnp.exp(b_last) * St + _dot_tn(vb, kd)
    return o, St_new


def _hgrn_kernel(q_ref, i_ref, ff_ref, fb_ref, og_ref, fbf_ref, fbb_ref, lbr_ref, ng_ref, s0_ref,
                 a_ref, s_out, of, ob, st, *, T, lb_layer):
    L = HG_L
    nc = T // L
    raw = lbr_ref[...]
    e = jnp.exp(raw - jnp.max(raw, axis=0, keepdims=True))
    p = e / jnp.sum(e, axis=0, keepdims=True)
    lb = jnp.sum(p[0:lb_layer + 1, :], axis=0, keepdims=True) - p[0:1, :]
    st[0] = s0_ref[0].T
    st[1] = s0_ref[1].T
    un = min(SCAN_UNROLL, nc)

    def body(i, carry):
        for direction, out, f_ref, b_ref in ((0, of, ff_ref, fbf_ref), (1, ob, fb_ref, fbb_ref)):
            St = st[direction]
            for u in range(un):
                ci = i * un + u
                c = (nc - 1 - ci) if direction else ci
                r0 = pl.multiple_of(c * L, L)
                f = lb + (1.0 - lb) * _sigmoid(f_ref[pl.ds(r0, L), :] + b_ref[...])
                o, St = _hgrn_chunk(_silu(q_ref[pl.ds(r0, L), :]), 1.0 - f, i_ref[pl.ds(r0, L), :],
                                    jnp.log(f), St, backward=bool(direction))
                out[pl.ds(r0, L), :] = o
            st[direction] = St
        return carry

    lax.fori_loop(0, nc // un, body, 0)
    s_out[0] = st[0].T
    s_out[1] = st[1].T

    def finish(c, carry):
        r0 = pl.multiple_of(c * L, L)
        tot = of[pl.ds(r0, L), :] + ob[pl.ds(r0, L), :]
        on = tot * lax.rsqrt(jnp.mean(tot * tot, axis=-1, keepdims=True) + EPS) * ng_ref[...]
        a_ref[pl.ds(r0, L), :] = (on * _silu(og_ref[pl.ds(r0, L), :])).astype(BF16)
        return carry

    lax.fori_loop(0, nc, finish, 0)


def _hgrn_scan(z, f_b, lb_raw, norm_g, S0, T, row_blk0, nseq, lb_layer):
    nh = HG_H
    rb = lambda s: row_blk0 + s
    zspec = lambda cb: pl.BlockSpec((T, HG_DK), lambda s, h: (rb(s), cb * nh + h))
    return pl.pallas_call(
        functools.partial(_hgrn_kernel, T=T, lb_layer=lb_layer),
        grid=(nseq, nh),
        in_specs=[zspec(0), zspec(1), zspec(2), zspec(3), zspec(4),
                  pl.BlockSpec((1, HG_DK), lambda s, h: (0, h)),
                  pl.BlockSpec((1, HG_DK), lambda s, h: (0, nh + h)),
                  pl.BlockSpec((DEPTH, HG_DK), lambda s, h: (0, h)),
                  pl.BlockSpec((1, HG_DV), lambda s, h: (0, h)),
                  pl.BlockSpec((None, 2, None, HG_DK, HG_DV), lambda s, h: (s, 0, h, 0, 0))],
        out_specs=[pl.BlockSpec((T, HG_DV), lambda s, h: (s, h)),
                   pl.BlockSpec((None, 2, None, HG_DK, HG_DV), lambda s, h: (s, 0, h, 0, 0))],
        out_shape=[jax.ShapeDtypeStruct((nseq * T, HG_V), BF16),
                   jax.ShapeDtypeStruct((nseq, 2, nh, HG_DK, HG_DV), F32)],
        scratch_shapes=[pltpu.VMEM((T, HG_DV), F32), pltpu.VMEM((T, HG_DV), F32),
                        pltpu.VMEM((2, HG_DV, HG_DK), F32)],
        compiler_params=_cparams(("arbitrary", "arbitrary")),
        name="hgrn_scan",
    )(z, z, z, z, z, f_b.reshape(1, 2 * HG_K), f_b.reshape(1, 2 * HG_K), lb_raw, norm_g.reshape(1, HG_V), S0)


def _hgrn_layer(x, mod4, layer, j, w_in, f_b, lb_raw, norm_g, S_lat):
    z = _proj(x, mod4, layer, w_in, j, 3 * HG_K + 2 * HG_V)
    zS = jnp.zeros((BATCH, 2, HG_H, HG_DK, HG_DV), F32)
    a_p, Sn = _hgrn_scan(z, f_b[j], lb_raw, norm_g[j], zS, SEQ, 0, BATCH, layer)
    a_s, _ = _hgrn_scan(z, f_b[j], lb_raw, norm_g[j], S_lat[:, j], DEC_SEQ, BATCH * SEQ // DEC_SEQ, DEC_BATCH, layer)
    return jnp.concatenate([a_p, a_s], axis=0), Sn


def _head_rms(x, g_tiled, nheads):
    lane = lax.broadcasted_iota(I32, x.shape, 1)
    sq = x * x
    ms = jnp.zeros_like(x)
    for h in range(nheads):
        in_h = (lane >= h * AT_HD) & (lane < (h + 1) * AT_HD)
        tot = jnp.sum(jnp.where(in_h, sq, 0.0), axis=-1, keepdims=True)
        ms = jnp.where(in_h, tot, ms)
    return x * lax.rsqrt(ms * (1.0 / AT_HD) + EPS) * g_tiled


def _rope(x, cos, sin):
    w = x.shape[1]
    lane = lax.broadcasted_iota(I32, x.shape, 1)
    up = pltpu.roll(x, w - 16, axis=1)
    down = pltpu.roll(x, 16, axis=1)
    swapped = jnp.where((lane % 32) < 16, up, down)
    return x * cos + swapped * sin


def _attn_ctx_kernel(q_ref, k_ref, v_ref, qn_ref, kn_ref, o_ref, ko_ref, vo_ref):
    k = _head_rms(k_ref[...], kn_ref[...], AT_KV)
    ko_ref[...] = k
    v = v_ref[...]
    vo_ref[...] = v
    kb, vb = k.astype(BF16), v.astype(BF16)
    outs = []
    for kv in range(AT_KV):
        q = _head_rms(q_ref[:, kv * 256:(kv + 1) * 256], qn_ref[...], AT_G) * (AT_HD ** -0.5)
        kh = kb[:, kv * AT_HD:(kv + 1) * AT_HD]
        vh = vb[:, kv * AT_HD:(kv + 1) * AT_HD]
        for g in range(AT_G):
            s = _dot_nt(q[:, g * AT_HD:(g + 1) * AT_HD].astype(BF16), kh)
            p = jnp.exp(s - jnp.max(s, axis=-1, keepdims=True))
            p = p / jnp.sum(p, axis=-1, keepdims=True)
            outs.append(_dot(p.astype(BF16), vh))
    o_ref[...] = jnp.concatenate(outs, axis=1).astype(BF16)


def _attn_ctx(z, q_norm, k_norm):
    qn = jnp.tile(q_norm, AT_G).reshape(1, 256)
    kn = jnp.tile(k_norm, AT_KV).reshape(1, 256)
    return pl.pallas_call(
        _attn_ctx_kernel,
        grid=(BATCH,),
        in_specs=[pl.BlockSpec((SEQ, D), lambda b: (b, 0)),
                  pl.BlockSpec((SEQ, 256), lambda b: (b, 4)),
                  pl.BlockSpec((SEQ, 256), lambda b: (b, 5)),
                  pl.BlockSpec((1, 256), lambda b: (0, 0)),
                  pl.BlockSpec((1, 256), lambda b: (0, 0))],
        out_specs=[pl.BlockSpec((SEQ, D), lambda b: (b, 0)),
                   pl.BlockSpec((SEQ, 256), lambda b: (b, 0)),
                   pl.BlockSpec((SEQ, 256), lambda b: (b, 0))],
        out_shape=[jax.ShapeDtypeStruct((BATCH * SEQ, D), BF16),
                   jax.ShapeDtypeStruct((BATCH * SEQ, 256), F32),
                   jax.ShapeDtypeStruct((BATCH * SEQ, 256), F32)],
        compiler_params=_cparams(("arbitrary",)),
        name="attn_context",
    )(z, z, z, qn, kn)


def _attn_kv_kernel(k_ref, v_ref, kn_ref, cos_ref, sin_ref, ko_ref, vo_ref):
    k = _rope(_head_rms(k_ref[...], kn_ref[...], AT_KV), cos_ref[...], sin_ref[...])
    v = v_ref[...]
    for h in range(AT_KV):
        ko_ref[h] = k[:, h * AT_HD:(h + 1) * AT_HD].astype(BF16)
        vo_ref[h] = v[:, h * AT_HD:(h + 1) * AT_HD].astype(BF16)


def _attn_kv(z, k_norm, cos4, sin4, tt=512):
    kn = jnp.tile(k_norm, AT_KV).reshape(1, 256)
    nt = DEC_SEQ // tt
    row0 = BATCH * SEQ // tt
    return pl.pallas_call(
        _attn_kv_kernel,
        grid=(DEC_BATCH, nt),
        in_specs=[pl.BlockSpec((tt, 256), lambda b, i: (row0 + b * nt + i, 4)),
                  pl.BlockSpec((tt, 256), lambda b, i: (row0 + b * nt + i, 5)),
                  pl.BlockSpec((1, 256), lambda b, i: (0, 0)),
                  pl.BlockSpec((tt, 256), lambda b, i: (i, 0)),
                  pl.BlockSpec((tt, 256), lambda b, i: (i, 0))],
        out_specs=[pl.BlockSpec((None, AT_KV, tt, AT_HD), lambda b, i: (b, 0, i, 0)),
                   pl.BlockSpec((None, AT_KV, tt, AT_HD), lambda b, i: (b, 0, i, 0))],
        out_shape=[jax.ShapeDtypeStruct((DEC_BATCH, AT_KV, DEC_SEQ, AT_HD), BF16),
                   jax.ShapeDtypeStruct((DEC_BATCH, AT_KV, DEC_SEQ, AT_HD), BF16)],
        compiler_params=_cparams(("arbitrary", "arbitrary")),
        name="attn_kv_prep",
    )(z, z, kn, cos4, sin4)


def _attn_lat_kernel(q_ref, qn_ref, cos_ref, sin_ref, k_ref, v_ref, o_ref, *, tq, tk):
    q = _rope(_head_rms(q_ref[...], qn_ref[...], AT_G), cos_ref[...], sin_ref[...]) * (AT_HD ** -0.5)
    qs = jnp.concatenate([q[:, g * AT_HD:(g + 1) * AT_HD] for g in range(AT_G)], axis=0).astype(BF16)
    rows = AT_G * tq
    nk = k_ref.shape[0] // tk

    def body(j, carry):
        m, l, acc = carry
        k0 = pl.multiple_of(j * tk, tk)
        s = _dot_nt(qs, k_ref[pl.ds(k0, tk), :])
        m_new = jnp.maximum(m, jnp.max(s, axis=-1, keepdims=True))
        alpha = jnp.exp(m - m_new)
        p = jnp.exp(s - m_new)
        l = alpha * l + jnp.sum(p, axis=-1, keepdims=True)
        acc = alpha * acc + _dot(p.astype(BF16), v_ref[pl.ds(k0, tk), :])
        return m_new, l, acc

    init = (jnp.full((rows, 1), -jnp.inf, F32), jnp.zeros((rows, 1), F32), jnp.zeros((rows, AT_HD), F32))
    _, l, acc = lax.fori_loop(0, nk, body, init)
    o = acc / l
    o_ref[...] = jnp.concatenate([o[g * tq:(g + 1) * tq, :] for g in range(AT_G)], axis=1).astype(BF16)


def _attn_lat(z, q_norm, cos4, sin4, kk, vv, tq=256, tk=512):
    qn = jnp.tile(q_norm, AT_G).reshape(1, 256)
    nq = DEC_SEQ // tq
    row0 = BATCH * SEQ // tq
    skv = kk.shape[2]
    return pl.pallas_call(
        functools.partial(_attn_lat_kernel, tq=tq, tk=tk),
        grid=(DEC_BATCH, AT_KV, nq),
        in_specs=[pl.BlockSpec((tq, 256), lambda b, h, i: (row0 + b * nq + i, h)),
                  pl.BlockSpec((1, 256), lambda b, h, i: (0, 0)),
                  pl.BlockSpec((tq, 256), lambda b, h, i: (i, 0)),
                  pl.BlockSpec((tq, 256), lambda b, h, i: (i, 0)),
                  pl.BlockSpec((None, None, skv, AT_HD), lambda b, h, i: (b, h, 0, 0)),
                  pl.BlockSpec((None, None, skv, AT_HD), lambda b, h, i: (b, h, 0, 0))],
        out_specs=pl.BlockSpec((tq, 256), lambda b, h, i: (b * nq + i, h)),
        out_shape=jax.ShapeDtypeStruct((DEC_BATCH * DEC_SEQ, D), BF16),
        compiler_params=_cparams(("arbitrary", "arbitrary", "arbitrary")),
        name="attn_latent",
    )(z, qn, cos4, sin4, kk, vv)


def _rope_tables():
    t = jnp.arange(DEC_SEQ)
    row = (t // GRID_W).astype(F32)
    col = (t % GRID_W).astype(F32)
    nf = AT_HD // 4
    inv = ROPE_THETA ** (-jnp.arange(nf, dtype=F32) / nf)
    ar, ac = row[:, None] * inv[None], col[:, None] * inv[None]
    cos = jnp.concatenate([jnp.cos(ar), jnp.cos(ar), jnp.cos(ac), jnp.cos(ac)], axis=1)
    sin = jnp.concatenate([-jnp.sin(ar), jnp.sin(ar), -jnp.sin(ac), jnp.sin(ac)], axis=1)
    return jnp.tile(cos, (1, 4)), jnp.tile(sin, (1, 4))


def _attn_layer(x, mod4, layer, j, w_in, q_norm, k_norm, cache_k, cache_v):
    z = _proj(x, mod4, layer, w_in, j, (AT_H + 2 * AT_KV) * AT_HD)
    a_p, k_new, v_new = _attn_ctx(z, q_norm[j], k_norm[j])
    cos4, sin4 = _rope_tables()
    k_lat, v_lat = _attn_kv(z, k_norm[j], cos4, sin4)
    kk = jnp.concatenate([cache_k[:, j].transpose(0, 2, 1, 3).astype(BF16), k_lat], axis=2)
    vv = jnp.concatenate([cache_v[:, j].transpose(0, 2, 1, 3).astype(BF16), v_lat], axis=2)
    a_s = _attn_lat(z, q_norm[j], cos4, sin4, kk, vv)
    a = jnp.concatenate([a_p, a_s], axis=0)
    return a, (k_new.reshape(BATCH, SEQ, AT_KV, AT_HD), v_new.reshape(BATCH, SEQ, AT_KV, AT_HD))


def kernel(x_prompt, x_sample, state_mlstm_C, state_mlstm_n, state_mlstm_m, cache_attn_k, cache_attn_v, state_hgrn_S, c, c_ctx, mod_w, mod_b, ln_g, ln_b, mlstm_w_in, mlstm_gate_b, mlstm_conv, mlstm_norm, mlstm_w_out, attn_w_in, attn_q_norm, attn_k_norm, attn_w_out, hgrn_w_in, hgrn_f_b, hgrn_lower_bounds, hgrn_norm, hgrn_w_out, moe_router, moe_router_b, moe_w_gate, moe_w_up, moe_w_down, moe_sh_gate, moe_sh_up, moe_sh_down):
    x = jnp.concatenate([x_prompt.reshape(BATCH * SEQ, D), x_sample.reshape(DEC_BATCH * DEC_SEQ, D)], axis=0)
    cond8 = jnp.concatenate([c_ctx[None], c, jnp.zeros((8 - 1 - DEC_BATCH, D), F32)], axis=0)
    mod4 = _mod_all(cond8, mod_w, mod_b)

    new_C, new_n, new_m, new_k, new_v, new_S = [], [], [], [], [], []
    for l in range(DEPTH):
        kind, j = l % 3, l // 3
        if kind == 0:
            a, (Cn, nn, mn) = _mlstm_layer(x, mod4, l, j, mlstm_w_in, mlstm_gate_b, mlstm_conv, mlstm_norm,
                                           state_mlstm_C, state_mlstm_n, state_mlstm_m)
            new_C.append(Cn)
            new_n.append(nn)
            new_m.append(mn)
            w_out = mlstm_w_out[j]
        elif kind == 1:
            a, (kn, vn) = _attn_layer(x, mod4, l, j, attn_w_in, attn_q_norm, attn_k_norm, cache_attn_k, cache_attn_v)
            new_k.append(kn)
            new_v.append(vn)
            w_out = attn_w_out[j]
        else:
            a, Sn = _hgrn_layer(x, mod4, l, j, hgrn_w_in, hgrn_f_b, hgrn_lower_bounds, hgrn_norm, state_hgrn_S)
            new_S.append(Sn)
            w_out = hgrn_w_out[j]
        x, hp, idx128, w128 = _post(a, x, mod4, l, w_out, ln_g[l, 0], ln_b[l, 0], moe_router[l], moe_router_b[l])
        routed = _moe_routed(hp, _route_tables(idx128, w128), l, moe_w_gate, moe_w_up, moe_w_down)
        x = _moe_post(x, hp, routed, mod4, l, moe_sh_gate[l], moe_sh_up[l], moe_sh_down[l], ln_g[l, 1], ln_b[l, 1])

    xp = x[:BATCH * SEQ].reshape(BATCH, SEQ, D)
    xs = x[BATCH * SEQ:].reshape(DEC_BATCH, DEC_SEQ, D)
    return (xp, xs, jnp.stack(new_C, 1), jnp.stack(new_n, 1), jnp.stack(new_m, 1),
            jnp.stack(new_k, 1), jnp.stack(new_v, 1), jnp.stack(new_S, 1))
```

```python
import functools

import jax
import jax.numpy as jnp
from jax import lax
from jax.experimental import pallas as pl
from jax.experimental.pallas import tpu as pltpu

F32 = jnp.float32
BF16 = jnp.bfloat16
I32 = jnp.int32
HI = lax.Precision.HIGHEST

D = 1024
BATCH, SEQ = 16, 256
DEPTH = 4
DEC_BATCH, DEC_SEQ = 2, 4096
PAST = 512
GRID_W = 64
GROUP = 4096
N_GROUPS = 3
T_ALL = N_GROUPS * GROUP

ML_H, ML_DK, ML_DV, ML_L = 4, 128, 256, 128
ML_QK, ML_V = ML_H * ML_DK, ML_H * ML_DV
AT_H, AT_KV, AT_HD, AT_G = 16, 4, 64, 4
HG_H, HG_DK, HG_DV, HG_L = 8, 128, 128, 64
HG_K, HG_V = HG_H * HG_DK, HG_H * HG_DV
N_EXP, TOP_K, D_EXP, D_SH = 64, 8, 256, 256
ROUTE_SCALE = 2.5
ALPHA = (2 * DEPTH) ** 0.25
EPS = 1e-6
ROPE_THETA = 10000.0

VMEM_LIMIT = 56 * 1024 * 1024
LANES = 128

MOE_M = 256
MOE_NB = GROUP * TOP_K // MOE_M + N_EXP
MOE_ROWS = MOE_NB * MOE_M
MOE_SORT_LEN = 65536
MOE_S = MOE_M + 8
MOE_U = 8
MOE_TG = 8
SCAN_UNROLL = 4


def _cparams(sem):
    return pltpu.CompilerParams(dimension_semantics=sem, vmem_limit_bytes=VMEM_LIMIT)


def _sigmoid(x):
    return 1.0 / (1.0 + jnp.exp(-x))


def _silu(x):
    return x * _sigmoid(x)


def _log_sigmoid(x):
    return jnp.minimum(x, 0.0) - jnp.log(1.0 + jnp.exp(-jnp.abs(x)))


def _dot(a, b):
    return jnp.dot(a, b, preferred_element_type=F32)


def _dot_nt(a, b):
    return lax.dot_general(a, b, (((1,), (1,)), ((), ())), preferred_element_type=F32)


def _dot_tn(a, b):
    return lax.dot_general(a, b, (((0,), (0,)), ((), ())), preferred_element_type=F32)


def _dot_hi(a, b):
    return jnp.dot(a, b, preferred_element_type=F32, precision=HI)


def _split3(x):
    p0 = x.astype(BF16)
    r1 = x - p0.astype(F32)
    p1 = r1.astype(BF16)
    p2 = (r1 - p1.astype(F32)).astype(BF16)
    return p0, p1, p2


def _mask_dot(mask_bf16, x):
    p0, p1, p2 = _split3(x)
    return _dot(mask_bf16, p0) + _dot(mask_bf16, p1) + _dot(mask_bf16, p2)


def _dot_mask(x, mask_bf16):
    p0, p1, p2 = _split3(x)
    return _dot(p0, mask_bf16) + _dot(p1, mask_bf16) + _dot(p2, mask_bf16)


def _dot_3x(a, b):
    a0 = a.astype(BF16)
    a1 = (a - a0.astype(F32)).astype(BF16)
    b0 = b.astype(BF16)
    b1 = (b - b0.astype(F32)).astype(BF16)
    return _dot(a0, b0) + _dot(a0, b1) + _dot(a1, b0)


def _mod_kernel(cond_ref, w_ref, b_ref, o_ref):
    o_ref[...] = _dot_hi(_silu(cond_ref[...]), w_ref[...]) + b_ref[...]


def _mod_all(cond8, mod_w, mod_b):
    tn = 1024
    out = pl.pallas_call(
        _mod_kernel,
        grid=(DEPTH, 6 * D // tn),
        in_specs=[pl.BlockSpec((8, D), lambda l, j: (0, 0)),
                  pl.BlockSpec((None, D, tn), lambda l, j: (l, 0, j)),
                  pl.BlockSpec((None, 1, tn), lambda l, j: (l, 0, j))],
        out_specs=pl.BlockSpec((None, 8, tn), lambda l, j: (l, 0, j)),
        out_shape=jax.ShapeDtypeStruct((DEPTH, 8, 6 * D), F32),
        compiler_params=_cparams(("arbitrary", "arbitrary")),
        name="mod_rows",
    )(cond8, mod_w, mod_b.reshape(DEPTH, 1, 6 * D))
    return out.reshape(DEPTH, 8, 6, D)


def _proj_kernel(x_ref, mod_ref, w_ref, o_ref, h_scr):
    @pl.when(pl.program_id(1) == 0)
    def _():
        h_scr[...] = (x_ref[...] * (1.0 + mod_ref[1:2, :]) + mod_ref[0:1, :]).astype(BF16)

    o_ref[...] = _dot(h_scr[...], w_ref[...].astype(BF16)).astype(o_ref.dtype)


def _proj(x, mod4, layer, w3, widx, ncols, out_dtype, tm=2048, tn=512):
    T = x.shape[0]
    return pl.pallas_call(
        _proj_kernel,
        grid=(T // tm, ncols // tn),
        in_specs=[pl.BlockSpec((tm, D), lambda i, j: (i, 0)),
                  pl.BlockSpec((None, None, 6, D), lambda i, j: (layer, (i * tm) // GROUP, 0, 0)),
                  pl.BlockSpec((None, D, tn), lambda i, j: (widx, 0, j))],
        out_specs=pl.BlockSpec((tm, tn), lambda i, j: (i, j)),
        out_shape=jax.ShapeDtypeStruct((T, ncols), out_dtype),
        scratch_shapes=[pltpu.VMEM((tm, D), BF16)],
        compiler_params=_cparams(("arbitrary", "arbitrary")),
        name="in_proj",
    )(x, mod4, w3)


def _proj_small_kernel(x_ref, mod_ref, w_ref, b_ref, o_ref):
    h = x_ref[...] * (1.0 + mod_ref[1:2, :]) + mod_ref[0:1, :]
    o_ref[...] = _dot_hi(h, w_ref[...]) + b_ref[...]


def _proj_small(x, mod4, layer, w, b, tm=1024):
    T, n = x.shape[0], w.shape[1]
    return pl.pallas_call(
        _proj_small_kernel,
        grid=(T // tm,),
        in_specs=[pl.BlockSpec((tm, D), lambda i: (i, 0)),
                  pl.BlockSpec((None, None, 6, D), lambda i: (layer, (i * tm) // GROUP, 0, 0)),
                  pl.BlockSpec((D, n), lambda i: (0, 0)),
                  pl.BlockSpec((1, n), lambda i: (0, 0))],
        out_specs=pl.BlockSpec((tm, n), lambda i: (i, 0)),
        out_shape=jax.ShapeDtypeStruct((T, n), F32),
        compiler_params=_cparams(("arbitrary",)),
        name="gate_proj",
    )(x, mod4, w, b.reshape(1, n))


def _layer_norm_rows(r, g, b):
    mu = jnp.mean(r, axis=-1, keepdims=True)
    c = r - mu
    var = jnp.mean(c * c, axis=-1, keepdims=True)
    return c * lax.rsqrt(var + EPS) * g + b


def _pack_bf16_pairs(lo, hi):
    lo_b = lax.bitcast_convert_type(lo.astype(BF16).astype(F32), I32)
    hi_b = lax.bitcast_convert_type(hi.astype(BF16).astype(F32), I32)
    return lax.shift_right_logical(lo_b, 16) | (hi_b & jnp.int32(-65536))


def _unpack_bf16_pairs(v):
    lo = lax.bitcast_convert_type(lax.shift_left(v, 16), F32)
    hi = lax.bitcast_convert_type(v & jnp.int32(-65536), F32)
    return lo.astype(BF16), hi.astype(BF16)


def _unpack_rows(ref, rows, stride, offset=0):
    chunks = []
    for r in range(4):
        if stride == 4:
            v = ref[pl.ds(r, rows, stride=4), :]
        else:
            v = ref[pl.ds(offset + r * stride, rows), :]
        chunks.extend(_unpack_bf16_pairs(v))
    return jnp.concatenate(chunks, axis=1)


def _post_kernel(a_ref, x_ref, mod_ref, w_ref, lng_ref, lnb_ref, rw_ref, rb_ref,
                 xo_ref, hp_ref, idx_ref, wt_ref, wb_scr, *, tm):
    @pl.when(pl.program_id(0) == 0)
    def _():
        wb_scr[...] = w_ref[...].astype(BF16)

    y = _dot(a_ref[...], wb_scr[...])
    r = ALPHA * x_ref[...] + mod_ref[2:3, :] * y
    xn = _layer_norm_rows(r, lng_ref[...], lnb_ref[...])
    xo_ref[...] = xn
    h1 = xn * (1.0 + mod_ref[4:5, :]) + mod_ref[3:4, :]
    for p in range(4):
        lo = h1[:, (2 * p) * LANES:(2 * p + 1) * LANES]
        hi = h1[:, (2 * p + 1) * LANES:(2 * p + 2) * LANES]
        hp_ref[pl.ds(p, tm, stride=4), :] = _pack_bf16_pairs(lo, hi)

    scores = _sigmoid(_dot_3x(h1, rw_ref[...]))
    sel = scores + rb_ref[...]
    e_iota = lax.broadcasted_iota(I32, (tm, N_EXP), 1)
    lane = lax.broadcasted_iota(I32, (tm, LANES), 1)
    idx_out = jnp.zeros((tm, LANES), I32)
    w_out = jnp.zeros((tm, LANES), F32)
    total = jnp.zeros((tm, 1), F32)
    for k in range(TOP_K):
        mx = jnp.max(sel, axis=-1, keepdims=True)
        ik = jnp.min(jnp.where(sel == mx, e_iota, N_EXP), axis=-1, keepdims=True)
        hit = e_iota == ik
        wk = jnp.sum(jnp.where(hit, scores, 0.0), axis=-1, keepdims=True)
        sel = jnp.where(hit, -jnp.inf, sel)
        total = total + wk
        idx_out = jnp.where(lane == k, ik, idx_out)
        w_out = jnp.where(lane == k, wk, w_out)
    idx_ref[...] = idx_out
    wt_ref[...] = w_out / total * ROUTE_SCALE


def _post(a, x, mod4, layer, w_out, ln_g, ln_b, router_w, router_b, tm=512):
    T = x.shape[0]
    return pl.pallas_call(
        functools.partial(_post_kernel, tm=tm),
        grid=(T // tm,),
        in_specs=[pl.BlockSpec((tm, D), lambda i: (i, 0)),
                  pl.BlockSpec((tm, D), lambda i: (i, 0)),
                  pl.BlockSpec((None, None, 6, D), lambda i: (layer, (i * tm) // GROUP, 0, 0)),
                  pl.BlockSpec((D, D), lambda i: (0, 0)),
                  pl.BlockSpec((1, D), lambda i: (0, 0)),
                  pl.BlockSpec((1, D), lambda i: (0, 0)),
                  pl.BlockSpec((D, N_EXP), lambda i: (0, 0)),
                  pl.BlockSpec((1, N_EXP), lambda i: (0, 0))],
        out_specs=[pl.BlockSpec((tm, D), lambda i: (i, 0)),
                   pl.BlockSpec((tm * 4, LANES), lambda i: (i, 0)),
                   pl.BlockSpec((tm, LANES), lambda i: (i, 0)),
                   pl.BlockSpec((tm, LANES), lambda i: (i, 0))],
        out_shape=[jax.ShapeDtypeStruct((T, D), F32),
                   jax.ShapeDtypeStruct((T * 4, LANES), I32),
                   jax.ShapeDtypeStruct((T, LANES), I32),
                   jax.ShapeDtypeStruct((T, LANES), F32)],
        scratch_shapes=[pltpu.VMEM((D, D), BF16)],
        compiler_params=_cparams(("arbitrary",)),
        name="out_proj_ln_router",
    )(a, x, mod4, w_out, ln_g.reshape(1, D), ln_b.reshape(1, D), router_w, router_b.reshape(1, N_EXP))


def _route_tables(idx128, w128):
    idx = idx128[:, :TOP_K].reshape(N_GROUPS, GROUP, TOP_K)
    w = w128[:, :TOP_K].reshape(N_GROUPS, GROUP, TOP_K)
    onehot = idx[..., None] == jnp.arange(N_EXP, dtype=I32)
    counts = jnp.sum(onehot.astype(I32), axis=(1, 2))
    padded = (counts + MOE_M - 1) // MOE_M * MOE_M
    pad_end = jnp.cumsum(padded, axis=-1)
    tok = jnp.arange(GROUP, dtype=I32)[None, :, None]
    real_keys = (idx * (2 * GROUP) + tok).reshape(N_GROUPS, GROUP * TOP_K)
    fill = jnp.arange(MOE_M, dtype=I32)[None, None, :]
    e_ids = jnp.arange(N_EXP, dtype=I32)[None, :, None]
    fill_keys = jnp.where(fill < (padded - counts)[:, :, None], e_ids * (2 * GROUP) + GROUP + fill,
                          N_EXP * 2 * GROUP + e_ids * MOE_M + fill).reshape(N_GROUPS, N_EXP * MOE_M)
    tail = MOE_SORT_LEN - MOE_ROWS
    tail_keys = jnp.broadcast_to(N_EXP * 2 * GROUP + N_EXP * MOE_M + jnp.arange(tail, dtype=I32), (N_GROUPS, tail))
    keys = jnp.concatenate([real_keys, fill_keys, tail_keys], axis=1)
    vals = jnp.concatenate([w.reshape(N_GROUPS, GROUP * TOP_K),
                            jnp.zeros((N_GROUPS, MOE_SORT_LEN - GROUP * TOP_K), F32)], axis=1)
    sorted_rows = [lax.sort((keys[s], vals[s]), dimension=0, num_keys=1) for s in range(N_GROUPS)]
    keys = jnp.stack([k[:MOE_ROWS] for k, _ in sorted_rows])
    row_w = jnp.stack([v[:MOE_ROWS] for _, v in sorted_rows])
    row_tok = jnp.where(keys < N_EXP * 2 * GROUP, jnp.minimum(keys & (2 * GROUP - 1), GROUP), GROUP)
    row_tok = row_tok.reshape(-1)
    n_used = pad_end[:, -1] // MOE_M
    starts = jnp.arange(MOE_NB, dtype=I32) * MOE_M
    block_e = jnp.sum((starts[None, :, None] >= pad_end[:, None, :]).astype(I32), axis=-1)
    block_e = jnp.minimum(block_e, N_EXP - 1)
    last_e = jnp.take_along_axis(block_e, jnp.maximum(n_used - 1, 0)[:, None], axis=1)
    block_e = jnp.where(jnp.arange(MOE_NB, dtype=I32)[None, :] < n_used[:, None], block_e, last_e)
    return (block_e.reshape(-1).astype(I32), n_used.astype(I32), row_tok,
            row_w.reshape(N_GROUPS * MOE_NB // MOE_TG, MOE_TG, 1, MOE_M))


def _cast_kernel(x_ref, o_ref):
    o_ref[...] = x_ref[...].astype(BF16)


def _cast_experts(w4, layer, eb=4):
    _, n_e, a, b = w4.shape
    return pl.pallas_call(
        _cast_kernel,
        grid=(n_e // eb,),
        in_specs=[pl.BlockSpec((None, eb, a, b), lambda i: (layer, i, 0, 0))],
        out_specs=pl.BlockSpec((eb, a, b), lambda i: (i, 0, 0)),
        out_shape=jax.ShapeDtypeStruct((n_e, a, b), BF16),
        compiler_params=_cparams(("arbitrary",)),
        name="cast_experts",
    )(w4)


def _moe_stage(rows, tg_ref, ts_ref, x_v, acc, rw_ref, wg_ref, wu_ref, wd_ref, tile_g, tile_c, ys_c, ys_s):
    rg, rc, rs = rows
    for mi in range(MOE_M):
        tile_g[pl.ds(mi, 4, stride=MOE_S), :] = x_v[pl.ds(pl.multiple_of(tg_ref[rg, mi], 4), 4), :]
    xb = _unpack_rows(tile_c, MOE_M, MOE_S)
    g = _dot(xb, wg_ref[...])
    u = _dot(xb, wu_ref[...])
    ri = lax.broadcasted_iota(I32, (MOE_M, MOE_M), 0)
    ci = lax.broadcasted_iota(I32, (MOE_M, MOE_M), 1)
    rw_col = jnp.sum(jnp.where(ri == ci, jnp.broadcast_to(rw_ref[rc], (MOE_M, MOE_M)), 0.0),
                     axis=1, keepdims=True)
    a = (_silu(g) * u) * rw_col
    y = _dot(a.astype(BF16), wd_ref[...])
    for c in range(D // LANES):
        ys_c[pl.ds(c * MOE_S, MOE_M), :] = y[:, c * LANES:(c + 1) * LANES]
    for m0 in range(0, MOE_M, MOE_U):
        offs = [pl.multiple_of(ts_ref[rs, m0 + j], 8) for j in range(MOE_U)]
        vals = [acc[pl.ds(offs[j], 8), :] + ys_s[pl.ds(m0 + j, 8, stride=MOE_S), :] for j in range(MOE_U)]
        for j in range(MOE_U):
            acc[pl.ds(offs[j], 8), :] = vals[j]


def _moe_kernel(be_ref, nu_ref, tg_ref, ts_ref, x_hbm, rw_ref, wg_ref, wu_ref, wd_ref, out_hbm,
                x_v, acc, tile_a, tile_b, ys_a, ys_b, sem):
    s = pl.program_id(0)
    j = pl.program_id(1)

    @pl.when(j == 0)
    def _():
        cp = pltpu.make_async_copy(x_hbm.at[s], x_v.at[pl.ds(0, GROUP * 4)], sem.at[0])
        cp.start()
        cp.wait()
        x_v[pl.ds(GROUP * 4, 8), :] = jnp.zeros((8, LANES), I32)

        def clear(i, carry):
            acc[pl.ds(pl.multiple_of(i * 8, 8), 8), :] = jnp.zeros((8, LANES), F32)
            return carry

        lax.fori_loop(0, GROUP + 1, clear, 0)
        for t in (tile_a, tile_b):
            t[...] = jnp.zeros(t.shape, I32)
        for y in (ys_a, ys_b):
            y[...] = jnp.zeros(y.shape, F32)

    live = j < nu_ref[s] + 2
    rows = tuple(jnp.clip(j - d, 0, MOE_NB - 1) % MOE_TG for d in range(3))
    common = (rows, tg_ref, ts_ref, x_v, acc, rw_ref, wg_ref, wu_ref, wd_ref)

    @pl.when(live & (j % 2 == 0))
    def _():
        _moe_stage(*common, tile_a, tile_b, ys_b, ys_a)

    @pl.when(live & (j % 2 == 1))
    def _():
        _moe_stage(*common, tile_b, tile_a, ys_a, ys_b)

    @pl.when(j == MOE_NB + 1)
    def _():
        cp = pltpu.make_async_copy(acc.at[pl.ds(0, GROUP * 8)], out_hbm.at[s], sem.at[1])
        cp.start()
        cp.wait()


def _moe_routed(hp, tables, layer, w_gate, w_up, w_down):
    block_e, n_used, row_tok, row_w = tables
    x3 = hp.reshape(N_GROUPS, GROUP * 4, LANES)
    blk = lambda s, j, d: s * MOE_NB + jnp.clip(j - d, 0, MOE_NB - 1)
    wspec = lambda shape: pl.BlockSpec((None,) + shape, lambda s, j, be, nu: (be[blk(s, j, 1)], 0, 0))
    tspec = lambda d: pl.BlockSpec((None, MOE_TG, MOE_M), lambda s, j, be, nu: (blk(s, j, d) // MOE_TG, 0, 0),
                                   memory_space=pltpu.SMEM)
    tok3 = row_tok.reshape(N_GROUPS * MOE_NB // MOE_TG, MOE_TG, MOE_M)
    out = pl.pallas_call(
        _moe_kernel,
        grid_spec=pltpu.PrefetchScalarGridSpec(
            num_scalar_prefetch=2,
            grid=(N_GROUPS, MOE_NB + 2),
            in_specs=[tspec(0), tspec(2),
                      pl.BlockSpec(memory_space=pl.ANY),
                      pl.BlockSpec((None, MOE_TG, 1, MOE_M), lambda s, j, be, nu: (blk(s, j, 1) // MOE_TG, 0, 0, 0)),
                      wspec((D, D_EXP)), wspec((D, D_EXP)), wspec((D_EXP, D))],
            out_specs=pl.BlockSpec(memory_space=pl.ANY),
            scratch_shapes=[pltpu.VMEM((GROUP * 4 + 8, LANES), I32),
                            pltpu.VMEM(((GROUP + 1) * 8, LANES), F32),
                            pltpu.VMEM((4 * MOE_S, LANES), I32),
                            pltpu.VMEM((4 * MOE_S, LANES), I32),
                            pltpu.VMEM((8 * MOE_S, LANES), F32),
                            pltpu.VMEM((8 * MOE_S, LANES), F32),
                            pltpu.SemaphoreType.DMA((2,))]),
        out_shape=jax.ShapeDtypeStruct((N_GROUPS, GROUP * 8, LANES), F32),
        compiler_params=_cparams(("arbitrary", "arbitrary")),
        name="moe_routed",
    )(block_e, n_used, tok3 * 4, tok3 * 8, x3, row_w,
      _cast_experts(w_gate, layer), _cast_experts(w_up, layer), _cast_experts(w_down, layer))
    return out.reshape(T_ALL * 8, LANES)


def _moe_post_kernel(x_ref, hp_ref, r_ref, mod_ref, sg_ref, su_ref, sd_ref, lng_ref, lnb_ref, o_ref,
                     sgb, sub, sdb, *, tm):
    @pl.when(pl.program_id(0) == 0)
    def _():
        sgb[...] = sg_ref[...].astype(BF16)
        sub[...] = su_ref[...].astype(BF16)
        sdb[...] = sd_ref[...].astype(BF16)

    hb = _unpack_rows(hp_ref, tm, 4)
    g = _dot(hb, sgb[...])
    u = _dot(hb, sub[...])
    sh = _dot((_silu(g) * u).astype(BF16), sdb[...])
    routed = jnp.concatenate([r_ref[pl.ds(c, tm, stride=8), :] for c in range(D // LANES)], axis=1)
    r = ALPHA * x_ref[...] + mod_ref[5:6, :] * (routed + sh)
    o_ref[...] = _layer_norm_rows(r, lng_ref[...], lnb_ref[...])


def _moe_post(x, hp, routed, mod4, layer, sg, su, sd, ln_g, ln_b, tm=512):
    T = x.shape[0]
    return pl.pallas_call(
        functools.partial(_moe_post_kernel, tm=tm),
        grid=(T // tm,),
        in_specs=[pl.BlockSpec((tm, D), lambda i: (i, 0)),
                  pl.BlockSpec((tm * 4, LANES), lambda i: (i, 0)),
                  pl.BlockSpec((tm * 8, LANES), lambda i: (i, 0)),
                  pl.BlockSpec((None, None, 6, D), lambda i: (layer, (i * tm) // GROUP, 0, 0)),
                  pl.BlockSpec((D, D_SH), lambda i: (0, 0)),
                  pl.BlockSpec((D, D_SH), lambda i: (0, 0)),
                  pl.BlockSpec((D_SH, D), lambda i: (0, 0)),
                  pl.BlockSpec((1, D), lambda i: (0, 0)),
                  pl.BlockSpec((1, D), lambda i: (0, 0))],
        out_specs=pl.BlockSpec((tm, D), lambda i: (i, 0)),
        out_shape=jax.ShapeDtypeStruct((T, D), F32),
        scratch_shapes=[pltpu.VMEM((D, D_SH), BF16), pltpu.VMEM((D, D_SH), BF16), pltpu.VMEM((D_SH, D), BF16)],
        compiler_params=_cparams(("arbitrary",)),
        name="shared_expert_ln",
    )(x, hp, routed, mod4, sg, su, sd, ln_g.reshape(1, D), ln_b.reshape(1, D))


def _conv_silu(src_ref, w_ref, dst_ref, T, scale):
    L = ML_L
    nc = T // L
    w0, w1, w2 = w_ref[0:1, :], w_ref[1:2, :], w_ref[2:3, :]
    row = lax.broadcasted_iota(I32, (L, ML_DK), 0)

    def body(c, carry):
        r0 = pl.multiple_of(c * L, L)
        cur = src_ref[pl.ds(r0, L), :].astype(F32)
        p0 = pl.multiple_of(jnp.maximum(r0 - 16, 0), 16)
        n0 = pl.multiple_of(jnp.minimum(r0 + L, T - 16), 16)
        prev_row = src_ref[pl.ds(p0, 16), :].astype(F32)[15:16, :] * jnp.where(c > 0, 1.0, 0.0).astype(F32)
        next_row = src_ref[pl.ds(n0, 16), :].astype(F32)[0:1, :] * jnp.where(c < nc - 1, 1.0, 0.0).astype(F32)
        prev = jnp.where(row == 0, prev_row, pltpu.roll(cur, 1, axis=0))
        nxt = jnp.where(row == L - 1, next_row, pltpu.roll(cur, L - 1, axis=0))
        dst_ref[pl.ds(r0, L), :] = _silu(w0 * prev + w1 * cur + w2 * nxt) * scale
        return carry

    lax.fori_loop(0, nc, body, 0)


def _mlstm_chunk(q, k, v, G, GT, C, n, m, backward):
    L = ML_L
    ri = lax.broadcasted_iota(I32, (L, L), 0)
    ci = lax.broadcasted_iota(I32, (L, L), 1)
    keep = (ci >= ri) if backward else (ci <= ri)
    A = keep.astype(BF16)
    AT = ((ri >= ci) if backward else (ri <= ci)).astype(BF16)
    ic, fc = (2, 3) if backward else (0, 1)
    last = 0 if backward else L - 1
    b_col = _mask_dot(A, _log_sigmoid(G))[:, fc:fc + 1]
    b_row = _dot_mask(_log_sigmoid(GT), AT)[fc:fc + 1, :]
    ig_col = G[:, ic:ic + 1]
    ig_row = GT[ic:ic + 1, :]
    log_d = jnp.where(keep, b_col - b_row + ig_row, -jnp.inf)
    log_inter = b_col + m
    m_t = jnp.maximum(log_inter, jnp.max(log_d, axis=-1, keepdims=True))
    d = jnp.exp(log_d - m_t)
    w_inter = jnp.exp(log_inter - m_t)
    qb, kb, vb = q.astype(BF16), k.astype(BF16), v.astype(BF16)
    s = _dot_nt(qb, kb) * d
    num = _dot(s.astype(BF16), vb) + w_inter * _dot(qb, C.astype(BF16))
    den = jnp.sum(s, axis=-1, keepdims=True) + w_inter * jnp.sum(q * n, axis=-1, keepdims=True)
    h = num / jnp.maximum(jnp.abs(den), jnp.exp(-m_t))
    m_new = m_t[last:last + 1, :]
    w_last = jnp.exp(b_col[last:last + 1, :] - b_col + ig_col - m_new)
    decay = w_inter[last:last + 1, :]
    kw = k * w_last
    C_new = decay * C + _dot_tn(kw.astype(BF16), vb)
    n_new = decay * n + jnp.sum(kw, axis=0, keepdims=True)
    return h, C_new, n_new, m_new


def _mlstm_kernel(q_ref, k_ref, v_ref, og_ref, g_ref, gt_ref, cq_ref, ck_ref, ng_ref, c0_ref, n0_ref, m0_ref,
                  a_ref, c_out, n_out, m_out, qs, ks, hf, hb, cst, nst, mst, *, T):
    L = ML_L
    nc = T // L
    _conv_silu(q_ref, cq_ref, qs, T, ML_DK ** -0.5)
    _conv_silu(k_ref, ck_ref, ks, T, 1.0)
    cst[...] = c0_ref[...]
    nst[...] = n0_ref[...]
    mst[...] = m0_ref[...]
    un = min(SCAN_UNROLL, nc)

    def body(i, carry):
        for direction, out in ((0, hf), (1, hb)):
            C, n, m = cst[direction], nst[direction], mst[direction]
            for u in range(un):
                ci = i * un + u
                c = (nc - 1 - ci) if direction else ci
                r0 = pl.multiple_of(c * L, L)
                h, C, n, m = _mlstm_chunk(
                    qs[pl.ds(r0, L), :], ks[pl.ds(r0, L), :], v_ref[pl.ds(r0, L), :],
                    g_ref[pl.ds(r0, L), :], gt_ref[c], C, n, m, backward=bool(direction))
                out[pl.ds(r0, L), :] = h
            cst[direction] = C
            nst[direction] = n
            mst[direction] = m
        return carry

    lax.fori_loop(0, nc // un, body, 0)
    c_out[...] = cst[...]
    n_out[...] = nst[...]
    m_out[...] = mst[...]

    def finish(c, carry):
        r0 = pl.multiple_of(c * L, L)
        tot = hf[pl.ds(r0, L), :] + hb[pl.ds(r0, L), :]
        mu = jnp.mean(tot, axis=-1, keepdims=True)
        cen = tot - mu
        var = jnp.mean(cen * cen, axis=-1, keepdims=True)
        hn = cen * lax.rsqrt(var + EPS) * ng_ref[...]
        a_ref[pl.ds(r0, L), :] = (hn * _sigmoid(og_ref[pl.ds(r0, L), :].astype(F32))).astype(BF16)
        return carry

    lax.fori_loop(0, nc, finish, 0)


def _mlstm_scan(z, gh, ght, conv_w, norm_g, C0, n0, m0, T, row_blk0, nseq):
    nc = T // ML_L
    qcol, kcol = 0, ML_QK // ML_DK
    vcol, ocol = 2 * ML_QK // ML_DV, (2 * ML_QK + ML_V) // ML_DV
    rb = lambda s: row_blk0 + s
    return pl.pallas_call(
        functools.partial(_mlstm_kernel, T=T),
        grid=(nseq, ML_H),
        in_specs=[pl.BlockSpec((T, ML_DK), lambda s, h: (rb(s), qcol + h)),
                  pl.BlockSpec((T, ML_DK), lambda s, h: (rb(s), kcol + h)),
                  pl.BlockSpec((T, ML_DV), lambda s, h: (rb(s), vcol + h)),
                  pl.BlockSpec((T, ML_DV), lambda s, h: (rb(s), ocol + h)),
                  pl.BlockSpec((None, T, 4), lambda s, h: (h, rb(s), 0)),
                  pl.BlockSpec((None, nc, 4, ML_L), lambda s, h: (h, rb(s), 0, 0)),
                  pl.BlockSpec((3, ML_DK), lambda s, h: (0, qcol + h)),
                  pl.BlockSpec((3, ML_DK), lambda s, h: (0, kcol + h)),
                  pl.BlockSpec((1, ML_DV), lambda s, h: (0, h)),
                  pl.BlockSpec((None, 2, None, ML_DK, ML_DV), lambda s, h: (s, 0, h, 0, 0)),
                  pl.BlockSpec((None, 2, None, 1, ML_DK), lambda s, h: (s, 0, h, 0, 0)),
                  pl.BlockSpec((None, 2, None, 1, 1), lambda s, h: (s, 0, h, 0, 0))],
        out_specs=[pl.BlockSpec((T, ML_DV), lambda s, h: (s, h)),
                   pl.BlockSpec((None, 2, None, ML_DK, ML_DV), lambda s, h: (s, 0, h, 0, 0)),
                   pl.BlockSpec((None, 2, None, 1, ML_DK), lambda s, h: (s, 0, h, 0, 0)),
                   pl.BlockSpec((None, 2, None, 1, 1), lambda s, h: (s, 0, h, 0, 0))],
        out_shape=[jax.ShapeDtypeStruct((nseq * T, ML_V), BF16),
                   jax.ShapeDtypeStruct((nseq, 2, ML_H, ML_DK, ML_DV), F32),
                   jax.ShapeDtypeStruct((nseq, 2, ML_H, 1, ML_DK), F32),
                   jax.ShapeDtypeStruct((nseq, 2, ML_H, 1, 1), F32)],
        scratch_shapes=[pltpu.VMEM((T, ML_DK), F32), pltpu.VMEM((T, ML_DK), F32),
                        pltpu.VMEM((T, ML_DV), F32), pltpu.VMEM((T, ML_DV), F32),
                        pltpu.VMEM((2, ML_DK, ML_DV), F32), pltpu.VMEM((2, 1, ML_DK), F32),
                        pltpu.VMEM((2, 1, 1), F32)],
        compiler_params=_cparams(("arbitrary", "arbitrary")),
        name="mlstm_scan",
    )(z, z, z, z, gh, ght, conv_w, conv_w, norm_g.reshape(1, ML_V), C0, n0, m0)


def _mlstm_layer(x, mod4, layer, j, w_in, gate_b, conv_w, norm_g, C_lat, n_lat, m_lat):
    z = _proj(x, mod4, layer, w_in, j, 2 * ML_QK + 2 * ML_V, BF16)
    gates = _proj_small(x, mod4, layer, w_in[j][:, 2 * ML_QK + 2 * ML_V:], gate_b[j])
    gh = gates.reshape(T_ALL, 4, ML_H).transpose(2, 0, 1)
    ght = gh.reshape(ML_H, T_ALL // ML_L, ML_L, 4).transpose(0, 1, 3, 2)
    zC = jnp.zeros((BATCH, 2, ML_H, ML_DK, ML_DV), F32)
    zn = jnp.zeros((BATCH, 2, ML_H, 1, ML_DK), F32)
    zm = jnp.zeros((BATCH, 2, ML_H, 1, 1), F32)
    a_p, Cn, nn, mn = _mlstm_scan(z, gh, ght, conv_w[j], norm_g[j], zC, zn, zm, SEQ, 0, BATCH)
    a_s, _, _, _ = _mlstm_scan(z, gh, ght, conv_w[j], norm_g[j], C_lat[:, j],
                               n_lat[:, j].reshape(DEC_BATCH, 2, ML_H, 1, ML_DK),
                               m_lat[:, j].reshape(DEC_BATCH, 2, ML_H, 1, 1),
                               DEC_SEQ, BATCH * SEQ // DEC_SEQ, DEC_BATCH)
    a = jnp.concatenate([a_p, a_s], axis=0)
    return a, (Cn, nn.reshape(BATCH, 2, ML_H, ML_DK), mn.reshape(BATCH, 2, ML_H))


def _hgrn_chunk(q, k, v, g, St, backward):
    L = HG_L
    ri = lax.broadcasted_iota(I32, (L, L), 0)
    ci = lax.broadcasted_iota(I32, (L, L), 1)
    keep = (ci >= ri) if backward else (ci <= ri)
    ref = L - 1 - L // 2 if backward else L // 2
    last = 0 if backward else L - 1
    b = _mask_dot(keep.astype(BF16), g)
    b_ref = b[ref:ref + 1, :]
    b_last = b[last:last + 1, :]
    qe = (q * jnp.exp(b - b_ref)).astype(BF16)
    ke = (k * jnp.exp(b_ref - b)).astype(BF16)
    vb = v.astype(BF16)
    a = jnp.where(keep, _dot_nt(qe, ke), 0.0)
    o = _dot(a.astype(BF16), vb) + _dot_nt((q * jnp.exp(b)).astype(BF16), St.astype(BF16))
    kd = (k * jnp.exp(b_last - b)).astype(BF16)
    St_new = jnp.exp(b_last) * St + _dot_tn(vb, kd)
    return o, St_new


def _hgrn_kernel(q_ref, i_ref, ff_ref, fb_ref, og_ref, fbf_ref, fbb_ref, lbr_ref, ng_ref, s0_ref,
                 a_ref, s_out, of, ob, st, *, T, lb_layer):
    L = HG_L
    nc = T // L
    raw = lbr_ref[...]
    e = jnp.exp(raw - jnp.max(raw, axis=0, keepdims=True))
    p = e / jnp.sum(e, axis=0, keepdims=True)
    lb = jnp.sum(p[0:lb_layer + 1, :], axis=0, keepdims=True) - p[0:1, :]
    st[0] = s0_ref[0].T
    st[1] = s0_ref[1].T
    un = min(SCAN_UNROLL, nc)

    def body(i, carry):
        for direction, out, f_ref, b_ref in ((0, of, ff_ref, fbf_ref), (1, ob, fb_ref, fbb_ref)):
            St = st[direction]
            for u in range(un):
                ci = i * un + u
                c = (nc - 1 - ci) if direction else ci
                r0 = pl.multiple_of(c * L, L)
                f = lb + (1.0 - lb) * _sigmoid(f_ref[pl.ds(r0, L), :] + b_ref[...])
                o, St = _hgrn_chunk(_silu(q_ref[pl.ds(r0, L), :]), 1.0 - f, i_ref[pl.ds(r0, L), :],
                                    jnp.log(f), St, backward=bool(direction))
                out[pl.ds(r0, L), :] = o
            st[direction] = St
        return carry

    lax.fori_loop(0, nc // un, body, 0)
    s_out[0] = st[0].T
    s_out[1] = st[1].T

    def finish(c, carry):
        r0 = pl.multiple_of(c * L, L)
        tot = of[pl.ds(r0, L), :] + ob[pl.ds(r0, L), :]
        on = tot * lax.rsqrt(jnp.mean(tot * tot, axis=-1, keepdims=True) + EPS) * ng_ref[...]
        a_ref[pl.ds(r0, L), :] = (on * _silu(og_ref[pl.ds(r0, L), :])).astype(BF16)
        return carry

    lax.fori_loop(0, nc, finish, 0)


def _hgrn_scan(z, f_b, lb_raw, norm_g, S0, T, row_blk0, nseq, lb_layer):
    nh = HG_H
    rb = lambda s: row_blk0 + s
    zspec = lambda cb: pl.BlockSpec((T, HG_DK), lambda s, h: (rb(s), cb * nh + h))
    return pl.pallas_call(
        functools.partial(_hgrn_kernel, T=T, lb_layer=lb_layer),
        grid=(nseq, nh),
        in_specs=[zspec(0), zspec(1), zspec(2), zspec(3), zspec(4),
                  pl.BlockSpec((1, HG_DK), lambda s, h: (0, h)),
                  pl.BlockSpec((1, HG_DK), lambda s, h: (0, nh + h)),
                  pl.BlockSpec((DEPTH, HG_DK), lambda s, h: (0, h)),
                  pl.BlockSpec((1, HG_DV), lambda s, h: (0, h)),
                  pl.BlockSpec((None, 2, None, HG_DK, HG_DV), lambda s, h: (s, 0, h, 0, 0))],
        out_specs=[pl.BlockSpec((T, HG_DV), lambda s, h: (s, h)),
                   pl.BlockSpec((None, 2, None, HG_DK, HG_DV), lambda s, h: (s, 0, h, 0, 0))],
        out_shape=[jax.ShapeDtypeStruct((nseq * T, HG_V), BF16),
                   jax.ShapeDtypeStruct((nseq, 2, nh, HG_DK, HG_DV), F32)],
        scratch_shapes=[pltpu.VMEM((T, HG_DV), F32), pltpu.VMEM((T, HG_DV), F32),
                        pltpu.VMEM((2, HG_DV, HG_DK), F32)],
        compiler_params=_cparams(("arbitrary", "arbitrary")),
        name="hgrn_scan",
    )(z, z, z, z, z, f_b.reshape(1, 2 * HG_K), f_b.reshape(1, 2 * HG_K), lb_raw, norm_g.reshape(1, HG_V), S0)


def _hgrn_layer(x, mod4, layer, j, w_in, f_b, lb_raw, norm_g, S_lat):
    z = _proj(x, mod4, layer, w_in, j, 3 * HG_K + 2 * HG_V, F32)
    zS = jnp.zeros((BATCH, 2, HG_H, HG_DK, HG_DV), F32)
    a_p, Sn = _hgrn_scan(z, f_b[j], lb_raw, norm_g[j], zS, SEQ, 0, BATCH, layer)
    a_s, _ = _hgrn_scan(z, f_b[j], lb_raw, norm_g[j], S_lat[:, j], DEC_SEQ, BATCH * SEQ // DEC_SEQ, DEC_BATCH, layer)
    return jnp.concatenate([a_p, a_s], axis=0), Sn


def _head_rms(x, g_tiled, nheads):
    lane = lax.broadcasted_iota(I32, x.shape, 1)
    sq = x * x
    ms = jnp.zeros_like(x)
    for h in range(nheads):
        in_h = (lane >= h * AT_HD) & (lane < (h + 1) * AT_HD)
        tot = jnp.sum(jnp.where(in_h, sq, 0.0), axis=-1, keepdims=True)
        ms = jnp.where(in_h, tot, ms)
    return x * lax.rsqrt(ms * (1.0 / AT_HD) + EPS) * g_tiled


def _rope(x, cos, sin):
    w = x.shape[1]
    lane = lax.broadcasted_iota(I32, x.shape, 1)
    up = pltpu.roll(x, w - 16, axis=1)
    down = pltpu.roll(x, 16, axis=1)
    swapped = jnp.where((lane % 32) < 16, up, down)
    return x * cos + swapped * sin


def _attn_ctx_kernel(q_ref, k_ref, v_ref, qn_ref, kn_ref, o_ref, ko_ref, vo_ref):
    k = _head_rms(k_ref[...].astype(F32), kn_ref[...], AT_KV)
    ko_ref[...] = k
    v = v_ref[...].astype(F32)
    vo_ref[...] = v
    kb, vb = k.astype(BF16), v.astype(BF16)
    outs = []
    for kv in range(AT_KV):
        q = _head_rms(q_ref[:, kv * 256:(kv + 1) * 256].astype(F32), qn_ref[...], AT_G) * (AT_HD ** -0.5)
        kh = kb[:, kv * AT_HD:(kv + 1) * AT_HD]
        vh = vb[:, kv * AT_HD:(kv + 1) * AT_HD]
        for g in range(AT_G):
            s = _dot_nt(q[:, g * AT_HD:(g + 1) * AT_HD].astype(BF16), kh)
            p = jnp.exp(s - jnp.max(s, axis=-1, keepdims=True))
            p = p / jnp.sum(p, axis=-1, keepdims=True)
            outs.append(_dot(p.astype(BF16), vh))
    o_ref[...] = jnp.concatenate(outs, axis=1).astype(BF16)


def _attn_ctx(z, q_norm, k_norm):
    qn = jnp.tile(q_norm, AT_G).reshape(1, 256)
    kn = jnp.tile(k_norm, AT_KV).reshape(1, 256)
    return pl.pallas_call(
        _attn_ctx_kernel,
        grid=(BATCH,),
        in_specs=[pl.BlockSpec((SEQ, D), lambda b: (b, 0)),
                  pl.BlockSpec((SEQ, 256), lambda b: (b, 4)),
                  pl.BlockSpec((SEQ, 256), lambda b: (b, 5)),
                  pl.BlockSpec((1, 256), lambda b: (0, 0)),
                  pl.BlockSpec((1, 256), lambda b: (0, 0))],
        out_specs=[pl.BlockSpec((SEQ, D), lambda b: (b, 0)),
                   pl.BlockSpec((SEQ, 256), lambda b: (b, 0)),
                   pl.BlockSpec((SEQ, 256), lambda b: (b, 0))],
        out_shape=[jax.ShapeDtypeStruct((BATCH * SEQ, D), BF16),
                   jax.ShapeDtypeStruct((BATCH * SEQ, 256), F32),
                   jax.ShapeDtypeStruct((BATCH * SEQ, 256), F32)],
        compiler_params=_cparams(("arbitrary",)),
        name="attn_context",
    )(z, z, z, qn, kn)


def _attn_kv_kernel(k_ref, v_ref, kn_ref, cos_ref, sin_ref, ko_ref, vo_ref):
    k = _rope(_head_rms(k_ref[...].astype(F32), kn_ref[...], AT_KV), cos_ref[...], sin_ref[...])
    v = v_ref[...]
    for h in range(AT_KV):
        ko_ref[h] = k[:, h * AT_HD:(h + 1) * AT_HD].astype(BF16)
        vo_ref[h] = v[:, h * AT_HD:(h + 1) * AT_HD].astype(BF16)


def _attn_kv(z, k_norm, cos4, sin4, tt=512):
    kn = jnp.tile(k_norm, AT_KV).reshape(1, 256)
    nt = DEC_SEQ // tt
    row0 = BATCH * SEQ // tt
    return pl.pallas_call(
        _attn_kv_kernel,
        grid=(DEC_BATCH, nt),
        in_specs=[pl.BlockSpec((tt, 256), lambda b, i: (row0 + b * nt + i, 4)),
                  pl.BlockSpec((tt, 256), lambda b, i: (row0 + b * nt + i, 5)),
                  pl.BlockSpec((1, 256), lambda b, i: (0, 0)),
                  pl.BlockSpec((tt, 256), lambda b, i: (i, 0)),
                  pl.BlockSpec((tt, 256), lambda b, i: (i, 0))],
        out_specs=[pl.BlockSpec((None, AT_KV, tt, AT_HD), lambda b, i: (b, 0, i, 0)),
                   pl.BlockSpec((None, AT_KV, tt, AT_HD), lambda b, i: (b, 0, i, 0))],
        out_shape=[jax.ShapeDtypeStruct((DEC_BATCH, AT_KV, DEC_SEQ, AT_HD), BF16),
                   jax.ShapeDtypeStruct((DEC_BATCH, AT_KV, DEC_SEQ, AT_HD), BF16)],
        compiler_params=_cparams(("arbitrary", "arbitrary")),
        name="attn_kv_prep",
    )(z, z, kn, cos4, sin4)


def _attn_lat_kernel(q_ref, qn_ref, cos_ref, sin_ref, k_ref, v_ref, o_ref, *, tq, tk):
    q = _rope(_head_rms(q_ref[...].astype(F32), qn_ref[...], AT_G), cos_ref[...], sin_ref[...]) * (AT_HD ** -0.5)
    qs = jnp.concatenate([q[:, g * AT_HD:(g + 1) * AT_HD] for g in range(AT_G)], axis=0).astype(BF16)
    rows = AT_G * tq
    nk = k_ref.shape[0] // tk

    def body(j, carry):
        m, l, acc = carry
        k0 = pl.multiple_of(j * tk, tk)
        s = _dot_nt(qs, k_ref[pl.ds(k0, tk), :])
        m_new = jnp.maximum(m, jnp.max(s, axis=-1, keepdims=True))
        alpha = jnp.exp(m - m_new)
        p = jnp.exp(s - m_new)
        l = alpha * l + jnp.sum(p, axis=-1, keepdims=True)
        acc = alpha * acc + _dot(p.astype(BF16), v_ref[pl.ds(k0, tk), :])
        return m_new, l, acc

    init = (jnp.full((rows, 1), -jnp.inf, F32), jnp.zeros((rows, 1), F32), jnp.zeros((rows, AT_HD), F32))
    _, l, acc = lax.fori_loop(0, nk, body, init)
    o = acc / l
    o_ref[...] = jnp.concatenate([o[g * tq:(g + 1) * tq, :] for g in range(AT_G)], axis=1).astype(BF16)


def _attn_lat(z, q_norm, cos4, sin4, kk, vv, tq=256, tk=1536):
    qn = jnp.tile(q_norm, AT_G).reshape(1, 256)
    nq = DEC_SEQ // tq
    row0 = BATCH * SEQ // tq
    skv = kk.shape[2]
    return pl.pallas_call(
        functools.partial(_attn_lat_kernel, tq=tq, tk=tk),
        grid=(DEC_BATCH, AT_KV, nq),
        in_specs=[pl.BlockSpec((tq, 256), lambda b, h, i: (row0 + b * nq + i, h)),
                  pl.BlockSpec((1, 256), lambda b, h, i: (0, 0)),
                  pl.BlockSpec((tq, 256), lambda b, h, i: (i, 0)),
                  pl.BlockSpec((tq, 256), lambda b, h, i: (i, 0)),
                  pl.BlockSpec((None, None, skv, AT_HD), lambda b, h, i: (b, h, 0, 0)),
                  pl.BlockSpec((None, None, skv, AT_HD), lambda b, h, i: (b, h, 0, 0))],
        out_specs=pl.BlockSpec((tq, 256), lambda b, h, i: (b * nq + i, h)),
        out_shape=jax.ShapeDtypeStruct((DEC_BATCH * DEC_SEQ, D), BF16),
        compiler_params=_cparams(("arbitrary", "arbitrary", "arbitrary")),
        name="attn_latent",
    )(z, qn, cos4, sin4, kk, vv)


def _rope_tables():
    t = jnp.arange(DEC_SEQ)
    row = (t // GRID_W).astype(F32)
    col = (t % GRID_W).astype(F32)
    nf = AT_HD // 4
    inv = ROPE_THETA ** (-jnp.arange(nf, dtype=F32) / nf)
    ar, ac = row[:, None] * inv[None], col[:, None] * inv[None]
    cos = jnp.concatenate([jnp.cos(ar), jnp.cos(ar), jnp.cos(ac), jnp.cos(ac)], axis=1)
    sin = jnp.concatenate([-jnp.sin(ar), jnp.sin(ar), -jnp.sin(ac), jnp.sin(ac)], axis=1)
    return jnp.tile(cos, (1, 4)), jnp.tile(sin, (1, 4))


def _attn_layer(x, mod4, layer, j, w_in, q_norm, k_norm, cache_k, cache_v):
    z = _proj(x, mod4, layer, w_in, j, (AT_H + 2 * AT_KV) * AT_HD, BF16)
    a_p, k_new, v_new = _attn_ctx(z, q_norm[j], k_norm[j])
    cos4, sin4 = _rope_tables()
    k_lat, v_lat = _attn_kv(z, k_norm[j], cos4, sin4)
    kk = jnp.concatenate([cache_k[:, j].transpose(0, 2, 1, 3).astype(BF16), k_lat], axis=2)
    vv = jnp.concatenate([cache_v[:, j].transpose(0, 2, 1, 3).astype(BF16), v_lat], axis=2)
    a_s = _attn_lat(z, q_norm[j], cos4, sin4, kk, vv)
    a = jnp.concatenate([a_p, a_s], axis=0)
    return a, (k_new.reshape(BATCH, SEQ, AT_KV, AT_HD), v_new.reshape(BATCH, SEQ, AT_KV, AT_HD))


def kernel(x_prompt, x_sample, state_mlstm_C, state_mlstm_n, state_mlstm_m, cache_attn_k, cache_attn_v, state_hgrn_S, c, c_ctx, mod_w, mod_b, ln_g, ln_b, mlstm_w_in, mlstm_gate_b, mlstm_conv, mlstm_norm, mlstm_w_out, attn_w_in, attn_q_norm, attn_k_norm, attn_w_out, hgrn_w_in, hgrn_f_b, hgrn_lower_bounds, hgrn_norm, hgrn_w_out, moe_router, moe_router_b, moe_w_gate, moe_w_up, moe_w_down, moe_sh_gate, moe_sh_up, moe_sh_down):
    x = jnp.concatenate([x_prompt.reshape(BATCH * SEQ, D), x_sample.reshape(DEC_BATCH * DEC_SEQ, D)], axis=0)
    cond8 = jnp.concatenate([c_ctx[None], c, jnp.zeros((8 - 1 - DEC_BATCH, D), F32)], axis=0)
    mod4 = _mod_all(cond8, mod_w, mod_b)

    new_C, new_n, new_m, new_k, new_v, new_S = [], [], [], [], [], []
    for l in range(DEPTH):
        kind, j = l % 3, l // 3
        if kind == 0:
            a, (Cn, nn, mn) = _mlstm_layer(x, mod4, l, j, mlstm_w_in, mlstm_gate_b, mlstm_conv, mlstm_norm,
                                           state_mlstm_C, state_mlstm_n, state_mlstm_m)
            new_C.append(Cn)
            new_n.append(nn)
            new_m.append(mn)
            w_out = mlstm_w_out[j]
        elif kind == 1:
            a, (kn, vn) = _attn_layer(x, mod4, l, j, attn_w_in, attn_q_norm, attn_k_norm, cache_attn_k, cache_attn_v)
            new_k.append(kn)
            new_v.append(vn)
            w_out = attn_w_out[j]
        else:
            a, Sn = _hgrn_layer(x, mod4, l, j, hgrn_w_in, hgrn_f_b, hgrn_lower_bounds, hgrn_norm, state_hgrn_S)
            new_S.append(Sn)
            w_out = hgrn_w_out[j]
        x, hp, idx128, w128 = _post(a, x, mod4, l, w_out, ln_g[l, 0], ln_b[l, 0], moe_router[l], moe_router_b[l])
        routed = _moe_routed(hp, _route_tables(idx128, w128), l, moe_w_gate, moe_w_up, moe_w_down)
        x = _moe_post(x, hp, routed, mod4, l, moe_sh_gate[l], moe_sh_up[l], moe_sh_down[l], ln_g[l, 1], ln_b[l, 1])

    xp = x[:BATCH * SEQ].reshape(BATCH, SEQ, D)
    xs = x[BATCH * SEQ:].reshape(DEC_BATCH, DEC_SEQ, D)
    return (xp, xs, jnp.stack(new_C, 1), jnp.stack(new_n, 1), jnp.stack(new_m, 1),
            jnp.stack(new_k, 1), jnp.stack(new_v, 1), jnp.stack(new_S, 1))
```

```python
import functools

import jax
import jax.numpy as jnp
from jax import lax
from jax.experimental import pallas as pl
from jax.experimental.pallas import tpu as pltpu

F32 = jnp.float32
BF16 = jnp.bfloat16
I32 = jnp.int32
HI = lax.Precision.HIGHEST

D = 1024
BATCH, SEQ = 16, 256
DEPTH = 4
DEC_BATCH, DEC_SEQ = 2, 4096
PAST = 512
GRID_W = 64
GROUP = 4096
N_GROUPS = 3
T_ALL = N_GROUPS * GROUP

ML_H, ML_DK, ML_DV, ML_L = 4, 128, 256, 128
ML_QK, ML_V = ML_H * ML_DK, ML_H * ML_DV
AT_H, AT_KV, AT_HD, AT_G = 16, 4, 64, 4
HG_H, HG_DK, HG_DV, HG_L = 8, 128, 128, 64
HG_K, HG_V = HG_H * HG_DK, HG_H * HG_DV
N_EXP, TOP_K, D_EXP, D_SH = 64, 8, 256, 256
ROUTE_SCALE = 2.5
ALPHA = (2 * DEPTH) ** 0.25
EPS = 1e-6
ROPE_THETA = 10000.0

VMEM_LIMIT = 56 * 1024 * 1024
LANES = 128

MOE_M = 256
MOE_NB = GROUP * TOP_K // MOE_M + N_EXP
MOE_ROWS = MOE_NB * MOE_M
MOE_SORT_LEN = 65536
MOE_S = MOE_M + 8
MOE_U = 8
MOE_TG = 1
SCAN_UNROLL = 4


def _cparams(sem):
    return pltpu.CompilerParams(dimension_semantics=sem, vmem_limit_bytes=VMEM_LIMIT)


def _sigmoid(x):
    return 1.0 / (1.0 + jnp.exp(-x))


def _silu(x):
    return x * _sigmoid(x)


def _log_sigmoid(x):
    return jnp.minimum(x, 0.0) - jnp.log(1.0 + jnp.exp(-jnp.abs(x)))


def _dot(a, b):
    return jnp.dot(a, b, preferred_element_type=F32)


def _dot_nt(a, b):
    return lax.dot_general(a, b, (((1,), (1,)), ((), ())), preferred_element_type=F32)


def _dot_tn(a, b):
    return lax.dot_general(a, b, (((0,), (0,)), ((), ())), preferred_element_type=F32)


def _dot_hi(a, b):
    return jnp.dot(a, b, preferred_element_type=F32, precision=HI)


def _split3(x):
    p0 = x.astype(BF16)
    r1 = x - p0.astype(F32)
    p1 = r1.astype(BF16)
    p2 = (r1 - p1.astype(F32)).astype(BF16)
    return p0, p1, p2


def _mask_dot(mask_bf16, x):
    p0, p1, p2 = _split3(x)
    return _dot(mask_bf16, p0) + _dot(mask_bf16, p1) + _dot(mask_bf16, p2)


def _dot_mask(x, mask_bf16):
    p0, p1, p2 = _split3(x)
    return _dot(p0, mask_bf16) + _dot(p1, mask_bf16) + _dot(p2, mask_bf16)


def _dot_3x(a, b):
    a0 = a.astype(BF16)
    a1 = (a - a0.astype(F32)).astype(BF16)
    b0 = b.astype(BF16)
    b1 = (b - b0.astype(F32)).astype(BF16)
    return _dot(a0, b0) + _dot(a0, b1) + _dot(a1, b0)


def _mod_kernel(cond_ref, w_ref, b_ref, o_ref):
    o_ref[...] = _dot_hi(_silu(cond_ref[...]), w_ref[...]) + b_ref[...]


def _mod_all(cond8, mod_w, mod_b):
    tn = 1024
    out = pl.pallas_call(
        _mod_kernel,
        grid=(DEPTH, 6 * D // tn),
        in_specs=[pl.BlockSpec((8, D), lambda l, j: (0, 0)),
                  pl.BlockSpec((None, D, tn), lambda l, j: (l, 0, j)),
                  pl.BlockSpec((None, 1, tn), lambda l, j: (l, 0, j))],
        out_specs=pl.BlockSpec((None, 8, tn), lambda l, j: (l, 0, j)),
        out_shape=jax.ShapeDtypeStruct((DEPTH, 8, 6 * D), F32),
        compiler_params=_cparams(("arbitrary", "arbitrary")),
        name="mod_rows",
    )(cond8, mod_w, mod_b.reshape(DEPTH, 1, 6 * D))
    return out.reshape(DEPTH, 8, 6, D)


def _proj_kernel(x_ref, mod_ref, w_ref, o_ref, h_scr):
    @pl.when(pl.program_id(1) == 0)
    def _():
        h_scr[...] = (x_ref[...] * (1.0 + mod_ref[1:2, :]) + mod_ref[0:1, :]).astype(BF16)

    o_ref[...] = _dot(h_scr[...], w_ref[...].astype(BF16)).astype(o_ref.dtype)


def _proj(x, mod4, layer, w3, widx, ncols, out_dtype, tm=2048, tn=512):
    T = x.shape[0]
    return pl.pallas_call(
        _proj_kernel,
        grid=(T // tm, ncols // tn),
        in_specs=[pl.BlockSpec((tm, D), lambda i, j: (i, 0)),
                  pl.BlockSpec((None, None, 6, D), lambda i, j: (layer, (i * tm) // GROUP, 0, 0)),
                  pl.BlockSpec((None, D, tn), lambda i, j: (widx, 0, j))],
        out_specs=pl.BlockSpec((tm, tn), lambda i, j: (i, j)),
        out_shape=jax.ShapeDtypeStruct((T, ncols), out_dtype),
        scratch_shapes=[pltpu.VMEM((tm, D), BF16)],
        compiler_params=_cparams(("arbitrary", "arbitrary")),
        name="in_proj",
    )(x, mod4, w3)


def _proj_small_kernel(x_ref, mod_ref, w_ref, b_ref, o_ref):
    h = x_ref[...] * (1.0 + mod_ref[1:2, :]) + mod_ref[0:1, :]
    o_ref[...] = _dot_hi(h, w_ref[...]) + b_ref[...]


def _proj_small(x, mod4, layer, w, b, tm=1024):
    T, n = x.shape[0], w.shape[1]
    return pl.pallas_call(
        _proj_small_kernel,
        grid=(T // tm,),
        in_specs=[pl.BlockSpec((tm, D), lambda i: (i, 0)),
                  pl.BlockSpec((None, None, 6, D), lambda i: (layer, (i * tm) // GROUP, 0, 0)),
                  pl.BlockSpec((D, n), lambda i: (0, 0)),
                  pl.BlockSpec((1, n), lambda i: (0, 0))],
        out_specs=pl.BlockSpec((tm, n), lambda i: (i, 0)),
        out_shape=jax.ShapeDtypeStruct((T, n), F32),
        compiler_params=_cparams(("arbitrary",)),
        name="gate_proj",
    )(x, mod4, w, b.reshape(1, n))


def _layer_norm_rows(r, g, b):
    mu = jnp.mean(r, axis=-1, keepdims=True)
    c = r - mu
    var = jnp.mean(c * c, axis=-1, keepdims=True)
    return c * lax.rsqrt(var + EPS) * g + b


def _pack_bf16_pairs(lo, hi):
    lo_b = lax.bitcast_convert_type(lo.astype(BF16).astype(F32), I32)
    hi_b = lax.bitcast_convert_type(hi.astype(BF16).astype(F32), I32)
    return lax.shift_right_logical(lo_b, 16) | (hi_b & jnp.int32(-65536))


def _unpack_bf16_pairs(v):
    lo = lax.bitcast_convert_type(lax.shift_left(v, 16), F32)
    hi = lax.bitcast_convert_type(v & jnp.int32(-65536), F32)
    return lo.astype(BF16), hi.astype(BF16)


def _unpack_rows(ref, rows, stride, offset=0):
    chunks = []
    for r in range(4):
        if stride == 4:
            v = ref[pl.ds(r, rows, stride=4), :]
        else:
            v = ref[pl.ds(offset + r * stride, rows), :]
        chunks.extend(_unpack_bf16_pairs(v))
    return jnp.concatenate(chunks, axis=1)


def _post_kernel(a_ref, x_ref, mod_ref, w_ref, lng_ref, lnb_ref, rw_ref, rb_ref,
                 xo_ref, hp_ref, idx_ref, wt_ref, wb_scr, *, tm):
    @pl.when(pl.program_id(0) == 0)
    def _():
        wb_scr[...] = w_ref[...].astype(BF16)

    y = _dot(a_ref[...], wb_scr[...])
    r = ALPHA * x_ref[...] + mod_ref[2:3, :] * y
    xn = _layer_norm_rows(r, lng_ref[...], lnb_ref[...])
    xo_ref[...] = xn
    h1 = xn * (1.0 + mod_ref[4:5, :]) + mod_ref[3:4, :]
    for p in range(4):
        lo = h1[:, (2 * p) * LANES:(2 * p + 1) * LANES]
        hi = h1[:, (2 * p + 1) * LANES:(2 * p + 2) * LANES]
        hp_ref[pl.ds(p, tm, stride=4), :] = _pack_bf16_pairs(lo, hi)

    scores = _sigmoid(_dot_3x(h1, rw_ref[...]))
    sel = scores + rb_ref[...]
    e_iota = lax.broadcasted_iota(I32, (tm, N_EXP), 1)
    lane = lax.broadcasted_iota(I32, (tm, LANES), 1)
    idx_out = jnp.zeros((tm, LANES), I32)
    w_out = jnp.zeros((tm, LANES), F32)
    total = jnp.zeros((tm, 1), F32)
    for k in range(TOP_K):
        mx = jnp.max(sel, axis=-1, keepdims=True)
        ik = jnp.min(jnp.where(sel == mx, e_iota, N_EXP), axis=-1, keepdims=True)
        hit = e_iota == ik
        wk = jnp.sum(jnp.where(hit, scores, 0.0), axis=-1, keepdims=True)
        sel = jnp.where(hit, -jnp.inf, sel)
        total = total + wk
        idx_out = jnp.where(lane == k, ik, idx_out)
        w_out = jnp.where(lane == k, wk, w_out)
    idx_ref[...] = idx_out
    wt_ref[...] = w_out / total * ROUTE_SCALE


def _post(a, x, mod4, layer, w_out, ln_g, ln_b, router_w, router_b, tm=512):
    T = x.shape[0]
    return pl.pallas_call(
        functools.partial(_post_kernel, tm=tm),
        grid=(T // tm,),
        in_specs=[pl.BlockSpec((tm, D), lambda i: (i, 0)),
                  pl.BlockSpec((tm, D), lambda i: (i, 0)),
                  pl.BlockSpec((None, None, 6, D), lambda i: (layer, (i * tm) // GROUP, 0, 0)),
                  pl.BlockSpec((D, D), lambda i: (0, 0)),
                  pl.BlockSpec((1, D), lambda i: (0, 0)),
                  pl.BlockSpec((1, D), lambda i: (0, 0)),
                  pl.BlockSpec((D, N_EXP), lambda i: (0, 0)),
                  pl.BlockSpec((1, N_EXP), lambda i: (0, 0))],
        out_specs=[pl.BlockSpec((tm, D), lambda i: (i, 0)),
                   pl.BlockSpec((tm * 4, LANES), lambda i: (i, 0)),
                   pl.BlockSpec((tm, LANES), lambda i: (i, 0)),
                   pl.BlockSpec((tm, LANES), lambda i: (i, 0))],
        out_shape=[jax.ShapeDtypeStruct((T, D), F32),
                   jax.ShapeDtypeStruct((T * 4, LANES), I32),
                   jax.ShapeDtypeStruct((T, LANES), I32),
                   jax.ShapeDtypeStruct((T, LANES), F32)],
        scratch_shapes=[pltpu.VMEM((D, D), BF16)],
        compiler_params=_cparams(("arbitrary",)),
        name="out_proj_ln_router",
    )(a, x, mod4, w_out, ln_g.reshape(1, D), ln_b.reshape(1, D), router_w, router_b.reshape(1, N_EXP))


def _route_tables(idx128, w128):
    idx = idx128[:, :TOP_K].reshape(N_GROUPS, GROUP, TOP_K)
    w = w128[:, :TOP_K].reshape(N_GROUPS, GROUP, TOP_K)
    onehot = idx[..., None] == jnp.arange(N_EXP, dtype=I32)
    counts = jnp.sum(onehot.astype(I32), axis=(1, 2))
    padded = (counts + MOE_M - 1) // MOE_M * MOE_M
    pad_end = jnp.cumsum(padded, axis=-1)
    tok = jnp.arange(GROUP, dtype=I32)[None, :, None]
    real_keys = (idx * (2 * GROUP) + tok).reshape(N_GROUPS, GROUP * TOP_K)
    fill = jnp.arange(MOE_M, dtype=I32)[None, None, :]
    e_ids = jnp.arange(N_EXP, dtype=I32)[None, :, None]
    fill_keys = jnp.where(fill < (padded - counts)[:, :, None], e_ids * (2 * GROUP) + GROUP + fill,
                          N_EXP * 2 * GROUP + e_ids * MOE_M + fill).reshape(N_GROUPS, N_EXP * MOE_M)
    tail = MOE_SORT_LEN - MOE_ROWS
    tail_keys = jnp.broadcast_to(N_EXP * 2 * GROUP + N_EXP * MOE_M + jnp.arange(tail, dtype=I32), (N_GROUPS, tail))
    keys = jnp.concatenate([real_keys, fill_keys, tail_keys], axis=1)
    vals = jnp.concatenate([w.reshape(N_GROUPS, GROUP * TOP_K),
                            jnp.zeros((N_GROUPS, MOE_SORT_LEN - GROUP * TOP_K), F32)], axis=1)
    sorted_rows = [lax.sort((keys[s], vals[s]), dimension=0, num_keys=1) for s in range(N_GROUPS)]
    keys = jnp.stack([k[:MOE_ROWS] for k, _ in sorted_rows])
    row_w = jnp.stack([v[:MOE_ROWS] for _, v in sorted_rows])
    row_tok = jnp.where(keys < N_EXP * 2 * GROUP, jnp.minimum(keys & (2 * GROUP - 1), GROUP), GROUP)
    row_tok = row_tok.reshape(-1)
    n_used = pad_end[:, -1] // MOE_M
    starts = jnp.arange(MOE_NB, dtype=I32) * MOE_M
    block_e = jnp.sum((starts[None, :, None] >= pad_end[:, None, :]).astype(I32), axis=-1)
    block_e = jnp.minimum(block_e, N_EXP - 1)
    last_e = jnp.take_along_axis(block_e, jnp.maximum(n_used - 1, 0)[:, None], axis=1)
    block_e = jnp.where(jnp.arange(MOE_NB, dtype=I32)[None, :] < n_used[:, None], block_e, last_e)
    return (block_e.reshape(-1).astype(I32), n_used.astype(I32), row_tok,
            row_w.reshape(N_GROUPS * MOE_NB // MOE_TG, MOE_TG, 1, MOE_M))


def _cast_kernel(x_ref, o_ref):
    o_ref[...] = x_ref[...].astype(BF16)


def _cast_experts(w4, layer, eb=4):
    _, n_e, a, b = w4.shape
    return pl.pallas_call(
        _cast_kernel,
        grid=(n_e // eb,),
        in_specs=[pl.BlockSpec((None, eb, a, b), lambda i: (layer, i, 0, 0))],
        out_specs=pl.BlockSpec((eb, a, b), lambda i: (i, 0, 0)),
        out_shape=jax.ShapeDtypeStruct((n_e, a, b), BF16),
        compiler_params=_cparams(("arbitrary",)),
        name="cast_experts",
    )(w4)


def _moe_stage(rows, tg_ref, ts_ref, x_v, acc, rw_ref, wg_ref, wu_ref, wd_ref, tile_g, tile_c, ys_c, ys_s):
    rg, rc, rs = rows
    for mi in range(MOE_M):
        tile_g[pl.ds(mi, 4, stride=MOE_S), :] = x_v[pl.ds(pl.multiple_of(tg_ref[rg, mi], 4), 4), :]
    xb = _unpack_rows(tile_c, MOE_M, MOE_S)
    g = _dot(xb, wg_ref[...])
    u = _dot(xb, wu_ref[...])
    ri = lax.broadcasted_iota(I32, (MOE_M, MOE_M), 0)
    ci = lax.broadcasted_iota(I32, (MOE_M, MOE_M), 1)
    rw_col = jnp.sum(jnp.where(ri == ci, jnp.broadcast_to(rw_ref[rc], (MOE_M, MOE_M)), 0.0),
                     axis=1, keepdims=True)
    a = (_silu(g) * u) * rw_col
    y = _dot(a.astype(BF16), wd_ref[...])
    for c in range(D // LANES):
        ys_c[pl.ds(c * MOE_S, MOE_M), :] = y[:, c * LANES:(c + 1) * LANES]
    for m0 in range(0, MOE_M, MOE_U):
        offs = [pl.multiple_of(ts_ref[rs, m0 + j], 8) for j in range(MOE_U)]
        vals = [acc[pl.ds(offs[j], 8), :] + ys_s[pl.ds(m0 + j, 8, stride=MOE_S), :] for j in range(MOE_U)]
        for j in range(MOE_U):
            acc[pl.ds(offs[j], 8), :] = vals[j]


def _moe_kernel(be_ref, nu_ref, tg_ref, ts_ref, x_hbm, rw_ref, wg_ref, wu_ref, wd_ref, out_hbm,
                x_v, acc, tile_a, tile_b, ys_a, ys_b, sem):
    s = pl.program_id(0)
    j = pl.program_id(1)

    @pl.when(j == 0)
    def _():
        cp = pltpu.make_async_copy(x_hbm.at[s], x_v.at[pl.ds(0, GROUP * 4)], sem.at[0])
        cp.start()
        cp.wait()
        x_v[pl.ds(GROUP * 4, 8), :] = jnp.zeros((8, LANES), I32)

        def clear(i, carry):
            acc[pl.ds(pl.multiple_of(i * 8, 8), 8), :] = jnp.zeros((8, LANES), F32)
            return carry

        lax.fori_loop(0, GROUP + 1, clear, 0)
        for t in (tile_a, tile_b):
            t[...] = jnp.zeros(t.shape, I32)
        for y in (ys_a, ys_b):
            y[...] = jnp.zeros(y.shape, F32)

    live = j < nu_ref[s] + 2
    if MOE_TG == 1:
        rows = (0, 0, 0)
    else:
        rows = tuple(jnp.clip(j - d, 0, MOE_NB - 1) % MOE_TG for d in range(3))
    common = (rows, tg_ref, ts_ref, x_v, acc, rw_ref, wg_ref, wu_ref, wd_ref)

    @pl.when(live & (j % 2 == 0))
    def _():
        _moe_stage(*common, tile_a, tile_b, ys_b, ys_a)

    @pl.when(live & (j % 2 == 1))
    def _():
        _moe_stage(*common, tile_b, tile_a, ys_a, ys_b)

    @pl.when(j == MOE_NB + 1)
    def _():
        cp = pltpu.make_async_copy(acc.at[pl.ds(0, GROUP * 8)], out_hbm.at[s], sem.at[1])
        cp.start()
        cp.wait()


def _moe_routed(hp, tables, layer, w_gate, w_up, w_down):
    block_e, n_used, row_tok, row_w = tables
    x3 = hp.reshape(N_GROUPS, GROUP * 4, LANES)
    blk = lambda s, j, d: s * MOE_NB + jnp.clip(j - d, 0, MOE_NB - 1)
    wspec = lambda shape: pl.BlockSpec((None,) + shape, lambda s, j, be, nu: (be[blk(s, j, 1)], 0, 0))
    tspec = lambda d: pl.BlockSpec((None, MOE_TG, MOE_M), lambda s, j, be, nu: (blk(s, j, d) // MOE_TG, 0, 0),
                                   memory_space=pltpu.SMEM)
    tok3 = row_tok.reshape(N_GROUPS * MOE_NB // MOE_TG, MOE_TG, MOE_M)
    out = pl.pallas_call(
        _moe_kernel,
        grid_spec=pltpu.PrefetchScalarGridSpec(
            num_scalar_prefetch=2,
            grid=(N_GROUPS, MOE_NB + 2),
            in_specs=[tspec(0), tspec(2),
                      pl.BlockSpec(memory_space=pl.ANY),
                      pl.BlockSpec((None, MOE_TG, 1, MOE_M), lambda s, j, be, nu: (blk(s, j, 1) // MOE_TG, 0, 0, 0)),
                      wspec((D, D_EXP)), wspec((D, D_EXP)), wspec((D_EXP, D))],
            out_specs=pl.BlockSpec(memory_space=pl.ANY),
            scratch_shapes=[pltpu.VMEM((GROUP * 4 + 8, LANES), I32),
                            pltpu.VMEM(((GROUP + 1) * 8, LANES), F32),
                            pltpu.VMEM((4 * MOE_S, LANES), I32),
                            pltpu.VMEM((4 * MOE_S, LANES), I32),
                            pltpu.VMEM((8 * MOE_S, LANES), F32),
                            pltpu.VMEM((8 * MOE_S, LANES), F32),
                            pltpu.SemaphoreType.DMA((2,))]),
        out_shape=jax.ShapeDtypeStruct((N_GROUPS, GROUP * 8, LANES), F32),
        compiler_params=_cparams(("arbitrary", "arbitrary")),
        name="moe_routed",
    )(block_e, n_used, tok3 * 4, tok3 * 8, x3, row_w,
      _cast_experts(w_gate, layer), _cast_experts(w_up, layer), _cast_experts(w_down, layer))
    return out.reshape(T_ALL * 8, LANES)


def _moe_post_kernel(x_ref, hp_ref, r_ref, mod_ref, sg_ref, su_ref, sd_ref, lng_ref, lnb_ref, o_ref,
                     sgb, sub, sdb, *, tm):
    @pl.when(pl.program_id(0) == 0)
    def _():
        sgb[...] = sg_ref[...].astype(BF16)
        sub[...] = su_ref[...].astype(BF16)
        sdb[...] = sd_ref[...].astype(BF16)

    hb = _unpack_rows(hp_ref, tm, 4)
    g = _dot(hb, sgb[...])
    u = _dot(hb, sub[...])
    sh = _dot((_silu(g) * u).astype(BF16), sdb[...])
    routed = jnp.concatenate([r_ref[pl.ds(c, tm, stride=8), :] for c in range(D // LANES)], axis=1)
    r = ALPHA * x_ref[...] + mod_ref[5:6, :] * (routed + sh)
    o_ref[...] = _layer_norm_rows(r, lng_ref[...], lnb_ref[...])


def _moe_post(x, hp, routed, mod4, layer, sg, su, sd, ln_g, ln_b, tm=512):
    T = x.shape[0]
    return pl.pallas_call(
        functools.partial(_moe_post_kernel, tm=tm),
        grid=(T // tm,),
        in_specs=[pl.BlockSpec((tm, D), lambda i: (i, 0)),
                  pl.BlockSpec((tm * 4, LANES), lambda i: (i, 0)),
                  pl.BlockSpec((tm * 8, LANES), lambda i: (i, 0)),
                  pl.BlockSpec((None, None, 6, D), lambda i: (layer, (i * tm) // GROUP, 0, 0)),
                  pl.BlockSpec((D, D_SH), lambda i: (0, 0)),
                  pl.BlockSpec((D, D_SH), lambda i: (0, 0)),
                  pl.BlockSpec((D_SH, D), lambda i: (0, 0)),
                  pl.BlockSpec((1, D), lambda i: (0, 0)),
                  pl.BlockSpec((1, D), lambda i: (0, 0))],
        out_specs=pl.BlockSpec((tm, D), lambda i: (i, 0)),
        out_shape=jax.ShapeDtypeStruct((T, D), F32),
        scratch_shapes=[pltpu.VMEM((D, D_SH), BF16), pltpu.VMEM((D, D_SH), BF16), pltpu.VMEM((D_SH, D), BF16)],
        compiler_params=_cparams(("arbitrary",)),
        name="shared_expert_ln",
    )(x, hp, routed, mod4, sg, su, sd, ln_g.reshape(1, D), ln_b.reshape(1, D))


def _conv_silu(src_ref, w_ref, dst_ref, T, scale):
    L = ML_L
    nc = T // L
    width = src_ref.shape[1]
    w0, w1, w2 = w_ref[0:1, :], w_ref[1:2, :], w_ref[2:3, :]
    row = lax.broadcasted_iota(I32, (L, width), 0)

    def body(c, carry):
        r0 = pl.multiple_of(c * L, L)
        cur = src_ref[pl.ds(r0, L), :].astype(F32)
        p0 = pl.multiple_of(jnp.maximum(r0 - 16, 0), 16)
        n0 = pl.multiple_of(jnp.minimum(r0 + L, T - 16), 16)
        prev_row = src_ref[pl.ds(p0, 16), :].astype(F32)[15:16, :] * jnp.where(c > 0, 1.0, 0.0).astype(F32)
        next_row = src_ref[pl.ds(n0, 16), :].astype(F32)[0:1, :] * jnp.where(c < nc - 1, 1.0, 0.0).astype(F32)
        prev = jnp.where(row == 0, prev_row, pltpu.roll(cur, 1, axis=0))
        nxt = jnp.where(row == L - 1, next_row, pltpu.roll(cur, L - 1, axis=0))
        dst_ref[pl.ds(r0, L), :] = _silu(w0 * prev + w1 * cur + w2 * nxt) * scale
        return carry

    lax.fori_loop(0, nc, body, 0)


def _mlstm_chunk(q, k, v, G, GT, C, n, m, backward):
    L = ML_L
    ri = lax.broadcasted_iota(I32, (L, L), 0)
    ci = lax.broadcasted_iota(I32, (L, L), 1)
    keep = (ci >= ri) if backward else (ci <= ri)
    A = keep.astype(BF16)
    AT = ((ri >= ci) if backward else (ri <= ci)).astype(BF16)
    ic, fc = (2, 3) if backward else (0, 1)
    last = 0 if backward else L - 1
    b_col = _mask_dot(A, _log_sigmoid(G))[:, fc:fc + 1]
    b_row = _dot_mask(_log_sigmoid(GT), AT)[fc:fc + 1, :]
    ig_col = G[:, ic:ic + 1]
    ig_row = GT[ic:ic + 1, :]
    log_d = jnp.where(keep, b_col - b_row + ig_row, -jnp.inf)
    log_inter = b_col + m
    m_t = jnp.maximum(log_inter, jnp.max(log_d, axis=-1, keepdims=True))
    d = jnp.exp(log_d - m_t)
    w_inter = jnp.exp(log_inter - m_t)
    qb, kb, vb = q.astype(BF16), k.astype(BF16), v.astype(BF16)
    s = _dot_nt(qb, kb) * d
    num = _dot(s.astype(BF16), vb) + w_inter * _dot(qb, C.astype(BF16))
    den = jnp.sum(s, axis=-1, keepdims=True) + w_inter * jnp.sum(q * n, axis=-1, keepdims=True)
    h = num / jnp.maximum(jnp.abs(den), jnp.exp(-m_t))
    m_new = m_t[last:last + 1, :]
    w_last = jnp.exp(b_col[last:last + 1, :] - b_col + ig_col - m_new)
    decay = w_inter[last:last + 1, :]
    kw = k * w_last
    C_new = decay * C + _dot_tn(kw.astype(BF16), vb)
    n_new = decay * n + jnp.sum(kw, axis=0, keepdims=True)
    return h, C_new, n_new, m_new


def _mlstm_kernel(q_ref, k_ref, v_ref, og_ref, g_ref, gt_ref, cq_ref, ck_ref, ng_ref, c0_ref, n0_ref, m0_ref,
                  a_ref, c_out, n_out, m_out, qs, ks, hf, hb, cst, nst, mst, *, T, nh):
    L = ML_L
    nc = T // L
    _conv_silu(q_ref, cq_ref, qs, T, ML_DK ** -0.5)
    _conv_silu(k_ref, ck_ref, ks, T, 1.0)
    cst[...] = c0_ref[...]
    nst[...] = n0_ref[...]
    mst[...] = m0_ref[...]
    un = min(SCAN_UNROLL, nc)

    def body(i, carry):
        for h in range(nh):
            kcols = slice(h * ML_DK, (h + 1) * ML_DK)
            vcols = slice(h * ML_DV, (h + 1) * ML_DV)
            for direction, out in ((0, hf), (1, hb)):
                C, n, m = cst[direction, h], nst[direction, h], mst[direction, h]
                for u in range(un):
                    ci = i * un + u
                    c = (nc - 1 - ci) if direction else ci
                    r0 = pl.multiple_of(c * L, L)
                    hh, C, n, m = _mlstm_chunk(
                        qs[pl.ds(r0, L), kcols], ks[pl.ds(r0, L), kcols], v_ref[pl.ds(r0, L), vcols],
                        g_ref[h, pl.ds(r0, L), :], gt_ref[h, c], C, n, m, backward=bool(direction))
                    out[pl.ds(r0, L), vcols] = hh
                cst[direction, h] = C
                nst[direction, h] = n
                mst[direction, h] = m
        return carry

    lax.fori_loop(0, nc // un, body, 0)
    c_out[...] = cst[...]
    n_out[...] = nst[...]
    m_out[...] = mst[...]

    def finish(c, carry):
        r0 = pl.multiple_of(c * L, L)
        for h in range(nh):
            vcols = slice(h * ML_DV, (h + 1) * ML_DV)
            tot = hf[pl.ds(r0, L), vcols] + hb[pl.ds(r0, L), vcols]
            mu = jnp.mean(tot, axis=-1, keepdims=True)
            cen = tot - mu
            var = jnp.mean(cen * cen, axis=-1, keepdims=True)
            hn = cen * lax.rsqrt(var + EPS) * ng_ref[:, vcols]
            a_ref[pl.ds(r0, L), vcols] = (hn * _sigmoid(og_ref[pl.ds(r0, L), vcols].astype(F32))).astype(BF16)
        return carry

    lax.fori_loop(0, nc, finish, 0)


def _mlstm_scan(z, gh, ght, conv_w, norm_g, C0, n0, m0, T, row_blk0, nseq, nh):
    nc = T // ML_L
    kw, vw = nh * ML_DK, nh * ML_DV
    qcol, kcol = 0, ML_QK // kw
    vcol, ocol = 2 * ML_QK // vw, (2 * ML_QK + ML_V) // vw
    rb = lambda s: row_blk0 + s
    state = lambda *tail: pl.BlockSpec((None, 2, nh) + tail, lambda s, h: (s, 0, h) + (0,) * len(tail))
    return pl.pallas_call(
        functools.partial(_mlstm_kernel, T=T, nh=nh),
        grid=(nseq, ML_H // nh),
        in_specs=[pl.BlockSpec((T, kw), lambda s, h: (rb(s), qcol + h)),
                  pl.BlockSpec((T, kw), lambda s, h: (rb(s), kcol + h)),
                  pl.BlockSpec((T, vw), lambda s, h: (rb(s), vcol + h)),
                  pl.BlockSpec((T, vw), lambda s, h: (rb(s), ocol + h)),
                  pl.BlockSpec((nh, T, 4), lambda s, h: (h, rb(s), 0)),
                  pl.BlockSpec((nh, nc, 4, ML_L), lambda s, h: (h, rb(s), 0, 0)),
                  pl.BlockSpec((3, kw), lambda s, h: (0, qcol + h)),
                  pl.BlockSpec((3, kw), lambda s, h: (0, kcol + h)),
                  pl.BlockSpec((1, vw), lambda s, h: (0, h)),
                  state(ML_DK, ML_DV), state(1, ML_DK), state(1, 1)],
        out_specs=[pl.BlockSpec((T, vw), lambda s, h: (s, h)),
                   state(ML_DK, ML_DV), state(1, ML_DK), state(1, 1)],
        out_shape=[jax.ShapeDtypeStruct((nseq * T, ML_V), BF16),
                   jax.ShapeDtypeStruct((nseq, 2, ML_H, ML_DK, ML_DV), F32),
                   jax.ShapeDtypeStruct((nseq, 2, ML_H, 1, ML_DK), F32),
                   jax.ShapeDtypeStruct((nseq, 2, ML_H, 1, 1), F32)],
        scratch_shapes=[pltpu.VMEM((T, kw), F32), pltpu.VMEM((T, kw), F32),
                        pltpu.VMEM((T, vw), F32), pltpu.VMEM((T, vw), F32),
                        pltpu.VMEM((2, nh, ML_DK, ML_DV), F32), pltpu.VMEM((2, nh, 1, ML_DK), F32),
                        pltpu.VMEM((2, nh, 1, 1), F32)],
        compiler_params=_cparams(("arbitrary", "arbitrary")),
        name="mlstm_scan",
    )(z, z, z, z, gh, ght, conv_w, conv_w, norm_g.reshape(1, ML_V), C0, n0, m0)


def _mlstm_layer(x, mod4, layer, j, w_in, gate_b, conv_w, norm_g, C_lat, n_lat, m_lat):
    z = _proj(x, mod4, layer, w_in, j, 2 * ML_QK + 2 * ML_V, BF16)
    gates = _proj_small(x, mod4, layer, w_in[j][:, 2 * ML_QK + 2 * ML_V:], gate_b[j])
    gh = gates.reshape(T_ALL, 4, ML_H).transpose(2, 0, 1)
    ght = gh.reshape(ML_H, T_ALL // ML_L, ML_L, 4).transpose(0, 1, 3, 2)
    zC = jnp.zeros((BATCH, 2, ML_H, ML_DK, ML_DV), F32)
    zn = jnp.zeros((BATCH, 2, ML_H, 1, ML_DK), F32)
    zm = jnp.zeros((BATCH, 2, ML_H, 1, 1), F32)
    a_p, Cn, nn, mn = _mlstm_scan(z, gh, ght, conv_w[j], norm_g[j], zC, zn, zm, SEQ, 0, BATCH, ML_H)
    a_s, _, _, _ = _mlstm_scan(z, gh, ght, conv_w[j], norm_g[j], C_lat[:, j],
                               n_lat[:, j].reshape(DEC_BATCH, 2, ML_H, 1, ML_DK),
                               m_lat[:, j].reshape(DEC_BATCH, 2, ML_H, 1, 1),
                               DEC_SEQ, BATCH * SEQ // DEC_SEQ, DEC_BATCH, 1)
    a = jnp.concatenate([a_p, a_s], axis=0)
    return a, (Cn, nn.reshape(BATCH, 2, ML_H, ML_DK), mn.reshape(BATCH, 2, ML_H))


def _hgrn_chunk(q, k, v, g, St, backward):
    L = HG_L
    ri = lax.broadcasted_iota(I32, (L, L), 0)
    ci = lax.broadcasted_iota(I32, (L, L), 1)
    keep = (ci >= ri) if backward else (ci <= ri)
    ref = L - 1 - L // 2 if backward else L // 2
    last = 0 if backward else L - 1
    b = _mask_dot(keep.astype(BF16), g)
    b_ref = b[ref:ref + 1, :]
    b_last = b[last:last + 1, :]
    qe = (q * jnp.exp(b - b_ref)).astype(BF16)
    ke = (k * jnp.exp(b_ref - b)).astype(BF16)
    vb = v.astype(BF16)
    a = jnp.where(keep, _dot_nt(qe, ke), 0.0)
    o = _dot(a.astype(BF16), vb) + _dot_nt((q * jnp.exp(b)).astype(BF16), St.astype(BF16))
    kd = (k * jnp.exp(b_last - b)).astype(BF16)
    St_new = jnp.exp(b_last) * St + _dot_tn(vb, kd)
    return o, St_new


def _hgrn_kernel(q_ref, i_ref, ff_ref, fb_ref, og_ref, fbf_ref, fbb_ref, lbr_ref, ng_ref, s0_ref,
                 a_ref, s_out, of, ob, st, *, T, lb_layer, nh):
    L = HG_L
    nc = T // L
    raw = lbr_ref[...]
    e = jnp.exp(raw - jnp.max(raw, axis=0, keepdims=True))
    p = e / jnp.sum(e, axis=0, keepdims=True)
    lb_all = jnp.sum(p[0:lb_layer + 1, :], axis=0, keepdims=True) - p[0:1, :]
    for h in range(nh):
        st[0, h] = s0_ref[0, h].T
        st[1, h] = s0_ref[1, h].T
    un = min(SCAN_UNROLL, nc)

    def body(i, carry):
        for h in range(nh):
            cols = slice(h * HG_DK, (h + 1) * HG_DK)
            lb = lb_all[:, cols]
            for direction, out, f_ref, b_ref in ((0, of, ff_ref, fbf_ref), (1, ob, fb_ref, fbb_ref)):
                St = st[direction, h]
                for u in range(un):
                    ci = i * un + u
                    c = (nc - 1 - ci) if direction else ci
                    r0 = pl.multiple_of(c * L, L)
                    f = lb + (1.0 - lb) * _sigmoid(f_ref[pl.ds(r0, L), cols] + b_ref[:, cols])
                    o, St = _hgrn_chunk(_silu(q_ref[pl.ds(r0, L), cols]), 1.0 - f, i_ref[pl.ds(r0, L), cols],
                                        jnp.log(f), St, backward=bool(direction))
                    out[pl.ds(r0, L), cols] = o
                st[direction, h] = St
        return carry

    lax.fori_loop(0, nc // un, body, 0)
    for h in range(nh):
        s_out[0, h] = st[0, h].T
        s_out[1, h] = st[1, h].T

    def finish(c, carry):
        r0 = pl.multiple_of(c * L, L)
        for h in range(nh):
            cols = slice(h * HG_DV, (h + 1) * HG_DV)
            tot = of[pl.ds(r0, L), cols] + ob[pl.ds(r0, L), cols]
            on = tot * lax.rsqrt(jnp.mean(tot * tot, axis=-1, keepdims=True) + EPS) * ng_ref[:, cols]
            a_ref[pl.ds(r0, L), cols] = (on * _silu(og_ref[pl.ds(r0, L), cols])).astype(BF16)
        return carry

    lax.fori_loop(0, nc, finish, 0)


def _hgrn_scan(z, f_b, lb_raw, norm_g, S0, T, row_blk0, nseq, lb_layer, nh):
    nhb = HG_H // nh
    w = nh * HG_DK
    rb = lambda s: row_blk0 + s
    zspec = lambda cb: pl.BlockSpec((T, w), lambda s, h: (rb(s), cb * nhb + h))
    sspec = pl.BlockSpec((None, 2, nh, HG_DK, HG_DV), lambda s, h: (s, 0, h, 0, 0))
    return pl.pallas_call(
        functools.partial(_hgrn_kernel, T=T, lb_layer=lb_layer, nh=nh),
        grid=(nseq, nhb),
        in_specs=[zspec(0), zspec(1), zspec(2), zspec(3), zspec(4),
                  pl.BlockSpec((1, w), lambda s, h: (0, h)),
                  pl.BlockSpec((1, w), lambda s, h: (0, nhb + h)),
                  pl.BlockSpec((DEPTH, w), lambda s, h: (0, h)),
                  pl.BlockSpec((1, w), lambda s, h: (0, h)),
                  sspec],
        out_specs=[pl.BlockSpec((T, w), lambda s, h: (s, h)), sspec],
        out_shape=[jax.ShapeDtypeStruct((nseq * T, HG_V), BF16),
                   jax.ShapeDtypeStruct((nseq, 2, HG_H, HG_DK, HG_DV), F32)],
        scratch_shapes=[pltpu.VMEM((T, w), F32), pltpu.VMEM((T, w), F32),
                        pltpu.VMEM((2, nh, HG_DV, HG_DK), F32)],
        compiler_params=_cparams(("arbitrary", "arbitrary")),
        name="hgrn_scan",
    )(z, z, z, z, z, f_b.reshape(1, 2 * HG_K), f_b.reshape(1, 2 * HG_K), lb_raw, norm_g.reshape(1, HG_V), S0)


def _hgrn_layer(x, mod4, layer, j, w_in, f_b, lb_raw, norm_g, S_lat):
    z = _proj(x, mod4, layer, w_in, j, 3 * HG_K + 2 * HG_V, F32)
    zS = jnp.zeros((BATCH, 2, HG_H, HG_DK, HG_DV), F32)
    a_p, Sn = _hgrn_scan(z, f_b[j], lb_raw, norm_g[j], zS, SEQ, 0, BATCH, layer, 4)
    a_s, _ = _hgrn_scan(z, f_b[j], lb_raw, norm_g[j], S_lat[:, j], DEC_SEQ, BATCH * SEQ // DEC_SEQ, DEC_BATCH,
                        layer, 1)
    return jnp.concatenate([a_p, a_s], axis=0), Sn


def _head_rms(x, g_tiled, nheads):
    lane = lax.broadcasted_iota(I32, x.shape, 1)
    sq = x * x
    ms = jnp.zeros_like(x)
    for h in range(nheads):
        in_h = (lane >= h * AT_HD) & (lane < (h + 1) * AT_HD)
        tot = jnp.sum(jnp.where(in_h, sq, 0.0), axis=-1, keepdims=True)
        ms = jnp.where(in_h, tot, ms)
    return x * lax.rsqrt(ms * (1.0 / AT_HD) + EPS) * g_tiled


def _rope(x, cos, sin):
    w = x.shape[1]
    lane = lax.broadcasted_iota(I32, x.shape, 1)
    up = pltpu.roll(x, w - 16, axis=1)
    down = pltpu.roll(x, 16, axis=1)
    swapped = jnp.where((lane % 32) < 16, up, down)
    return x * cos + swapped * sin


def _attn_ctx_kernel(q_ref, k_ref, v_ref, qn_ref, kn_ref, o_ref, ko_ref, vo_ref):
    k = _head_rms(k_ref[...].astype(F32), kn_ref[...], AT_KV)
    ko_ref[...] = k
    v = v_ref[...].astype(F32)
    vo_ref[...] = v
    kb, vb = k.astype(BF16), v.astype(BF16)
    outs = []
    for kv in range(AT_KV):
        q = _head_rms(q_ref[:, kv * 256:(kv + 1) * 256].astype(F32), qn_ref[...], AT_G) * (AT_HD ** -0.5)
        kh = kb[:, kv * AT_HD:(kv + 1) * AT_HD]
        vh = vb[:, kv * AT_HD:(kv + 1) * AT_HD]
        for g in range(AT_G):
            s = _dot_nt(q[:, g * AT_HD:(g + 1) * AT_HD].astype(BF16), kh)
            p = jnp.exp(s - jnp.max(s, axis=-1, keepdims=True))
            p = p / jnp.sum(p, axis=-1, keepdims=True)
            outs.append(_dot(p.astype(BF16), vh))
    o_ref[...] = jnp.concatenate(outs, axis=1).astype(BF16)


def _attn_ctx(z, q_norm, k_norm):
    qn = jnp.tile(q_norm, AT_G).reshape(1, 256)
    kn = jnp.tile(k_norm, AT_KV).reshape(1, 256)
    return pl.pallas_call(
        _attn_ctx_kernel,
        grid=(BATCH,),
        in_specs=[pl.BlockSpec((SEQ, D), lambda b: (b, 0)),
                  pl.BlockSpec((SEQ, 256), lambda b: (b, 4)),
                  pl.BlockSpec((SEQ, 256), lambda b: (b, 5)),
                  pl.BlockSpec((1, 256), lambda b: (0, 0)),
                  pl.BlockSpec((1, 256), lambda b: (0, 0))],
        out_specs=[pl.BlockSpec((SEQ, D), lambda b: (b, 0)),
                   pl.BlockSpec((SEQ, 256), lambda b: (b, 0)),
                   pl.BlockSpec((SEQ, 256), lambda b: (b, 0))],
        out_shape=[jax.ShapeDtypeStruct((BATCH * SEQ, D), BF16),
                   jax.ShapeDtypeStruct((BATCH * SEQ, 256), F32),
                   jax.ShapeDtypeStruct((BATCH * SEQ, 256), F32)],
        compiler_params=_cparams(("arbitrary",)),
        name="attn_context",
    )(z, z, z, qn, kn)


def _attn_kv_kernel(k_ref, v_ref, kn_ref, cos_ref, sin_ref, ko_ref, vo_ref):
    k = _rope(_head_rms(k_ref[...].astype(F32), kn_ref[...], AT_KV), cos_ref[...], sin_ref[...])
    v = v_ref[...]
    for h in range(AT_KV):
        ko_ref[h] = k[:, h * AT_HD:(h + 1) * AT_HD].astype(BF16)
        vo_ref[h] = v[:, h * AT_HD:(h + 1) * AT_HD].astype(BF16)


def _attn_kv(z, k_norm, cos4, sin4, tt=512):
    kn = jnp.tile(k_norm, AT_KV).reshape(1, 256)
    nt = DEC_SEQ // tt
    row0 = BATCH * SEQ // tt
    return pl.pallas_call(
        _attn_kv_kernel,
        grid=(DEC_BATCH, nt),
        in_specs=[pl.BlockSpec((tt, 256), lambda b, i: (row0 + b * nt + i, 4)),
                  pl.BlockSpec((tt, 256), lambda b, i: (row0 + b * nt + i, 5)),
                  pl.BlockSpec((1, 256), lambda b, i: (0, 0)),
                  pl.BlockSpec((tt, 256), lambda b, i: (i, 0)),
                  pl.BlockSpec((tt, 256), lambda b, i: (i, 0))],
        out_specs=[pl.BlockSpec((None, AT_KV, tt, AT_HD), lambda b, i: (b, 0, i, 0)),
                   pl.BlockSpec((None, AT_KV, tt, AT_HD), lambda b, i: (b, 0, i, 0))],
        out_shape=[jax.ShapeDtypeStruct((DEC_BATCH, AT_KV, DEC_SEQ, AT_HD), BF16),
                   jax.ShapeDtypeStruct((DEC_BATCH, AT_KV, DEC_SEQ, AT_HD), BF16)],
        compiler_params=_cparams(("arbitrary", "arbitrary")),
        name="attn_kv_prep",
    )(z, z, kn, cos4, sin4)


def _attn_lat_kernel(q_ref, qn_ref, cos_ref, sin_ref, k_ref, v_ref, o_ref, *, tq, tk):
    q = _rope(_head_rms(q_ref[...].astype(F32), qn_ref[...], AT_G), cos_ref[...], sin_ref[...]) * (AT_HD ** -0.5)
    qs = jnp.concatenate([q[:, g * AT_HD:(g + 1) * AT_HD] for g in range(AT_G)], axis=0).astype(BF16)
    rows = AT_G * tq
    nk = k_ref.shape[0] // tk

    def body(j, carry):
        m, l, acc = carry
        k0 = pl.multiple_of(j * tk, tk)
        s = _dot_nt(qs, k_ref[pl.ds(k0, tk), :])
        m_new = jnp.maximum(m, jnp.max(s, axis=-1, keepdims=True))
        alpha = jnp.exp(m - m_new)
        p = jnp.exp(s - m_new)
        l = alpha * l + jnp.sum(p, axis=-1, keepdims=True)
        acc = alpha * acc + _dot(p.astype(BF16), v_ref[pl.ds(k0, tk), :])
        return m_new, l, acc

    init = (jnp.full((rows, 1), -jnp.inf, F32), jnp.zeros((rows, 1), F32), jnp.zeros((rows, AT_HD), F32))
    _, l, acc = lax.fori_loop(0, nk, body, init)
    o = acc / l
    o_ref[...] = jnp.concatenate([o[g * tq:(g + 1) * tq, :] for g in range(AT_G)], axis=1).astype(BF16)


def _attn_lat(z, q_norm, cos4, sin4, kk, vv, tq=256, tk=1536):
    qn = jnp.tile(q_norm, AT_G).reshape(1, 256)
    nq = DEC_SEQ // tq
    row0 = BATCH * SEQ // tq
    skv = kk.shape[2]
    return pl.pallas_call(
        functools.partial(_attn_lat_kernel, tq=tq, tk=tk),
        grid=(DEC_BATCH, AT_KV, nq),
        in_specs=[pl.BlockSpec((tq, 256), lambda b, h, i: (row0 + b * nq + i, h)),
                  pl.BlockSpec((1, 256), lambda b, h, i: (0, 0)),
                  pl.BlockSpec((tq, 256), lambda b, h, i: (i, 0)),
                  pl.BlockSpec((tq, 256), lambda b, h, i: (i, 0)),
                  pl.BlockSpec((None, None, skv, AT_HD), lambda b, h, i: (b, h, 0, 0)),
                  pl.BlockSpec((None, None, skv, AT_HD), lambda b, h, i: (b, h, 0, 0))],
        out_specs=pl.BlockSpec((tq, 256), lambda b, h, i: (b * nq + i, h)),
        out_shape=jax.ShapeDtypeStruct((DEC_BATCH * DEC_SEQ, D), BF16),
        compiler_params=_cparams(("arbitrary", "arbitrary", "arbitrary")),
        name="attn_latent",
    )(z, qn, cos4, sin4, kk, vv)


def _rope_tables():
    t = jnp.arange(DEC_SEQ)
    row = (t // GRID_W).astype(F32)
    col = (t % GRID_W).astype(F32)
    nf = AT_HD // 4
    inv = ROPE_THETA ** (-jnp.arange(nf, dtype=F32) / nf)
    ar, ac = row[:, None] * inv[None], col[:, None] * inv[None]
    cos = jnp.concatenate([jnp.cos(ar), jnp.cos(ar), jnp.cos(ac), jnp.cos(ac)], axis=1)
    sin = jnp.concatenate([-jnp.sin(ar), jnp.sin(ar), -jnp.sin(ac), jnp.sin(ac)], axis=1)
    return jnp.tile(cos, (1, 4)), jnp.tile(sin, (1, 4))


def _attn_layer(x, mod4, layer, j, w_in, q_norm, k_norm, cache_k, cache_v):
    z = _proj(x, mod4, layer, w_in, j, (AT_H + 2 * AT_KV) * AT_HD, BF16)
    a_p, k_new, v_new = _attn_ctx(z, q_norm[j], k_norm[j])
    cos4, sin4 = _rope_tables()
    k_lat, v_lat = _attn_kv(z, k_norm[j], cos4, sin4)
    kk = jnp.concatenate([cache_k[:, j].transpose(0, 2, 1, 3).astype(BF16), k_lat], axis=2)
    vv = jnp.concatenate([cache_v[:, j].transpose(0, 2, 1, 3).astype(BF16), v_lat], axis=2)
    a_s = _attn_lat(z, q_norm[j], cos4, sin4, kk, vv)
    a = jnp.concatenate([a_p, a_s], axis=0)
    return a, (k_new.reshape(BATCH, SEQ, AT_KV, AT_HD), v_new.reshape(BATCH, SEQ, AT_KV, AT_HD))


def kernel(x_prompt, x_sample, state_mlstm_C, state_mlstm_n, state_mlstm_m, cache_attn_k, cache_attn_v, state_hgrn_S, c, c_ctx, mod_w, mod_b, ln_g, ln_b, mlstm_w_in, mlstm_gate_b, mlstm_conv, mlstm_norm, mlstm_w_out, attn_w_in, attn_q_norm, attn_k_norm, attn_w_out, hgrn_w_in, hgrn_f_b, hgrn_lower_bounds, hgrn_norm, hgrn_w_out, moe_router, moe_router_b, moe_w_gate, moe_w_up, moe_w_down, moe_sh_gate, moe_sh_up, moe_sh_down):
    x = jnp.concatenate([x_prompt.reshape(BATCH * SEQ, D), x_sample.reshape(DEC_BATCH * DEC_SEQ, D)], axis=0)
    cond8 = jnp.concatenate([c_ctx[None], c, jnp.zeros((8 - 1 - DEC_BATCH, D), F32)], axis=0)
    mod4 = _mod_all(cond8, mod_w, mod_b)

    new_C, new_n, new_m, new_k, new_v, new_S = [], [], [], [], [], []
    for l in range(DEPTH):
        kind, j = l % 3, l // 3
        if kind == 0:
            a, (Cn, nn, mn) = _mlstm_layer(x, mod4, l, j, mlstm_w_in, mlstm_gate_b, mlstm_conv, mlstm_norm,
                                           state_mlstm_C, state_mlstm_n, state_mlstm_m)
            new_C.append(Cn)
            new_n.append(nn)
            new_m.append(mn)
            w_out = mlstm_w_out[j]
        elif kind == 1:
            a, (kn, vn) = _attn_layer(x, mod4, l, j, attn_w_in, attn_q_norm, attn_k_norm, cache_attn_k, cache_attn_v)
            new_k.append(kn)
            new_v.append(vn)
            w_out = attn_w_out[j]
        else:
            a, Sn = _hgrn_layer(x, mod4, l, j, hgrn_w_in, hgrn_f_b, hgrn_lower_bounds, hgrn_norm, state_hgrn_S)
            new_S.append(Sn)
            w_out = hgrn_w_out[j]
        x, hp, idx128, w128 = _post(a, x, mod4, l, w_out, ln_g[l, 0], ln_b[l, 0], moe_router[l], moe_router_b[l])
        routed = _moe_routed(hp, _route_tables(idx128, w128), l, moe_w_gate, moe_w_up, moe_w_down)
        x = _moe_post(x, hp, routed, mod4, l, moe_sh_gate[l], moe_sh_up[l], moe_sh_down[l], ln_g[l, 1], ln_b[l, 1])

    xp = x[:BATCH * SEQ].reshape(BATCH, SEQ, D)
    xs = x[BATCH * SEQ:].reshape(DEC_BATCH, DEC_SEQ, D)
    return (xp, xs, jnp.stack(new_C, 1), jnp.stack(new_n, 1), jnp.stack(new_m, 1),
            jnp.stack(new_k, 1), jnp.stack(new_v, 1), jnp.stack(new_S, 1))
```

```python
import functools

import jax
import jax.numpy as jnp
from jax import lax
from jax.experimental import pallas as pl
from jax.experimental.pallas import tpu as pltpu

F32 = jnp.float32
BF16 = jnp.bfloat16
I32 = jnp.int32
HI = lax.Precision.HIGHEST

D = 1024
BATCH, SEQ = 16, 256
DEPTH = 4
DEC_BATCH, DEC_SEQ = 2, 4096
PAST = 512
GRID_W = 64
GROUP = 4096
N_GROUPS = 3
T_ALL = N_GROUPS * GROUP

ML_H, ML_DK, ML_DV, ML_L = 4, 128, 256, 128
ML_QK, ML_V = ML_H * ML_DK, ML_H * ML_DV
AT_H, AT_KV, AT_HD, AT_G = 16, 4, 64, 4
HG_H, HG_DK, HG_DV, HG_L = 8, 128, 128, 64
HG_K, HG_V = HG_H * HG_DK, HG_H * HG_DV
N_EXP, TOP_K, D_EXP, D_SH = 64, 8, 256, 256
ROUTE_SCALE = 2.5
ALPHA = (2 * DEPTH) ** 0.25
EPS = 1e-6
ROPE_THETA = 10000.0

VMEM_LIMIT = 56 * 1024 * 1024
LANES = 128

MOE_NS = 2
MOE_G = T_ALL // MOE_NS
MOE_M = 256
MOE_NB = MOE_G * TOP_K // MOE_M + N_EXP
MOE_ROWS = MOE_NB * MOE_M
MOE_KEY = 16384
MOE_S = MOE_M + 8
MOE_U = 8
MOE_TG = 1
SCAN_UNROLL = 4


def _cparams(sem):
    return pltpu.CompilerParams(dimension_semantics=sem, vmem_limit_bytes=VMEM_LIMIT)


def _sigmoid(x):
    return 1.0 / (1.0 + jnp.exp(-x))


def _silu(x):
    return x * _sigmoid(x)


def _log_sigmoid(x):
    return jnp.minimum(x, 0.0) - jnp.log(1.0 + jnp.exp(-jnp.abs(x)))


def _dot(a, b):
    return jnp.dot(a, b, preferred_element_type=F32)


def _dot_nt(a, b):
    return lax.dot_general(a, b, (((1,), (1,)), ((), ())), preferred_element_type=F32)


def _dot_tn(a, b):
    return lax.dot_general(a, b, (((0,), (0,)), ((), ())), preferred_element_type=F32)


def _dot_hi(a, b):
    return jnp.dot(a, b, preferred_element_type=F32, precision=HI)


def _split3(x):
    p0 = x.astype(BF16)
    r1 = x - p0.astype(F32)
    p1 = r1.astype(BF16)
    p2 = (r1 - p1.astype(F32)).astype(BF16)
    return p0, p1, p2


def _mask_dot(mask_bf16, x):
    p0, p1, p2 = _split3(x)
    return _dot(mask_bf16, p0) + _dot(mask_bf16, p1) + _dot(mask_bf16, p2)


def _dot_mask(x, mask_bf16):
    p0, p1, p2 = _split3(x)
    return _dot(p0, mask_bf16) + _dot(p1, mask_bf16) + _dot(p2, mask_bf16)


def _dot_3x(a, b):
    a0 = a.astype(BF16)
    a1 = (a - a0.astype(F32)).astype(BF16)
    b0 = b.astype(BF16)
    b1 = (b - b0.astype(F32)).astype(BF16)
    return _dot(a0, b0) + _dot(a0, b1) + _dot(a1, b0)


def _mod_kernel(cond_ref, w_ref, b_ref, o_ref):
    o_ref[...] = _dot_hi(_silu(cond_ref[...]), w_ref[...]) + b_ref[...]


def _mod_all(cond8, mod_w, mod_b):
    tn = 1024
    out = pl.pallas_call(
        _mod_kernel,
        grid=(DEPTH, 6 * D // tn),
        in_specs=[pl.BlockSpec((8, D), lambda l, j: (0, 0)),
                  pl.BlockSpec((None, D, tn), lambda l, j: (l, 0, j)),
                  pl.BlockSpec((None, 1, tn), lambda l, j: (l, 0, j))],
        out_specs=pl.BlockSpec((None, 8, tn), lambda l, j: (l, 0, j)),
        out_shape=jax.ShapeDtypeStruct((DEPTH, 8, 6 * D), F32),
        compiler_params=_cparams(("arbitrary", "arbitrary")),
        name="mod_rows",
    )(cond8, mod_w, mod_b.reshape(DEPTH, 1, 6 * D))
    return out.reshape(DEPTH, 8, 6, D)


def _proj_kernel(x_ref, mod_ref, w_ref, o_ref, h_scr):
    @pl.when(pl.program_id(1) == 0)
    def _():
        h_scr[...] = (x_ref[...] * (1.0 + mod_ref[1:2, :]) + mod_ref[0:1, :]).astype(BF16)

    o_ref[...] = _dot(h_scr[...], w_ref[...].astype(BF16)).astype(o_ref.dtype)


def _proj(x, mod4, layer, w3, widx, ncols, out_dtype, tm=2048, tn=512):
    T = x.shape[0]
    return pl.pallas_call(
        _proj_kernel,
        grid=(T // tm, ncols // tn),
        in_specs=[pl.BlockSpec((tm, D), lambda i, j: (i, 0)),
                  pl.BlockSpec((None, None, 6, D), lambda i, j: (layer, (i * tm) // GROUP, 0, 0)),
                  pl.BlockSpec((None, D, tn), lambda i, j: (widx, 0, j))],
        out_specs=pl.BlockSpec((tm, tn), lambda i, j: (i, j)),
        out_shape=jax.ShapeDtypeStruct((T, ncols), out_dtype),
        scratch_shapes=[pltpu.VMEM((tm, D), BF16)],
        compiler_params=_cparams(("arbitrary", "arbitrary")),
        name="in_proj",
    )(x, mod4, w3)


def _proj_small_kernel(x_ref, mod_ref, w_ref, b_ref, o_ref):
    h = x_ref[...] * (1.0 + mod_ref[1:2, :]) + mod_ref[0:1, :]
    o_ref[...] = _dot_hi(h, w_ref[...]) + b_ref[...]


def _proj_small(x, mod4, layer, w, b, tm=1024):
    T, n = x.shape[0], w.shape[1]
    return pl.pallas_call(
        _proj_small_kernel,
        grid=(T // tm,),
        in_specs=[pl.BlockSpec((tm, D), lambda i: (i, 0)),
                  pl.BlockSpec((None, None, 6, D), lambda i: (layer, (i * tm) // GROUP, 0, 0)),
                  pl.BlockSpec((D, n), lambda i: (0, 0)),
                  pl.BlockSpec((1, n), lambda i: (0, 0))],
        out_specs=pl.BlockSpec((tm, n), lambda i: (i, 0)),
        out_shape=jax.ShapeDtypeStruct((T, n), F32),
        compiler_params=_cparams(("arbitrary",)),
        name="gate_proj",
    )(x, mod4, w, b.reshape(1, n))


def _layer_norm_rows(r, g, b):
    mu = jnp.mean(r, axis=-1, keepdims=True)
    c = r - mu
    var = jnp.mean(c * c, axis=-1, keepdims=True)
    return c * lax.rsqrt(var + EPS) * g + b


def _pack_bf16_pairs(lo, hi):
    lo_b = lax.bitcast_convert_type(lo.astype(BF16).astype(F32), I32)
    hi_b = lax.bitcast_convert_type(hi.astype(BF16).astype(F32), I32)
    return lax.shift_right_logical(lo_b, 16) | (hi_b & jnp.int32(-65536))


def _unpack_bf16_pairs(v):
    lo = lax.bitcast_convert_type(lax.shift_left(v, 16), F32)
    hi = lax.bitcast_convert_type(v & jnp.int32(-65536), F32)
    return lo.astype(BF16), hi.astype(BF16)


def _unpack_rows(ref, rows, stride, offset=0):
    chunks = []
    for r in range(4):
        if stride == 4:
            v = ref[pl.ds(r, rows, stride=4), :]
        else:
            v = ref[pl.ds(offset + r * stride, rows), :]
        chunks.extend(_unpack_bf16_pairs(v))
    return jnp.concatenate(chunks, axis=1)


def _post_kernel(a_ref, x_ref, mod_ref, w_ref, lng_ref, lnb_ref, rw_ref, rb_ref,
                 xo_ref, hp_ref, idx_ref, wt_ref, wb_scr, *, tm):
    @pl.when(pl.program_id(0) == 0)
    def _():
        wb_scr[...] = w_ref[...].astype(BF16)

    y = _dot(a_ref[...], wb_scr[...])
    r = ALPHA * x_ref[...] + mod_ref[2:3, :] * y
    xn = _layer_norm_rows(r, lng_ref[...], lnb_ref[...])
    xo_ref[...] = xn
    h1 = xn * (1.0 + mod_ref[4:5, :]) + mod_ref[3:4, :]
    for p in range(4):
        lo = h1[:, (2 * p) * LANES:(2 * p + 1) * LANES]
        hi = h1[:, (2 * p + 1) * LANES:(2 * p + 2) * LANES]
        hp_ref[pl.ds(p, tm, stride=4), :] = _pack_bf16_pairs(lo, hi)

    scores = _sigmoid(_dot_3x(h1, rw_ref[...]))
    sel = scores + rb_ref[...]
    e_iota = lax.broadcasted_iota(I32, (tm, N_EXP), 1).astype(F32)
    lane = lax.broadcasted_iota(I32, (tm, LANES), 1)
    idx_out = jnp.zeros((tm, LANES), F32)
    w_out = jnp.zeros((tm, LANES), F32)
    total = jnp.zeros((tm, 1), F32)
    for k in range(TOP_K):
        mx = jnp.max(sel, axis=-1, keepdims=True)
        ik = jnp.min(jnp.where(sel == mx, e_iota, float(N_EXP)), axis=-1, keepdims=True)
        hit = e_iota == ik
        wk = jnp.sum(jnp.where(hit, scores, 0.0), axis=-1, keepdims=True)
        sel = jnp.where(hit, -jnp.inf, sel)
        total = total + wk
        idx_out = jnp.where(lane == k, ik, idx_out)
        w_out = jnp.where(lane == k, wk, w_out)
    idx_ref[...] = idx_out.astype(I32)
    wt_ref[...] = w_out / total * ROUTE_SCALE


def _post(a, x, mod4, layer, w_out, ln_g, ln_b, router_w, router_b, tm=512):
    T = x.shape[0]
    return pl.pallas_call(
        functools.partial(_post_kernel, tm=tm),
        grid=(T // tm,),
        in_specs=[pl.BlockSpec((tm, D), lambda i: (i, 0)),
                  pl.BlockSpec((tm, D), lambda i: (i, 0)),
                  pl.BlockSpec((None, None, 6, D), lambda i: (layer, (i * tm) // GROUP, 0, 0)),
                  pl.BlockSpec((D, D), lambda i: (0, 0)),
                  pl.BlockSpec((1, D), lambda i: (0, 0)),
                  pl.BlockSpec((1, D), lambda i: (0, 0)),
                  pl.BlockSpec((D, N_EXP), lambda i: (0, 0)),
                  pl.BlockSpec((1, N_EXP), lambda i: (0, 0))],
        out_specs=[pl.BlockSpec((tm, D), lambda i: (i, 0)),
                   pl.BlockSpec((tm * 4, LANES), lambda i: (i, 0)),
                   pl.BlockSpec((tm, LANES), lambda i: (i, 0)),
                   pl.BlockSpec((tm, LANES), lambda i: (i, 0))],
        out_shape=[jax.ShapeDtypeStruct((T, D), F32),
                   jax.ShapeDtypeStruct((T * 4, LANES), I32),
                   jax.ShapeDtypeStruct((T, LANES), I32),
                   jax.ShapeDtypeStruct((T, LANES), F32)],
        scratch_shapes=[pltpu.VMEM((D, D), BF16)],
        compiler_params=_cparams(("arbitrary",)),
        name="out_proj_ln_router",
    )(a, x, mod4, w_out, ln_g.reshape(1, D), ln_b.reshape(1, D), router_w, router_b.reshape(1, N_EXP))


def _route_tables(idx128, w128):
    idx = idx128[:, :TOP_K].reshape(MOE_NS, MOE_G, TOP_K)
    w = w128[:, :TOP_K].reshape(MOE_NS, MOE_G, TOP_K)
    onehot = idx[..., None] == jnp.arange(N_EXP, dtype=I32)
    counts = jnp.sum(onehot.astype(I32), axis=(1, 2))
    padded = (counts + MOE_M - 1) // MOE_M * MOE_M
    pad_end = jnp.cumsum(padded, axis=-1)
    tok = jnp.arange(MOE_G, dtype=I32)[None, :, None]
    real_keys = (idx * MOE_KEY + tok).reshape(MOE_NS, MOE_G * TOP_K)
    fill = jnp.arange(MOE_M, dtype=I32)[None, None, :]
    e_ids = jnp.arange(N_EXP, dtype=I32)[None, :, None]
    fill_keys = jnp.where(fill < (padded - counts)[:, :, None], e_ids * MOE_KEY + MOE_KEY // 2 + fill,
                          N_EXP * MOE_KEY + e_ids * MOE_M + fill).reshape(MOE_NS, N_EXP * MOE_M)
    keys = jnp.concatenate([real_keys, fill_keys], axis=1)
    vals = jnp.concatenate([w.reshape(MOE_NS, MOE_G * TOP_K), jnp.zeros((MOE_NS, N_EXP * MOE_M), F32)], axis=1)
    sorted_rows = [lax.sort((keys[s], vals[s]), dimension=0, num_keys=1) for s in range(MOE_NS)]
    keys = jnp.stack([k for k, _ in sorted_rows])
    row_w = jnp.stack([v for _, v in sorted_rows])
    row_tok = jnp.where(keys < N_EXP * MOE_KEY, jnp.minimum(keys & (MOE_KEY - 1), MOE_G), MOE_G)
    row_tok = row_tok.reshape(-1)
    n_used = pad_end[:, -1] // MOE_M
    starts = jnp.arange(MOE_NB, dtype=I32) * MOE_M
    block_e = jnp.sum((starts[None, :, None] >= pad_end[:, None, :]).astype(I32), axis=-1)
    block_e = jnp.minimum(block_e, N_EXP - 1)
    last_e = jnp.take_along_axis(block_e, jnp.maximum(n_used - 1, 0)[:, None], axis=1)
    block_e = jnp.where(jnp.arange(MOE_NB, dtype=I32)[None, :] < n_used[:, None], block_e, last_e)
    return (block_e.reshape(-1).astype(I32), n_used.astype(I32), row_tok,
            row_w.reshape(MOE_NS * MOE_NB // MOE_TG, MOE_TG, 1, MOE_M))


def _cast_kernel(x_ref, o_ref):
    o_ref[...] = x_ref[...].astype(BF16)


def _cast_experts(w4, layer, eb=4):
    _, n_e, a, b = w4.shape
    return pl.pallas_call(
        _cast_kernel,
        grid=(n_e // eb,),
        in_specs=[pl.BlockSpec((None, eb, a, b), lambda i: (layer, i, 0, 0))],
        out_specs=pl.BlockSpec((eb, a, b), lambda i: (i, 0, 0)),
        out_shape=jax.ShapeDtypeStruct((n_e, a, b), BF16),
        compiler_params=_cparams(("arbitrary",)),
        name="cast_experts",
    )(w4)


def _moe_stage(rows, tg_ref, ts_ref, x_v, acc, rw_ref, wg_ref, wu_ref, wd_ref, tile_g, tile_c, ys_c, ys_s):
    rg, rc, rs = rows
    for mi in range(MOE_M):
        tile_g[pl.ds(mi, 4, stride=MOE_S), :] = x_v[pl.ds(pl.multiple_of(tg_ref[rg, mi], 4), 4), :]
    xb = _unpack_rows(tile_c, MOE_M, MOE_S)
    g = _dot(xb, wg_ref[...])
    u = _dot(xb, wu_ref[...])
    ri = lax.broadcasted_iota(I32, (MOE_M, MOE_M), 0)
    ci = lax.broadcasted_iota(I32, (MOE_M, MOE_M), 1)
    rw_col = jnp.sum(jnp.where(ri == ci, jnp.broadcast_to(rw_ref[rc], (MOE_M, MOE_M)), 0.0),
                     axis=1, keepdims=True)
    a = (_silu(g) * u) * rw_col
    y = _dot(a.astype(BF16), wd_ref[...])
    for c in range(D // LANES):
        ys_c[pl.ds(c * MOE_S, MOE_M), :] = y[:, c * LANES:(c + 1) * LANES]
    for m0 in range(0, MOE_M, MOE_U):
        offs = [pl.multiple_of(ts_ref[rs, m0 + j], 8) for j in range(MOE_U)]
        vals = [acc[pl.ds(offs[j], 8), :] + ys_s[pl.ds(m0 + j, 8, stride=MOE_S), :] for j in range(MOE_U)]
        for j in range(MOE_U):
            acc[pl.ds(offs[j], 8), :] = vals[j]


def _moe_kernel(be_ref, nu_ref, tg_ref, ts_ref, x_hbm, rw_ref, wg_ref, wu_ref, wd_ref, out_hbm,
                x_v, acc, tile_a, tile_b, ys_a, ys_b, sem):
    s = pl.program_id(0)
    j = pl.program_id(1)

    @pl.when(j == 0)
    def _():
        cp = pltpu.make_async_copy(x_hbm.at[s], x_v.at[pl.ds(0, MOE_G * 4)], sem.at[0])
        cp.start()
        cp.wait()
        x_v[pl.ds(MOE_G * 4, 8), :] = jnp.zeros((8, LANES), I32)

        def clear(i, carry):
            acc[pl.ds(pl.multiple_of(i * 8, 8), 8), :] = jnp.zeros((8, LANES), F32)
            return carry

        lax.fori_loop(0, MOE_G + 1, clear, 0)
        for t in (tile_a, tile_b):
            t[...] = jnp.zeros(t.shape, I32)
        for y in (ys_a, ys_b):
            y[...] = jnp.zeros(y.shape, F32)

    live = j < nu_ref[s] + 2
    if MOE_TG == 1:
        rows = (0, 0, 0)
    else:
        rows = tuple(jnp.clip(j - d, 0, MOE_NB - 1) % MOE_TG for d in range(3))
    common = (rows, tg_ref, ts_ref, x_v, acc, rw_ref, wg_ref, wu_ref, wd_ref)

    @pl.when(live & (j % 2 == 0))
    def _():
        _moe_stage(*common, tile_a, tile_b, ys_b, ys_a)

    @pl.when(live & (j % 2 == 1))
    def _():
        _moe_stage(*common, tile_b, tile_a, ys_a, ys_b)

    @pl.when(j == MOE_NB + 1)
    def _():
        cp = pltpu.make_async_copy(acc.at[pl.ds(0, MOE_G * 8)], out_hbm.at[s], sem.at[1])
        cp.start()
        cp.wait()


def _moe_routed(hp, tables, layer, w_gate, w_up, w_down):
    block_e, n_used, row_tok, row_w = tables
    x3 = hp.reshape(MOE_NS, MOE_G * 4, LANES)
    blk = lambda s, j, d: s * MOE_NB + jnp.clip(j - d, 0, MOE_NB - 1)
    wspec = lambda shape: pl.BlockSpec((None,) + shape, lambda s, j, be, nu: (be[blk(s, j, 1)], 0, 0))
    tspec = lambda d: pl.BlockSpec((None, MOE_TG, MOE_M), lambda s, j, be, nu: (blk(s, j, d) // MOE_TG, 0, 0),
                                   memory_space=pltpu.SMEM)
    tok3 = row_tok.reshape(MOE_NS * MOE_NB // MOE_TG, MOE_TG, MOE_M)
    out = pl.pallas_call(
        _moe_kernel,
        grid_spec=pltpu.PrefetchScalarGridSpec(
            num_scalar_prefetch=2,
            grid=(MOE_NS, MOE_NB + 2),
            in_specs=[tspec(0), tspec(2),
                      pl.BlockSpec(memory_space=pl.ANY),
                      pl.BlockSpec((None, MOE_TG, 1, MOE_M), lambda s, j, be, nu: (blk(s, j, 1) // MOE_TG, 0, 0, 0)),
                      wspec((D, D_EXP)), wspec((D, D_EXP)), wspec((D_EXP, D))],
            out_specs=pl.BlockSpec(memory_space=pl.ANY),
            scratch_shapes=[pltpu.VMEM((MOE_G * 4 + 8, LANES), I32),
                            pltpu.VMEM(((MOE_G + 1) * 8, LANES), F32),
                            pltpu.VMEM((4 * MOE_S, LANES), I32),
                            pltpu.VMEM((4 * MOE_S, LANES), I32),
                            pltpu.VMEM((8 * MOE_S, LANES), F32),
                            pltpu.VMEM((8 * MOE_S, LANES), F32),
                            pltpu.SemaphoreType.DMA((2,))]),
        out_shape=jax.ShapeDtypeStruct((MOE_NS, MOE_G * 8, LANES), F32),
        compiler_params=_cparams(("arbitrary", "arbitrary")),
        name="moe_routed",
    )(block_e, n_used, tok3 * 4, tok3 * 8, x3, row_w,
      _cast_experts(w_gate, layer), _cast_experts(w_up, layer), _cast_experts(w_down, layer))
    return out.reshape(T_ALL * 8, LANES)


def _moe_post_kernel(x_ref, hp_ref, r_ref, mod_ref, sg_ref, su_ref, sd_ref, lng_ref, lnb_ref, o_ref,
                     sgb, sub, sdb, *, tm):
    @pl.when(pl.program_id(0) == 0)
    def _():
        sgb[...] = sg_ref[...].astype(BF16)
        sub[...] = su_ref[...].astype(BF16)
        sdb[...] = sd_ref[...].astype(BF16)

    hb = _unpack_rows(hp_ref, tm, 4)
    g = _dot(hb, sgb[...])
    u = _dot(hb, sub[...])
    sh = _dot((_silu(g) * u).astype(BF16), sdb[...])
    routed = jnp.concatenate([r_ref[pl.ds(c, tm, stride=8), :] for c in range(D // LANES)], axis=1)
    r = ALPHA * x_ref[...] + mod_ref[5:6, :] * (routed + sh)
    o_ref[...] = _layer_norm_rows(r, lng_ref[...], lnb_ref[...])


def _moe_post(x, hp, routed, mod4, layer, sg, su, sd, ln_g, ln_b, tm=512):
    T = x.shape[0]
    return pl.pallas_call(
        functools.partial(_moe_post_kernel, tm=tm),
        grid=(T // tm,),
        in_specs=[pl.BlockSpec((tm, D), lambda i: (i, 0)),
                  pl.BlockSpec((tm * 4, LANES), lambda i: (i, 0)),
                  pl.BlockSpec((tm * 8, LANES), lambda i: (i, 0)),
                  pl.BlockSpec((None, None, 6, D), lambda i: (layer, (i * tm) // GROUP, 0, 0)),
                  pl.BlockSpec((D, D_SH), lambda i: (0, 0)),
                  pl.BlockSpec((D, D_SH), lambda i: (0, 0)),
                  pl.BlockSpec((D_SH, D), lambda i: (0, 0)),
                  pl.BlockSpec((1, D), lambda i: (0, 0)),
                  pl.BlockSpec((1, D), lambda i: (0, 0))],
        out_specs=pl.BlockSpec((tm, D), lambda i: (i, 0)),
        out_shape=jax.ShapeDtypeStruct((T, D), F32),
        scratch_shapes=[pltpu.VMEM((D, D_SH), BF16), pltpu.VMEM((D, D_SH), BF16), pltpu.VMEM((D_SH, D), BF16)],
        compiler_params=_cparams(("arbitrary",)),
        name="shared_expert_ln",
    )(x, hp, routed, mod4, sg, su, sd, ln_g.reshape(1, D), ln_b.reshape(1, D))


def _conv_silu(src_ref, w_ref, dst_ref, T, scale):
    L = ML_L
    nc = T // L
    width = src_ref.shape[1]
    w0, w1, w2 = w_ref[0:1, :], w_ref[1:2, :], w_ref[2:3, :]
    row = lax.broadcasted_iota(I32, (L, width), 0)

    def body(c, carry):
        r0 = pl.multiple_of(c * L, L)
        cur = src_ref[pl.ds(r0, L), :].astype(F32)
        p0 = pl.multiple_of(jnp.maximum(r0 - 16, 0), 16)
        n0 = pl.multiple_of(jnp.minimum(r0 + L, T - 16), 16)
        prev_row = src_ref[pl.ds(p0, 16), :].astype(F32)[15:16, :] * jnp.where(c > 0, 1.0, 0.0).astype(F32)
        next_row = src_ref[pl.ds(n0, 16), :].astype(F32)[0:1, :] * jnp.where(c < nc - 1, 1.0, 0.0).astype(F32)
        prev = jnp.where(row == 0, prev_row, pltpu.roll(cur, 1, axis=0))
        nxt = jnp.where(row == L - 1, next_row, pltpu.roll(cur, L - 1, axis=0))
        dst_ref[pl.ds(r0, L), :] = _silu(w0 * prev + w1 * cur + w2 * nxt) * scale
        return carry

    lax.fori_loop(0, nc, body, 0)


def _mlstm_chunk(q, k, v, G, GT, C, n, m, backward):
    L = ML_L
    ri = lax.broadcasted_iota(I32, (L, L), 0)
    ci = lax.broadcasted_iota(I32, (L, L), 1)
    keep = (ci >= ri) if backward else (ci <= ri)
    A = keep.astype(BF16)
    AT = ((ri >= ci) if backward else (ri <= ci)).astype(BF16)
    ic, fc = (2, 3) if backward else (0, 1)
    last = 0 if backward else L - 1
    b_col = _mask_dot(A, _log_sigmoid(G))[:, fc:fc + 1]
    b_row = _dot_mask(_log_sigmoid(GT), AT)[fc:fc + 1, :]
    ig_col = G[:, ic:ic + 1]
    ig_row = GT[ic:ic + 1, :]
    log_d = jnp.where(keep, b_col - b_row + ig_row, -jnp.inf)
    log_inter = b_col + m
    m_t = jnp.maximum(log_inter, jnp.max(log_d, axis=-1, keepdims=True))
    d = jnp.exp(log_d - m_t)
    w_inter = jnp.exp(log_inter - m_t)
    qb, kb, vb = q.astype(BF16), k.astype(BF16), v.astype(BF16)
    s = _dot_nt(qb, kb) * d
    num = _dot(s.astype(BF16), vb) + w_inter * _dot(qb, C.astype(BF16))
    den = jnp.sum(s, axis=-1, keepdims=True) + w_inter * jnp.sum(q * n, axis=-1, keepdims=True)
    h = num / jnp.maximum(jnp.abs(den), jnp.exp(-m_t))
    m_new = m_t[last:last + 1, :]
    w_last = jnp.exp(b_col[last:last + 1, :] - b_col + ig_col - m_new)
    decay = w_inter[last:last + 1, :]
    kw = k * w_last
    C_new = decay * C + _dot_tn(kw.astype(BF16), vb)
    n_new = decay * n + jnp.sum(kw, axis=0, keepdims=True)
    return h, C_new, n_new, m_new


def _mlstm_kernel(q_ref, k_ref, v_ref, og_ref, g_ref, gt_ref, cq_ref, ck_ref, ng_ref, c0_ref, n0_ref, m0_ref,
                  a_ref, c_out, n_out, m_out, qs, ks, hf, hb, cst, nst, mst, *, T, nh):
    L = ML_L
    nc = T // L
    _conv_silu(q_ref, cq_ref, qs, T, ML_DK ** -0.5)
    _conv_silu(k_ref, ck_ref, ks, T, 1.0)
    cst[...] = c0_ref[...]
    nst[...] = n0_ref[...]
    mst[...] = m0_ref[...]
    un = min(SCAN_UNROLL, nc)

    def body(i, carry):
        for h in range(nh):
            kcols = slice(h * ML_DK, (h + 1) * ML_DK)
            vcols = slice(h * ML_DV, (h + 1) * ML_DV)
            for direction, out in ((0, hf), (1, hb)):
                C, n, m = cst[direction, h], nst[direction, h], mst[direction, h]
                for u in range(un):
                    ci = i * un + u
                    c = (nc - 1 - ci) if direction else ci
                    r0 = pl.multiple_of(c * L, L)
                    hh, C, n, m = _mlstm_chunk(
                        qs[pl.ds(r0, L), kcols], ks[pl.ds(r0, L), kcols], v_ref[pl.ds(r0, L), vcols],
                        g_ref[h, pl.ds(r0, L), :], gt_ref[h, c], C, n, m, backward=bool(direction))
                    out[pl.ds(r0, L), vcols] = hh
                cst[direction, h] = C
                nst[direction, h] = n
                mst[direction, h] = m
        return carry

    lax.fori_loop(0, nc // un, body, 0)
    c_out[...] = cst[...]
    n_out[...] = nst[...]
    m_out[...] = mst[...]

    def finish(c, carry):
        r0 = pl.multiple_of(c * L, L)
        for h in range(nh):
            vcols = slice(h * ML_DV, (h + 1) * ML_DV)
            tot = hf[pl.ds(r0, L), vcols] + hb[pl.ds(r0, L), vcols]
            mu = jnp.mean(tot, axis=-1, keepdims=True)
            cen = tot - mu
            var = jnp.mean(cen * cen, axis=-1, keepdims=True)
            hn = cen * lax.rsqrt(var + EPS) * ng_ref[:, vcols]
            a_ref[pl.ds(r0, L), vcols] = (hn * _sigmoid(og_ref[pl.ds(r0, L), vcols].astype(F32))).astype(BF16)
        return carry

    lax.fori_loop(0, nc, finish, 0)


def _mlstm_scan(z, gh, ght, conv_w, norm_g, C0, n0, m0, T, row_blk0, nseq, nh):
    nc = T // ML_L
    kw, vw = nh * ML_DK, nh * ML_DV
    qcol, kcol = 0, ML_QK // kw
    vcol, ocol = 2 * ML_QK // vw, (2 * ML_QK + ML_V) // vw
    rb = lambda s: row_blk0 + s
    state = lambda *tail: pl.BlockSpec((None, 2, nh) + tail, lambda s, h: (s, 0, h) + (0,) * len(tail))
    return pl.pallas_call(
        functools.partial(_mlstm_kernel, T=T, nh=nh),
        grid=(nseq, ML_H // nh),
        in_specs=[pl.BlockSpec((T, kw), lambda s, h: (rb(s), qcol + h)),
                  pl.BlockSpec((T, kw), lambda s, h: (rb(s), kcol + h)),
                  pl.BlockSpec((T, vw), lambda s, h: (rb(s), vcol + h)),
                  pl.BlockSpec((T, vw), lambda s, h: (rb(s), ocol + h)),
                  pl.BlockSpec((nh, T, 4), lambda s, h: (h, rb(s), 0)),
                  pl.BlockSpec((nh, nc, 4, ML_L), lambda s, h: (h, rb(s), 0, 0)),
                  pl.BlockSpec((3, kw), lambda s, h: (0, qcol + h)),
                  pl.BlockSpec((3, kw), lambda s, h: (0, kcol + h)),
                  pl.BlockSpec((1, vw), lambda s, h: (0, h)),
                  state(ML_DK, ML_DV), state(1, ML_DK), state(1, 1)],
        out_specs=[pl.BlockSpec((T, vw), lambda s, h: (s, h)),
                   state(ML_DK, ML_DV), state(1, ML_DK), state(1, 1)],
        out_shape=[jax.ShapeDtypeStruct((nseq * T, ML_V), BF16),
                   jax.ShapeDtypeStruct((nseq, 2, ML_H, ML_DK, ML_DV), F32),
                   jax.ShapeDtypeStruct((nseq, 2, ML_H, 1, ML_DK), F32),
                   jax.ShapeDtypeStruct((nseq, 2, ML_H, 1, 1), F32)],
        scratch_shapes=[pltpu.VMEM((T, kw), F32), pltpu.VMEM((T, kw), F32),
                        pltpu.VMEM((T, vw), F32), pltpu.VMEM((T, vw), F32),
                        pltpu.VMEM((2, nh, ML_DK, ML_DV), F32), pltpu.VMEM((2, nh, 1, ML_DK), F32),
                        pltpu.VMEM((2, nh, 1, 1), F32)],
        compiler_params=_cparams(("arbitrary", "arbitrary")),
        name="mlstm_scan",
    )(z, z, z, z, gh, ght, conv_w, conv_w, norm_g.reshape(1, ML_V), C0, n0, m0)


def _mlstm_layer(x, mod4, layer, j, w_in, gate_b, conv_w, norm_g, C_lat, n_lat, m_lat):
    z = _proj(x, mod4, layer, w_in, j, 2 * ML_QK + 2 * ML_V, BF16)
    gates = _proj_small(x, mod4, layer, w_in[j][:, 2 * ML_QK + 2 * ML_V:], gate_b[j])
    gh = gates.reshape(T_ALL, 4, ML_H).transpose(2, 0, 1)
    ght = gh.reshape(ML_H, T_ALL // ML_L, ML_L, 4).transpose(0, 1, 3, 2)
    zC = jnp.zeros((BATCH, 2, ML_H, ML_DK, ML_DV), F32)
    zn = jnp.zeros((BATCH, 2, ML_H, 1, ML_DK), F32)
    zm = jnp.zeros((BATCH, 2, ML_H, 1, 1), F32)
    a_p, Cn, nn, mn = _mlstm_scan(z, gh, ght, conv_w[j], norm_g[j], zC, zn, zm, SEQ, 0, BATCH, ML_H)
    a_s, _, _, _ = _mlstm_scan(z, gh, ght, conv_w[j], norm_g[j], C_lat[:, j],
                               n_lat[:, j].reshape(DEC_BATCH, 2, ML_H, 1, ML_DK),
                               m_lat[:, j].reshape(DEC_BATCH, 2, ML_H, 1, 1),
                               DEC_SEQ, BATCH * SEQ // DEC_SEQ, DEC_BATCH, 1)
    a = jnp.concatenate([a_p, a_s], axis=0)
    return a, (Cn, nn.reshape(BATCH, 2, ML_H, ML_DK), mn.reshape(BATCH, 2, ML_H))


def _hgrn_chunk(q, k, v, g, St, backward):
    L = HG_L
    ri = lax.broadcasted_iota(I32, (L, L), 0)
    ci = lax.broadcasted_iota(I32, (L, L), 1)
    keep = (ci >= ri) if backward else (ci <= ri)
    ref = L - 1 - L // 2 if backward else L // 2
    last = 0 if backward else L - 1
    b = _mask_dot(keep.astype(BF16), g)
    b_ref = b[ref:ref + 1, :]
    b_last = b[last:last + 1, :]
    qe = (q * jnp.exp(b - b_ref)).astype(BF16)
    ke = (k * jnp.exp(b_ref - b)).astype(BF16)
    vb = v.astype(BF16)
    a = jnp.where(keep, _dot_nt(qe, ke), 0.0)
    o = _dot(a.astype(BF16), vb) + _dot_nt((q * jnp.exp(b)).astype(BF16), St.astype(BF16))
    kd = (k * jnp.exp(b_last - b)).astype(BF16)
    St_new = jnp.exp(b_last) * St + _dot_tn(vb, kd)
    return o, St_new


def _hgrn_kernel(q_ref, i_ref, ff_ref, fb_ref, og_ref, fbf_ref, fbb_ref, lbr_ref, ng_ref, s0_ref,
                 a_ref, s_out, of, ob, st, *, T, lb_layer, nh):
    L = HG_L
    nc = T // L
    raw = lbr_ref[...]
    e = jnp.exp(raw - jnp.max(raw, axis=0, keepdims=True))
    p = e / jnp.sum(e, axis=0, keepdims=True)
    lb_all = jnp.sum(p[0:lb_layer + 1, :], axis=0, keepdims=True) - p[0:1, :]
    for h in range(nh):
        st[0, h] = s0_ref[0, h].T
        st[1, h] = s0_ref[1, h].T
    un = min(SCAN_UNROLL, nc)

    def body(i, carry):
        for h in range(nh):
            cols = slice(h * HG_DK, (h + 1) * HG_DK)
            lb = lb_all[:, cols]
            for direction, out, f_ref, b_ref in ((0, of, ff_ref, fbf_ref), (1, ob, fb_ref, fbb_ref)):
                St = st[direction, h]
                for u in range(un):
                    ci = i * un + u
                    c = (nc - 1 - ci) if direction else ci
                    r0 = pl.multiple_of(c * L, L)
                    f = lb + (1.0 - lb) * _sigmoid(f_ref[pl.ds(r0, L), cols] + b_ref[:, cols])
                    o, St = _hgrn_chunk(_silu(q_ref[pl.ds(r0, L), cols]), 1.0 - f, i_ref[pl.ds(r0, L), cols],
                                        jnp.log(f), St, backward=bool(direction))
                    out[pl.ds(r0, L), cols] = o
                st[direction, h] = St
        return carry

    lax.fori_loop(0, nc // un, body, 0)
    for h in range(nh):
        s_out[0, h] = st[0, h].T
        s_out[1, h] = st[1, h].T

    def finish(c, carry):
        r0 = pl.multiple_of(c * L, L)
        for h in range(nh):
            cols = slice(h * HG_DV, (h + 1) * HG_DV)
            tot = of[pl.ds(r0, L), cols] + ob[pl.ds(r0, L), cols]
            on = tot * lax.rsqrt(jnp.mean(tot * tot, axis=-1, keepdims=True) + EPS) * ng_ref[:, cols]
            a_ref[pl.ds(r0, L), cols] = (on * _silu(og_ref[pl.ds(r0, L), cols])).astype(BF16)
        return carry

    lax.fori_loop(0, nc, finish, 0)


def _hgrn_scan(z, f_b, lb_raw, norm_g, S0, T, row_blk0, nseq, lb_layer, nh):
    nhb = HG_H // nh
    w = nh * HG_DK
    rb = lambda s: row_blk0 + s
    zspec = lambda cb: pl.BlockSpec((T, w), lambda s, h: (rb(s), cb * nhb + h))
    sspec = pl.BlockSpec((None, 2, nh, HG_DK, HG_DV), lambda s, h: (s, 0, h, 0, 0))
    return pl.pallas_call(
        functools.partial(_hgrn_kernel, T=T, lb_layer=lb_layer, nh=nh),
        grid=(nseq, nhb),
        in_specs=[zspec(0), zspec(1), zspec(2), zspec(3), zspec(4),
                  pl.BlockSpec((1, w), lambda s, h: (0, h)),
                  pl.BlockSpec((1, w), lambda s, h: (0, nhb + h)),
                  pl.BlockSpec((DEPTH, w), lambda s, h: (0, h)),
                  pl.BlockSpec((1, w), lambda s, h: (0, h)),
                  sspec],
        out_specs=[pl.BlockSpec((T, w), lambda s, h: (s, h)), sspec],
        out_shape=[jax.ShapeDtypeStruct((nseq * T, HG_V), BF16),
                   jax.ShapeDtypeStruct((nseq, 2, HG_H, HG_DK, HG_DV), F32)],
        scratch_shapes=[pltpu.VMEM((T, w), F32), pltpu.VMEM((T, w), F32),
                        pltpu.VMEM((2, nh, HG_DV, HG_DK), F32)],
        compiler_params=_cparams(("arbitrary", "arbitrary")),
        name="hgrn_scan",
    )(z, z, z, z, z, f_b.reshape(1, 2 * HG_K), f_b.reshape(1, 2 * HG_K), lb_raw, norm_g.reshape(1, HG_V), S0)


def _hgrn_layer(x, mod4, layer, j, w_in, f_b, lb_raw, norm_g, S_lat):
    z = _proj(x, mod4, layer, w_in, j, 3 * HG_K + 2 * HG_V, F32)
    zS = jnp.zeros((BATCH, 2, HG_H, HG_DK, HG_DV), F32)
    a_p, Sn = _hgrn_scan(z, f_b[j], lb_raw, norm_g[j], zS, SEQ, 0, BATCH, layer, 4)
    a_s, _ = _hgrn_scan(z, f_b[j], lb_raw, norm_g[j], S_lat[:, j], DEC_SEQ, BATCH * SEQ // DEC_SEQ, DEC_BATCH,
                        layer, 1)
    return jnp.concatenate([a_p, a_s], axis=0), Sn


def _head_rms(x, g_tiled, nheads):
    lane = lax.broadcasted_iota(I32, x.shape, 1)
    sq = x * x
    ms = jnp.zeros_like(x)
    for h in range(nheads):
        in_h = (lane >= h * AT_HD) & (lane < (h + 1) * AT_HD)
        tot = jnp.sum(jnp.where(in_h, sq, 0.0), axis=-1, keepdims=True)
        ms = jnp.where(in_h, tot, ms)
    return x * lax.rsqrt(ms * (1.0 / AT_HD) + EPS) * g_tiled


def _rope(x, cos, sin):
    w = x.shape[1]
    lane = lax.broadcasted_iota(I32, x.shape, 1)
    up = pltpu.roll(x, w - 16, axis=1)
    down = pltpu.roll(x, 16, axis=1)
    swapped = jnp.where((lane % 32) < 16, up, down)
    return x * cos + swapped * sin


def _attn_ctx_kernel(q_ref, k_ref, v_ref, qn_ref, kn_ref, o_ref, ko_ref, vo_ref):
    k = _head_rms(k_ref[...].astype(F32), kn_ref[...], AT_KV)
    ko_ref[...] = k
    v = v_ref[...].astype(F32)
    vo_ref[...] = v
    kb, vb = k.astype(BF16), v.astype(BF16)
    outs = []
    for kv in range(AT_KV):
        q = _head_rms(q_ref[:, kv * 256:(kv + 1) * 256].astype(F32), qn_ref[...], AT_G) * (AT_HD ** -0.5)
        kh = kb[:, kv * AT_HD:(kv + 1) * AT_HD]
        vh = vb[:, kv * AT_HD:(kv + 1) * AT_HD]
        for g in range(AT_G):
            s = _dot_nt(q[:, g * AT_HD:(g + 1) * AT_HD].astype(BF16), kh)
            p = jnp.exp(s - jnp.max(s, axis=-1, keepdims=True))
            p = p / jnp.sum(p, axis=-1, keepdims=True)
            outs.append(_dot(p.astype(BF16), vh))
    o_ref[...] = jnp.concatenate(outs, axis=1).astype(BF16)


def _attn_ctx(z, q_norm, k_norm):
    qn = jnp.tile(q_norm, AT_G).reshape(1, 256)
    kn = jnp.tile(k_norm, AT_KV).reshape(1, 256)
    return pl.pallas_call(
        _attn_ctx_kernel,
        grid=(BATCH,),
        in_specs=[pl.BlockSpec((SEQ, D), lambda b: (b, 0)),
                  pl.BlockSpec((SEQ, 256), lambda b: (b, 4)),
                  pl.BlockSpec((SEQ, 256), lambda b: (b, 5)),
                  pl.BlockSpec((1, 256), lambda b: (0, 0)),
                  pl.BlockSpec((1, 256), lambda b: (0, 0))],
        out_specs=[pl.BlockSpec((SEQ, D), lambda b: (b, 0)),
                   pl.BlockSpec((SEQ, 256), lambda b: (b, 0)),
                   pl.BlockSpec((SEQ, 256), lambda b: (b, 0))],
        out_shape=[jax.ShapeDtypeStruct((BATCH * SEQ, D), BF16),
                   jax.ShapeDtypeStruct((BATCH * SEQ, 256), F32),
                   jax.ShapeDtypeStruct((BATCH * SEQ, 256), F32)],
        compiler_params=_cparams(("arbitrary",)),
        name="attn_context",
    )(z, z, z, qn, kn)


def _attn_kv_kernel(k_ref, v_ref, kn_ref, cos_ref, sin_ref, ko_ref, vo_ref):
    k = _rope(_head_rms(k_ref[...].astype(F32), kn_ref[...], AT_KV), cos_ref[...], sin_ref[...])
    v = v_ref[...]
    for h in range(AT_KV):
        ko_ref[h] = k[:, h * AT_HD:(h + 1) * AT_HD].astype(BF16)
        vo_ref[h] = v[:, h * AT_HD:(h + 1) * AT_HD].astype(BF16)


def _attn_kv(z, k_norm, cos4, sin4, tt=512):
    kn = jnp.tile(k_norm, AT_KV).reshape(1, 256)
    nt = DEC_SEQ // tt
    row0 = BATCH * SEQ // tt
    return pl.pallas_call(
        _attn_kv_kernel,
        grid=(DEC_BATCH, nt),
        in_specs=[pl.BlockSpec((tt, 256), lambda b, i: (row0 + b * nt + i, 4)),
                  pl.BlockSpec((tt, 256), lambda b, i: (row0 + b * nt + i, 5)),
                  pl.BlockSpec((1, 256), lambda b, i: (0, 0)),
                  pl.BlockSpec((tt, 256), lambda b, i: (i, 0)),
                  pl.BlockSpec((tt, 256), lambda b, i: (i, 0))],
        out_specs=[pl.BlockSpec((None, AT_KV, tt, AT_HD), lambda b, i: (b, 0, i, 0)),
                   pl.BlockSpec((None, AT_KV, tt, AT_HD), lambda b, i: (b, 0, i, 0))],
        out_shape=[jax.ShapeDtypeStruct((DEC_BATCH, AT_KV, DEC_SEQ, AT_HD), BF16),
                   jax.ShapeDtypeStruct((DEC_BATCH, AT_KV, DEC_SEQ, AT_HD), BF16)],
        compiler_params=_cparams(("arbitrary", "arbitrary")),
        name="attn_kv_prep",
    )(z, z, kn, cos4, sin4)


def _attn_lat_kernel(q_ref, qn_ref, cos_ref, sin_ref, k_ref, v_ref, o_ref, *, tq, tk):
    q = _rope(_head_rms(q_ref[...].astype(F32), qn_ref[...], AT_G), cos_ref[...], sin_ref[...]) * (AT_HD ** -0.5)
    qs = jnp.concatenate([q[:, g * AT_HD:(g + 1) * AT_HD] for g in range(AT_G)], axis=0).astype(BF16)
    rows = AT_G * tq
    nk = k_ref.shape[0] // tk

    def body(j, carry):
        m, l, acc = carry
        k0 = pl.multiple_of(j * tk, tk)
        s = _dot_nt(qs, k_ref[pl.ds(k0, tk), :])
        m_new = jnp.maximum(m, jnp.max(s, axis=-1, keepdims=True))
        alpha = jnp.exp(m - m_new)
        p = jnp.exp(s - m_new)
        l = alpha * l + jnp.sum(p, axis=-1, keepdims=True)
        acc = alpha * acc + _dot(p.astype(BF16), v_ref[pl.ds(k0, tk), :])
        return m_new, l, acc

    init = (jnp.full((rows, 1), -jnp.inf, F32), jnp.zeros((rows, 1), F32), jnp.zeros((rows, AT_HD), F32))
    _, l, acc = lax.fori_loop(0, nk, body, init)
    o = acc / l
    o_ref[...] = jnp.concatenate([o[g * tq:(g + 1) * tq, :] for g in range(AT_G)], axis=1).astype(BF16)


def _attn_lat(z, q_norm, cos4, sin4, kk, vv, tq=256, tk=1536):
    qn = jnp.tile(q_norm, AT_G).reshape(1, 256)
    nq = DEC_SEQ // tq
    row0 = BATCH * SEQ // tq
    skv = kk.shape[2]
    return pl.pallas_call(
        functools.partial(_attn_lat_kernel, tq=tq, tk=tk),
        grid=(DEC_BATCH, AT_KV, nq),
        in_specs=[pl.BlockSpec((tq, 256), lambda b, h, i: (row0 + b * nq + i, h)),
                  pl.BlockSpec((1, 256), lambda b, h, i: (0, 0)),
                  pl.BlockSpec((tq, 256), lambda b, h, i: (i, 0)),
                  pl.BlockSpec((tq, 256), lambda b, h, i: (i, 0)),
                  pl.BlockSpec((None, None, skv, AT_HD), lambda b, h, i: (b, h, 0, 0)),
                  pl.BlockSpec((None, None, skv, AT_HD), lambda b, h, i: (b, h, 0, 0))],
        out_specs=pl.BlockSpec((tq, 256), lambda b, h, i: (b * nq + i, h)),
        out_shape=jax.ShapeDtypeStruct((DEC_BATCH * DEC_SEQ, D), BF16),
        compiler_params=_cparams(("arbitrary", "arbitrary", "arbitrary")),
        name="attn_latent",
    )(z, qn, cos4, sin4, kk, vv)


def _rope_tables():
    t = jnp.arange(DEC_SEQ)
    row = (t // GRID_W).astype(F32)
    col = (t % GRID_W).astype(F32)
    nf = AT_HD // 4
    inv = ROPE_THETA ** (-jnp.arange(nf, dtype=F32) / nf)
    ar, ac = row[:, None] * inv[None], col[:, None] * inv[None]
    cos = jnp.concatenate([jnp.cos(ar), jnp.cos(ar), jnp.cos(ac), jnp.cos(ac)], axis=1)
    sin = jnp.concatenate([-jnp.sin(ar), jnp.sin(ar), -jnp.sin(ac), jnp.sin(ac)], axis=1)
    return jnp.tile(cos, (1, 4)), jnp.tile(sin, (1, 4))


def _attn_layer(x, mod4, layer, j, w_in, q_norm, k_norm, cache_k, cache_v):
    z = _proj(x, mod4, layer, w_in, j, (AT_H + 2 * AT_KV) * AT_HD, BF16)
    a_p, k_new, v_new = _attn_ctx(z, q_norm[j], k_norm[j])
    cos4, sin4 = _rope_tables()
    k_lat, v_lat = _attn_kv(z, k_norm[j], cos4, sin4)
    kk = jnp.concatenate([cache_k[:, j].transpose(0, 2, 1, 3).astype(BF16), k_lat], axis=2)
    vv = jnp.concatenate([cache_v[:, j].transpose(0, 2, 1, 3).astype(BF16), v_lat], axis=2)
    a_s = _attn_lat(z, q_norm[j], cos4, sin4, kk, vv)
    a = jnp.concatenate([a_p, a_s], axis=0)
    return a, (k_new.reshape(BATCH, SEQ, AT_KV, AT_HD), v_new.reshape(BATCH, SEQ, AT_KV, AT_HD))


def kernel(x_prompt, x_sample, state_mlstm_C, state_mlstm_n, state_mlstm_m, cache_attn_k, cache_attn_v, state_hgrn_S, c, c_ctx, mod_w, mod_b, ln_g, ln_b, mlstm_w_in, mlstm_gate_b, mlstm_conv, mlstm_norm, mlstm_w_out, attn_w_in, attn_q_norm, attn_k_norm, attn_w_out, hgrn_w_in, hgrn_f_b, hgrn_lower_bounds, hgrn_norm, hgrn_w_out, moe_router, moe_router_b, moe_w_gate, moe_w_up, moe_w_down, moe_sh_gate, moe_sh_up, moe_sh_down):
    x = jnp.concatenate([x_prompt.reshape(BATCH * SEQ, D), x_sample.reshape(DEC_BATCH * DEC_SEQ, D)], axis=0)
    cond8 = jnp.concatenate([c_ctx[None], c, jnp.zeros((8 - 1 - DEC_BATCH, D), F32)], axis=0)
    mod4 = _mod_all(cond8, mod_w, mod_b)

    new_C, new_n, new_m, new_k, new_v, new_S = [], [], [], [], [], []
    for l in range(DEPTH):
        kind, j = l % 3, l // 3
        if kind == 0:
            a, (Cn, nn, mn) = _mlstm_layer(x, mod4, l, j, mlstm_w_in, mlstm_gate_b, mlstm_conv, mlstm_norm,
                                           state_mlstm_C, state_mlstm_n, state_mlstm_m)
            new_C.append(Cn)
            new_n.append(nn)
            new_m.append(mn)
            w_out = mlstm_w_out[j]
        elif kind == 1:
            a, (kn, vn) = _attn_layer(x, mod4, l, j, attn_w_in, attn_q_norm, attn_k_norm, cache_attn_k, cache_attn_v)
            new_k.append(kn)
            new_v.append(vn)
            w_out = attn_w_out[j]
        else:
            a, Sn = _hgrn_layer(x, mod4, l, j, hgrn_w_in, hgrn_f_b, hgrn_lower_bounds, hgrn_norm, state_hgrn_S)
            new_S.append(Sn)
            w_out = hgrn_w_out[j]
        x, hp, idx128, w128 = _post(a, x, mod4, l, w_out, ln_g[l, 0], ln_b[l, 0], moe_router[l], moe_router_b[l])
        routed = _moe_routed(hp, _route_tables(idx128, w128), l, moe_w_gate, moe_w_up, moe_w_down)
        x = _moe_post(x, hp, routed, mod4, l, moe_sh_gate[l], moe_sh_up[l], moe_sh_down[l], ln_g[l, 1], ln_b[l, 1])

    xp = x[:BATCH * SEQ].reshape(BATCH, SEQ, D)
    xs = x[BATCH * SEQ:].reshape(DEC_BATCH, DEC_SEQ, D)
    return (xp, xs, jnp.stack(new_C, 1), jnp.stack(new_n, 1), jnp.stack(new_m, 1),
            jnp.stack(new_k, 1), jnp.stack(new_v, 1), jnp.stack(new_S, 1))
```

```python
import functools

import jax
import jax.numpy as jnp
from jax import lax
from jax.experimental import pallas as pl
from jax.experimental.pallas import tpu as pltpu

F32 = jnp.float32
BF16 = jnp.bfloat16
I32 = jnp.int32
HI = lax.Precision.HIGHEST

D = 1024
BATCH, SEQ = 16, 256
DEPTH = 4
DEC_BATCH, DEC_SEQ = 2, 4096
PAST = 512
GRID_W = 64
GROUP = 4096
N_GROUPS = 3
T_ALL = N_GROUPS * GROUP

ML_H, ML_DK, ML_DV, ML_L = 4, 128, 256, 128
ML_QK, ML_V = ML_H * ML_DK, ML_H * ML_DV
AT_H, AT_KV, AT_HD, AT_G = 16, 4, 64, 4
HG_H, HG_DK, HG_DV, HG_L = 8, 128, 128, 64
HG_K, HG_V = HG_H * HG_DK, HG_H * HG_DV
N_EXP, TOP_K, D_EXP, D_SH = 64, 8, 256, 256
ROUTE_SCALE = 2.5
ALPHA = (2 * DEPTH) ** 0.25
EPS = 1e-6
ROPE_THETA = 10000.0

VMEM_LIMIT = 56 * 1024 * 1024
LANES = 128

MOE_NS = 2
MOE_G = T_ALL // MOE_NS
MOE_M = 256
MOE_NB = MOE_G * TOP_K // MOE_M + N_EXP
MOE_ROWS = MOE_NB * MOE_M
MOE_KEY = 16384
MOE_S = MOE_M + 8
MOE_U = 8
MOE_TG = 1
SCAN_UNROLL = 4


def _cparams(sem):
    return pltpu.CompilerParams(dimension_semantics=sem, vmem_limit_bytes=VMEM_LIMIT)


def _sigmoid(x):
    return 1.0 / (1.0 + jnp.exp(-x))


def _silu(x):
    return x * _sigmoid(x)


def _log_sigmoid(x):
    return jnp.minimum(x, 0.0) - jnp.log(1.0 + jnp.exp(-jnp.abs(x)))


def _dot(a, b):
    return jnp.dot(a, b, preferred_element_type=F32)


def _dot_nt(a, b):
    return lax.dot_general(a, b, (((1,), (1,)), ((), ())), preferred_element_type=F32)


def _dot_tn(a, b):
    return lax.dot_general(a, b, (((0,), (0,)), ((), ())), preferred_element_type=F32)


def _dot_hi(a, b):
    return jnp.dot(a, b, preferred_element_type=F32, precision=HI)


def _split3(x):
    p0 = x.astype(BF16)
    r1 = x - p0.astype(F32)
    p1 = r1.astype(BF16)
    p2 = (r1 - p1.astype(F32)).astype(BF16)
    return p0, p1, p2


def _mask_dot(mask_bf16, x):
    p0, p1, p2 = _split3(x)
    return _dot(mask_bf16, p0) + _dot(mask_bf16, p1) + _dot(mask_bf16, p2)


def _dot_mask(x, mask_bf16):
    p0, p1, p2 = _split3(x)
    return _dot(p0, mask_bf16) + _dot(p1, mask_bf16) + _dot(p2, mask_bf16)


def _dot_3x(a, b):
    a0 = a.astype(BF16)
    a1 = (a - a0.astype(F32)).astype(BF16)
    b0 = b.astype(BF16)
    b1 = (b - b0.astype(F32)).astype(BF16)
    return _dot(a0, b0) + _dot(a0, b1) + _dot(a1, b0)


def _mod_kernel(cond_ref, w_ref, b_ref, o_ref):
    o_ref[...] = _dot_hi(_silu(cond_ref[...]), w_ref[...]) + b_ref[...]


def _mod_all(cond8, mod_w, mod_b):
    tn = 1024
    out = pl.pallas_call(
        _mod_kernel,
        grid=(DEPTH, 6 * D // tn),
        in_specs=[pl.BlockSpec((8, D), lambda l, j: (0, 0)),
                  pl.BlockSpec((None, D, tn), lambda l, j: (l, 0, j)),
                  pl.BlockSpec((None, 1, tn), lambda l, j: (l, 0, j))],
        out_specs=pl.BlockSpec((None, 8, tn), lambda l, j: (l, 0, j)),
        out_shape=jax.ShapeDtypeStruct((DEPTH, 8, 6 * D), F32),
        compiler_params=_cparams(("arbitrary", "arbitrary")),
        name="mod_rows",
    )(cond8, mod_w, mod_b.reshape(DEPTH, 1, 6 * D))
    return out.reshape(DEPTH, 8, 6, D)


def _proj_kernel(x_ref, mod_ref, w_ref, o_ref, h_scr):
    @pl.when(pl.program_id(1) == 0)
    def _():
        h_scr[...] = (x_ref[...] * (1.0 + mod_ref[1:2, :]) + mod_ref[0:1, :]).astype(BF16)

    o_ref[...] = _dot(h_scr[...], w_ref[...].astype(BF16)).astype(o_ref.dtype)


def _proj(x, mod4, layer, w3, widx, ncols, out_dtype, tm=2048, tn=512):
    T = x.shape[0]
    return pl.pallas_call(
        _proj_kernel,
        grid=(T // tm, ncols // tn),
        in_specs=[pl.BlockSpec((tm, D), lambda i, j: (i, 0)),
                  pl.BlockSpec((None, None, 6, D), lambda i, j: (layer, (i * tm) // GROUP, 0, 0)),
                  pl.BlockSpec((None, D, tn), lambda i, j: (widx, 0, j))],
        out_specs=pl.BlockSpec((tm, tn), lambda i, j: (i, j)),
        out_shape=jax.ShapeDtypeStruct((T, ncols), out_dtype),
        scratch_shapes=[pltpu.VMEM((tm, D), BF16)],
        compiler_params=_cparams(("arbitrary", "arbitrary")),
        name="in_proj",
    )(x, mod4, w3)


def _proj_small_kernel(x_ref, mod_ref, w_ref, b_ref, o_ref):
    h = x_ref[...] * (1.0 + mod_ref[1:2, :]) + mod_ref[0:1, :]
    o_ref[...] = _dot_3x(h, w_ref[...]) + b_ref[...]


def _proj_small(x, mod4, layer, w, b, tm=1024):
    T, n = x.shape[0], w.shape[1]
    return pl.pallas_call(
        _proj_small_kernel,
        grid=(T // tm,),
        in_specs=[pl.BlockSpec((tm, D), lambda i: (i, 0)),
                  pl.BlockSpec((None, None, 6, D), lambda i: (layer, (i * tm) // GROUP, 0, 0)),
                  pl.BlockSpec((D, n), lambda i: (0, 0)),
                  pl.BlockSpec((1, n), lambda i: (0, 0))],
        out_specs=pl.BlockSpec((tm, n), lambda i: (i, 0)),
        out_shape=jax.ShapeDtypeStruct((T, n), F32),
        compiler_params=_cparams(("arbitrary",)),
        name="gate_proj",
    )(x, mod4, w, b.reshape(1, n))


def _layer_norm_rows(r, g, b):
    mu = jnp.mean(r, axis=-1, keepdims=True)
    c = r - mu
    var = jnp.mean(c * c, axis=-1, keepdims=True)
    return c * lax.rsqrt(var + EPS) * g + b


def _pack_bf16_pairs(lo, hi):
    lo_b = lax.bitcast_convert_type(lo.astype(BF16).astype(F32), I32)
    hi_b = lax.bitcast_convert_type(hi.astype(BF16).astype(F32), I32)
    return lax.shift_right_logical(lo_b, 16) | (hi_b & jnp.int32(-65536))


def _unpack_bf16_pairs(v):
    lo = lax.bitcast_convert_type(lax.shift_left(v, 16), F32)
    hi = lax.bitcast_convert_type(v & jnp.int32(-65536), F32)
    return lo.astype(BF16), hi.astype(BF16)


def _unpack_rows(ref, rows, stride, offset=0):
    chunks = []
    for r in range(4):
        if stride == 4:
            v = ref[pl.ds(r, rows, stride=4), :]
        else:
            v = ref[pl.ds(offset + r * stride, rows), :]
        chunks.extend(_unpack_bf16_pairs(v))
    return jnp.concatenate(chunks, axis=1)


def _post_kernel(ap_ref, as_ref, x_ref, mod_ref, w_ref, lng_ref, lnb_ref, rw_ref, rb_ref,
                 xo_ref, hp_ref, idx_ref, wt_ref, wb_scr, *, tm, n_ctx_tiles):
    @pl.when(pl.program_id(0) == 0)
    def _():
        wb_scr[...] = w_ref[...].astype(BF16)

    a = jnp.where(pl.program_id(0) < n_ctx_tiles, ap_ref[...], as_ref[...])
    y = _dot(a, wb_scr[...])
    r = ALPHA * x_ref[...] + mod_ref[2:3, :] * y
    xn = _layer_norm_rows(r, lng_ref[...], lnb_ref[...])
    xo_ref[...] = xn
    h1 = xn * (1.0 + mod_ref[4:5, :]) + mod_ref[3:4, :]
    for p in range(4):
        lo = h1[:, (2 * p) * LANES:(2 * p + 1) * LANES]
        hi = h1[:, (2 * p + 1) * LANES:(2 * p + 2) * LANES]
        hp_ref[pl.ds(p, tm, stride=4), :] = _pack_bf16_pairs(lo, hi)

    scores = _sigmoid(_dot_3x(h1, rw_ref[...]))
    sel = scores + rb_ref[...]
    e_iota = lax.broadcasted_iota(I32, (tm, N_EXP), 1).astype(F32)
    lane = lax.broadcasted_iota(I32, (tm, LANES), 1)
    idx_out = jnp.zeros((tm, LANES), F32)
    w_out = jnp.zeros((tm, LANES), F32)
    total = jnp.zeros((tm, 1), F32)
    for k in range(TOP_K):
        mx = jnp.max(sel, axis=-1, keepdims=True)
        ik = jnp.min(jnp.where(sel == mx, e_iota, float(N_EXP)), axis=-1, keepdims=True)
        hit = e_iota == ik
        wk = jnp.sum(jnp.where(hit, scores, 0.0), axis=-1, keepdims=True)
        sel = jnp.where(hit, -jnp.inf, sel)
        total = total + wk
        idx_out = jnp.where(lane == k, ik, idx_out)
        w_out = jnp.where(lane == k, wk, w_out)
    idx_ref[...] = idx_out.astype(I32)
    wt_ref[...] = w_out / total * ROUTE_SCALE


def _post(a_ctx, a_lat, x, mod4, layer, w_out, ln_g, ln_b, router_w, router_b, tm=512):
    T = x.shape[0]
    nc = a_ctx.shape[0] // tm
    return pl.pallas_call(
        functools.partial(_post_kernel, tm=tm, n_ctx_tiles=nc),
        grid=(T // tm,),
        in_specs=[pl.BlockSpec((tm, D), lambda i: (jnp.minimum(i, nc - 1), 0)),
                  pl.BlockSpec((tm, D), lambda i: (jnp.maximum(i - nc, 0), 0)),
                  pl.BlockSpec((tm, D), lambda i: (i, 0)),
                  pl.BlockSpec((None, None, 6, D), lambda i: (layer, (i * tm) // GROUP, 0, 0)),
                  pl.BlockSpec((D, D), lambda i: (0, 0)),
                  pl.BlockSpec((1, D), lambda i: (0, 0)),
                  pl.BlockSpec((1, D), lambda i: (0, 0)),
                  pl.BlockSpec((D, N_EXP), lambda i: (0, 0)),
                  pl.BlockSpec((1, N_EXP), lambda i: (0, 0))],
        out_specs=[pl.BlockSpec((tm, D), lambda i: (i, 0)),
                   pl.BlockSpec((tm * 4, LANES), lambda i: (i, 0)),
                   pl.BlockSpec((tm, LANES), lambda i: (i, 0)),
                   pl.BlockSpec((tm, LANES), lambda i: (i, 0))],
        out_shape=[jax.ShapeDtypeStruct((T, D), F32),
                   jax.ShapeDtypeStruct((T * 4, LANES), I32),
                   jax.ShapeDtypeStruct((T, LANES), I32),
                   jax.ShapeDtypeStruct((T, LANES), F32)],
        scratch_shapes=[pltpu.VMEM((D, D), BF16)],
        compiler_params=_cparams(("arbitrary",)),
        name="out_proj_ln_router",
    )(a_ctx, a_lat, x, mod4, w_out, ln_g.reshape(1, D), ln_b.reshape(1, D), router_w, router_b.reshape(1, N_EXP))


def _route_tables(idx128, w128):
    idx = idx128[:, :TOP_K].reshape(MOE_NS, MOE_G, TOP_K)
    w = w128[:, :TOP_K].reshape(MOE_NS, MOE_G, TOP_K)
    onehot = idx[..., None] == jnp.arange(N_EXP, dtype=I32)
    counts = jnp.sum(onehot.astype(I32), axis=(1, 2))
    padded = (counts + MOE_M - 1) // MOE_M * MOE_M
    pad_end = jnp.cumsum(padded, axis=-1)
    tok = jnp.arange(MOE_G, dtype=I32)[None, :, None]
    real_keys = (idx * MOE_KEY + tok).reshape(MOE_NS, MOE_G * TOP_K)
    fill = jnp.arange(MOE_M, dtype=I32)[None, None, :]
    e_ids = jnp.arange(N_EXP, dtype=I32)[None, :, None]
    fill_keys = jnp.where(fill < (padded - counts)[:, :, None], e_ids * MOE_KEY + MOE_KEY // 2 + fill,
                          N_EXP * MOE_KEY + e_ids * MOE_M + fill).reshape(MOE_NS, N_EXP * MOE_M)
    keys = jnp.concatenate([real_keys, fill_keys], axis=1)
    vals = jnp.concatenate([w.reshape(MOE_NS, MOE_G * TOP_K), jnp.zeros((MOE_NS, N_EXP * MOE_M), F32)], axis=1)
    sorted_rows = [lax.sort((keys[s], vals[s]), dimension=0, num_keys=1) for s in range(MOE_NS)]
    keys = jnp.stack([k for k, _ in sorted_rows])
    row_w = jnp.stack([v for _, v in sorted_rows])
    row_tok = jnp.where(keys < N_EXP * MOE_KEY, jnp.minimum(keys & (MOE_KEY - 1), MOE_G), MOE_G)
    row_tok = row_tok.reshape(-1)
    n_used = pad_end[:, -1] // MOE_M
    starts = jnp.arange(MOE_NB, dtype=I32) * MOE_M
    block_e = jnp.sum((starts[None, :, None] >= pad_end[:, None, :]).astype(I32), axis=-1)
    block_e = jnp.minimum(block_e, N_EXP - 1)
    last_e = jnp.take_along_axis(block_e, jnp.maximum(n_used - 1, 0)[:, None], axis=1)
    block_e = jnp.where(jnp.arange(MOE_NB, dtype=I32)[None, :] < n_used[:, None], block_e, last_e)
    return (block_e.reshape(-1).astype(I32), n_used.astype(I32), row_tok,
            row_w.reshape(MOE_NS * MOE_NB // MOE_TG, MOE_TG, 1, MOE_M))


def _cast_kernel(x_ref, o_ref):
    o_ref[...] = x_ref[...].astype(BF16)


def _cast_experts(w4, layer, eb=4):
    _, n_e, a, b = w4.shape
    return pl.pallas_call(
        _cast_kernel,
        grid=(n_e // eb,),
        in_specs=[pl.BlockSpec((None, eb, a, b), lambda i: (layer, i, 0, 0))],
        out_specs=pl.BlockSpec((eb, a, b), lambda i: (i, 0, 0)),
        out_shape=jax.ShapeDtypeStruct((n_e, a, b), BF16),
        compiler_params=_cparams(("arbitrary",)),
        name="cast_experts",
    )(w4)


def _moe_stage(rows, tg_ref, ts_ref, x_v, acc, rw_ref, wg_ref, wu_ref, wd_ref, tile_g, tile_c, ys_c, ys_s):
    rg, rc, rs = rows
    for mi in range(MOE_M):
        tile_g[pl.ds(mi, 4, stride=MOE_S), :] = x_v[pl.ds(pl.multiple_of(tg_ref[rg, mi], 4), 4), :]
    xb = _unpack_rows(tile_c, MOE_M, MOE_S)
    g = _dot(xb, wg_ref[...])
    u = _dot(xb, wu_ref[...])
    ri = lax.broadcasted_iota(I32, (MOE_M, MOE_M), 0)
    ci = lax.broadcasted_iota(I32, (MOE_M, MOE_M), 1)
    rw_col = jnp.sum(jnp.where(ri == ci, jnp.broadcast_to(rw_ref[rc], (MOE_M, MOE_M)), 0.0),
                     axis=1, keepdims=True)
    a = (_silu(g) * u) * rw_col
    y = _dot(a.astype(BF16), wd_ref[...])
    for c in range(D // LANES):
        ys_c[pl.ds(c * MOE_S, MOE_M), :] = y[:, c * LANES:(c + 1) * LANES]
    for m0 in range(0, MOE_M, MOE_U):
        offs = [pl.multiple_of(ts_ref[rs, m0 + j], 8) for j in range(MOE_U)]
        vals = [acc[pl.ds(offs[j], 8), :] + ys_s[pl.ds(m0 + j, 8, stride=MOE_S), :] for j in range(MOE_U)]
        for j in range(MOE_U):
            acc[pl.ds(offs[j], 8), :] = vals[j]


def _moe_kernel(be_ref, nu_ref, tg_ref, ts_ref, x_hbm, rw_ref, wg_ref, wu_ref, wd_ref, out_hbm,
                x_v, acc, tile_a, tile_b, ys_a, ys_b, sem):
    s = pl.program_id(0)
    j = pl.program_id(1)

    @pl.when(j == 0)
    def _():
        cp = pltpu.make_async_copy(x_hbm.at[s], x_v.at[pl.ds(0, MOE_G * 4)], sem.at[0])
        cp.start()
        cp.wait()
        x_v[pl.ds(MOE_G * 4, 8), :] = jnp.zeros((8, LANES), I32)

        def clear(i, carry):
            acc[pl.ds(pl.multiple_of(i * 8, 8), 8), :] = jnp.zeros((8, LANES), F32)
            return carry

        lax.fori_loop(0, MOE_G + 1, clear, 0)
        for t in (tile_a, tile_b):
            t[...] = jnp.zeros(t.shape, I32)
        for y in (ys_a, ys_b):
            y[...] = jnp.zeros(y.shape, F32)

    live = j < nu_ref[s] + 2
    if MOE_TG == 1:
        rows = (0, 0, 0)
    else:
        rows = tuple(jnp.clip(j - d, 0, MOE_NB - 1) % MOE_TG for d in range(3))
    common = (rows, tg_ref, ts_ref, x_v, acc, rw_ref, wg_ref, wu_ref, wd_ref)

    @pl.when(live & (j % 2 == 0))
    def _():
        _moe_stage(*common, tile_a, tile_b, ys_b, ys_a)

    @pl.when(live & (j % 2 == 1))
    def _():
        _moe_stage(*common, tile_b, tile_a, ys_a, ys_b)

    @pl.when(j == MOE_NB + 1)
    def _():
        cp = pltpu.make_async_copy(acc.at[pl.ds(0, MOE_G * 8)], out_hbm.at[s], sem.at[1])
        cp.start()
        cp.wait()


def _moe_routed(hp, tables, layer, w_gate, w_up, w_down):
    block_e, n_used, row_tok, row_w = tables
    x3 = hp.reshape(MOE_NS, MOE_G * 4, LANES)
    blk = lambda s, j, d: s * MOE_NB + jnp.clip(j - d, 0, MOE_NB - 1)
    wspec = lambda shape: pl.BlockSpec((None,) + shape, lambda s, j, be, nu: (be[blk(s, j, 1)], 0, 0))
    tspec = lambda d: pl.BlockSpec((None, MOE_TG, MOE_M), lambda s, j, be, nu: (blk(s, j, d) // MOE_TG, 0, 0),
                                   memory_space=pltpu.SMEM)
    tok3 = row_tok.reshape(MOE_NS * MOE_NB // MOE_TG, MOE_TG, MOE_M)
    out = pl.pallas_call(
        _moe_kernel,
        grid_spec=pltpu.PrefetchScalarGridSpec(
            num_scalar_prefetch=2,
            grid=(MOE_NS, MOE_NB + 2),
            in_specs=[tspec(0), tspec(2),
                      pl.BlockSpec(memory_space=pl.ANY),
                      pl.BlockSpec((None, MOE_TG, 1, MOE_M), lambda s, j, be, nu: (blk(s, j, 1) // MOE_TG, 0, 0, 0)),
                      wspec((D, D_EXP)), wspec((D, D_EXP)), wspec((D_EXP, D))],
            out_specs=pl.BlockSpec(memory_space=pl.ANY),
            scratch_shapes=[pltpu.VMEM((MOE_G * 4 + 8, LANES), I32),
                            pltpu.VMEM(((MOE_G + 1) * 8, LANES), F32),
                            pltpu.VMEM((4 * MOE_S, LANES), I32),
                            pltpu.VMEM((4 * MOE_S, LANES), I32),
                            pltpu.VMEM((8 * MOE_S, LANES), F32),
                            pltpu.VMEM((8 * MOE_S, LANES), F32),
                            pltpu.SemaphoreType.DMA((2,))]),
        out_shape=jax.ShapeDtypeStruct((MOE_NS, MOE_G * 8, LANES), F32),
        compiler_params=_cparams(("arbitrary", "arbitrary")),
        name="moe_routed",
    )(block_e, n_used, tok3 * 4, tok3 * 8, x3, row_w,
      _cast_experts(w_gate, layer), _cast_experts(w_up, layer), _cast_experts(w_down, layer))
    return out.reshape(T_ALL * 8, LANES)


def _moe_post_kernel(x_ref, hp_ref, r_ref, mod_ref, sg_ref, su_ref, sd_ref, lng_ref, lnb_ref, o_ref,
                     sgb, sub, sdb, *, tm):
    @pl.when(pl.program_id(0) == 0)
    def _():
        sgb[...] = sg_ref[...].astype(BF16)
        sub[...] = su_ref[...].astype(BF16)
        sdb[...] = sd_ref[...].astype(BF16)

    hb = _unpack_rows(hp_ref, tm, 4)
    g = _dot(hb, sgb[...])
    u = _dot(hb, sub[...])
    sh = _dot((_silu(g) * u).astype(BF16), sdb[...])
    routed = jnp.concatenate([r_ref[pl.ds(c, tm, stride=8), :] for c in range(D // LANES)], axis=1)
    r = ALPHA * x_ref[...] + mod_ref[5:6, :] * (routed + sh)
    o_ref[...] = _layer_norm_rows(r, lng_ref[...], lnb_ref[...])


def _moe_post(x, hp, routed, mod4, layer, sg, su, sd, ln_g, ln_b, tm=512):
    T = x.shape[0]
    return pl.pallas_call(
        functools.partial(_moe_post_kernel, tm=tm),
        grid=(T // tm,),
        in_specs=[pl.BlockSpec((tm, D), lambda i: (i, 0)),
                  pl.BlockSpec((tm * 4, LANES), lambda i: (i, 0)),
                  pl.BlockSpec((tm * 8, LANES), lambda i: (i, 0)),
                  pl.BlockSpec((None, None, 6, D), lambda i: (layer, (i * tm) // GROUP, 0, 0)),
                  pl.BlockSpec((D, D_SH), lambda i: (0, 0)),
                  pl.BlockSpec((D, D_SH), lambda i: (0, 0)),
                  pl.BlockSpec((D_SH, D), lambda i: (0, 0)),
                  pl.BlockSpec((1, D), lambda i: (0, 0)),
                  pl.BlockSpec((1, D), lambda i: (0, 0))],
        out_specs=pl.BlockSpec((tm, D), lambda i: (i, 0)),
        out_shape=jax.ShapeDtypeStruct((T, D), F32),
        scratch_shapes=[pltpu.VMEM((D, D_SH), BF16), pltpu.VMEM((D, D_SH), BF16), pltpu.VMEM((D_SH, D), BF16)],
        compiler_params=_cparams(("arbitrary",)),
        name="shared_expert_ln",
    )(x, hp, routed, mod4, sg, su, sd, ln_g.reshape(1, D), ln_b.reshape(1, D))


def _conv_silu(src_ref, w_ref, dst_ref, T, scale):
    L = ML_L
    nc = T // L
    width = src_ref.shape[1]
    w0, w1, w2 = w_ref[0:1, :], w_ref[1:2, :], w_ref[2:3, :]
    row = lax.broadcasted_iota(I32, (L, width), 0)

    def body(c, carry):
        r0 = pl.multiple_of(c * L, L)
        cur = src_ref[pl.ds(r0, L), :].astype(F32)
        p0 = pl.multiple_of(jnp.maximum(r0 - 16, 0), 16)
        n0 = pl.multiple_of(jnp.minimum(r0 + L, T - 16), 16)
        prev_row = src_ref[pl.ds(p0, 16), :].astype(F32)[15:16, :] * jnp.where(c > 0, 1.0, 0.0).astype(F32)
        next_row = src_ref[pl.ds(n0, 16), :].astype(F32)[0:1, :] * jnp.where(c < nc - 1, 1.0, 0.0).astype(F32)
        prev = jnp.where(row == 0, prev_row, pltpu.roll(cur, 1, axis=0))
        nxt = jnp.where(row == L - 1, next_row, pltpu.roll(cur, L - 1, axis=0))
        dst_ref[pl.ds(r0, L), :] = _silu(w0 * prev + w1 * cur + w2 * nxt) * scale
        return carry

    lax.fori_loop(0, nc, body, 0)


def _mlstm_chunk(q, k, v, G, GT, C, n, m, backward):
    L = ML_L
    ri = lax.broadcasted_iota(I32, (L, L), 0)
    ci = lax.broadcasted_iota(I32, (L, L), 1)
    keep = (ci >= ri) if backward else (ci <= ri)
    A = keep.astype(BF16)
    AT = ((ri >= ci) if backward else (ri <= ci)).astype(BF16)
    ic, fc = (2, 3) if backward else (0, 1)
    last = 0 if backward else L - 1
    b_col = _mask_dot(A, _log_sigmoid(G))[:, fc:fc + 1]
    b_row = _dot_mask(_log_sigmoid(GT), AT)[fc:fc + 1, :]
    ig_col = G[:, ic:ic + 1]
    ig_row = GT[ic:ic + 1, :]
    log_d = jnp.where(keep, b_col - b_row + ig_row, -jnp.inf)
    log_inter = b_col + m
    m_t = jnp.maximum(log_inter, jnp.max(log_d, axis=-1, keepdims=True))
    d = jnp.exp(log_d - m_t)
    w_inter = jnp.exp(log_inter - m_t)
    qb, kb, vb = q.astype(BF16), k.astype(BF16), v.astype(BF16)
    s = _dot_nt(qb, kb) * d
    num = _dot(s.astype(BF16), vb) + w_inter * _dot(qb, C.astype(BF16))
    den = jnp.sum(s, axis=-1, keepdims=True) + w_inter * jnp.sum(q * n, axis=-1, keepdims=True)
    h = num / jnp.maximum(jnp.abs(den), jnp.exp(-m_t))
    m_new = m_t[last:last + 1, :]
    w_last = jnp.exp(b_col[last:last + 1, :] - b_col + ig_col - m_new)
    decay = w_inter[last:last + 1, :]
    kw = k * w_last
    C_new = decay * C + _dot_tn(kw.astype(BF16), vb)
    n_new = decay * n + jnp.sum(kw, axis=0, keepdims=True)
    return h, C_new, n_new, m_new


def _mlstm_kernel(q_ref, k_ref, v_ref, og_ref, g_ref, gt_ref, cq_ref, ck_ref, ng_ref, c0_ref, n0_ref, m0_ref,
                  a_ref, c_out, n_out, m_out, qs, ks, hf, hb, cst, nst, mst, *, T, nh):
    L = ML_L
    nc = T // L
    _conv_silu(q_ref, cq_ref, qs, T, ML_DK ** -0.5)
    _conv_silu(k_ref, ck_ref, ks, T, 1.0)
    cst[...] = c0_ref[...]
    nst[...] = n0_ref[...]
    mst[...] = m0_ref[...]
    un = min(SCAN_UNROLL, nc)

    def body(i, carry):
        for h in range(nh):
            kcols = slice(h * ML_DK, (h + 1) * ML_DK)
            vcols = slice(h * ML_DV, (h + 1) * ML_DV)
            for direction, out in ((0, hf), (1, hb)):
                C, n, m = cst[direction, h], nst[direction, h], mst[direction, h]
                for u in range(un):
                    ci = i * un + u
                    c = (nc - 1 - ci) if direction else ci
                    r0 = pl.multiple_of(c * L, L)
                    hh, C, n, m = _mlstm_chunk(
                        qs[pl.ds(r0, L), kcols], ks[pl.ds(r0, L), kcols], v_ref[pl.ds(r0, L), vcols],
                        g_ref[h, pl.ds(r0, L), :], gt_ref[h, c], C, n, m, backward=bool(direction))
                    out[pl.ds(r0, L), vcols] = hh
                cst[direction, h] = C
                nst[direction, h] = n
                mst[direction, h] = m
        return carry

    lax.fori_loop(0, nc // un, body, 0)
    c_out[...] = cst[...]
    n_out[...] = nst[...]
    m_out[...] = mst[...]

    def finish(c, carry):
        r0 = pl.multiple_of(c * L, L)
        for h in range(nh):
            vcols = slice(h * ML_DV, (h + 1) * ML_DV)
            tot = hf[pl.ds(r0, L), vcols] + hb[pl.ds(r0, L), vcols]
            mu = jnp.mean(tot, axis=-1, keepdims=True)
            cen = tot - mu
            var = jnp.mean(cen * cen, axis=-1, keepdims=True)
            hn = cen * lax.rsqrt(var + EPS) * ng_ref[:, vcols]
            a_ref[pl.ds(r0, L), vcols] = (hn * _sigmoid(og_ref[pl.ds(r0, L), vcols].astype(F32))).astype(BF16)
        return carry

    lax.fori_loop(0, nc, finish, 0)


def _mlstm_scan(z, gh, ght, conv_w, norm_g, C0, n0, m0, T, row_blk0, nseq, nh):
    nc = T // ML_L
    kw, vw = nh * ML_DK, nh * ML_DV
    qcol, kcol = 0, ML_QK // kw
    vcol, ocol = 2 * ML_QK // vw, (2 * ML_QK + ML_V) // vw
    rb = lambda s: row_blk0 + s
    state = lambda *tail: pl.BlockSpec((None, 2, nh) + tail, lambda s, h: (s, 0, h) + (0,) * len(tail))
    return pl.pallas_call(
        functools.partial(_mlstm_kernel, T=T, nh=nh),
        grid=(nseq, ML_H // nh),
        in_specs=[pl.BlockSpec((T, kw), lambda s, h: (rb(s), qcol + h)),
                  pl.BlockSpec((T, kw), lambda s, h: (rb(s), kcol + h)),
                  pl.BlockSpec((T, vw), lambda s, h: (rb(s), vcol + h)),
                  pl.BlockSpec((T, vw), lambda s, h: (rb(s), ocol + h)),
                  pl.BlockSpec((nh, T, 4), lambda s, h: (h, rb(s), 0)),
                  pl.BlockSpec((nh, nc, 4, ML_L), lambda s, h: (h, rb(s), 0, 0)),
                  pl.BlockSpec((3, kw), lambda s, h: (0, qcol + h)),
                  pl.BlockSpec((3, kw), lambda s, h: (0, kcol + h)),
                  pl.BlockSpec((1, vw), lambda s, h: (0, h)),
                  state(ML_DK, ML_DV), state(1, ML_DK), state(1, 1)],
        out_specs=[pl.BlockSpec((T, vw), lambda s, h: (s, h)),
                   state(ML_DK, ML_DV), state(1, ML_DK), state(1, 1)],
        out_shape=[jax.ShapeDtypeStruct((nseq * T, ML_V), BF16),
                   jax.ShapeDtypeStruct((nseq, 2, ML_H, ML_DK, ML_DV), F32),
                   jax.ShapeDtypeStruct((nseq, 2, ML_H, 1, ML_DK), F32),
                   jax.ShapeDtypeStruct((nseq, 2, ML_H, 1, 1), F32)],
        scratch_shapes=[pltpu.VMEM((T, kw), F32), pltpu.VMEM((T, kw), F32),
                        pltpu.VMEM((T, vw), F32), pltpu.VMEM((T, vw), F32),
                        pltpu.VMEM((2, nh, ML_DK, ML_DV), F32), pltpu.VMEM((2, nh, 1, ML_DK), F32),
                        pltpu.VMEM((2, nh, 1, 1), F32)],
        compiler_params=_cparams(("arbitrary", "arbitrary")),
        name="mlstm_scan",
    )(z, z, z, z, gh, ght, conv_w, conv_w, norm_g.reshape(1, ML_V), C0, n0, m0)


def _mlstm_layer(x, mod4, layer, j, w_in, gate_b, conv_w, norm_g, C_lat, n_lat, m_lat):
    z = _proj(x, mod4, layer, w_in, j, 2 * ML_QK + 2 * ML_V, BF16)
    gates = _proj_small(x, mod4, layer, w_in[j][:, 2 * ML_QK + 2 * ML_V:], gate_b[j])
    gh = gates.reshape(T_ALL, 4, ML_H).transpose(2, 0, 1)
    ght = gh.reshape(ML_H, T_ALL // ML_L, ML_L, 4).transpose(0, 1, 3, 2)
    zC = jnp.zeros((BATCH, 2, ML_H, ML_DK, ML_DV), F32)
    zn = jnp.zeros((BATCH, 2, ML_H, 1, ML_DK), F32)
    zm = jnp.zeros((BATCH, 2, ML_H, 1, 1), F32)
    a_p, Cn, nn, mn = _mlstm_scan(z, gh, ght, conv_w[j], norm_g[j], zC, zn, zm, SEQ, 0, BATCH, ML_H)
    a_s, _, _, _ = _mlstm_scan(z, gh, ght, conv_w[j], norm_g[j], C_lat[:, j],
                               n_lat[:, j].reshape(DEC_BATCH, 2, ML_H, 1, ML_DK),
                               m_lat[:, j].reshape(DEC_BATCH, 2, ML_H, 1, 1),
                               DEC_SEQ, BATCH * SEQ // DEC_SEQ, DEC_BATCH, 1)
    return (a_p, a_s), (Cn, nn.reshape(BATCH, 2, ML_H, ML_DK), mn.reshape(BATCH, 2, ML_H))


def _hgrn_chunk(q, k, v, g, St, backward):
    L = HG_L
    ri = lax.broadcasted_iota(I32, (L, L), 0)
    ci = lax.broadcasted_iota(I32, (L, L), 1)
    keep = (ci >= ri) if backward else (ci <= ri)
    ref = L - 1 - L // 2 if backward else L // 2
    last = 0 if backward else L - 1
    b = _mask_dot(keep.astype(BF16), g)
    b_ref = b[ref:ref + 1, :]
    b_last = b[last:last + 1, :]
    qe = (q * jnp.exp(b - b_ref)).astype(BF16)
    ke = (k * jnp.exp(b_ref - b)).astype(BF16)
    vb = v.astype(BF16)
    a = jnp.where(keep, _dot_nt(qe, ke), 0.0)
    o = _dot(a.astype(BF16), vb) + _dot_nt((q * jnp.exp(b)).astype(BF16), St.astype(BF16))
    kd = (k * jnp.exp(b_last - b)).astype(BF16)
    St_new = jnp.exp(b_last) * St + _dot_tn(vb, kd)
    return o, St_new


def _hgrn_kernel(q_ref, i_ref, ff_ref, fb_ref, og_ref, fbf_ref, fbb_ref, lbr_ref, ng_ref, s0_ref,
                 a_ref, s_out, of, ob, st, *, T, lb_layer, nh):
    L = HG_L
    nc = T // L
    raw = lbr_ref[...]
    e = jnp.exp(raw - jnp.max(raw, axis=0, keepdims=True))
    p = e / jnp.sum(e, axis=0, keepdims=True)
    lb_all = jnp.sum(p[0:lb_layer + 1, :], axis=0, keepdims=True) - p[0:1, :]
    for h in range(nh):
        st[0, h] = s0_ref[0, h].T
        st[1, h] = s0_ref[1, h].T
    un = min(SCAN_UNROLL, nc)

    def body(i, carry):
        for h in range(nh):
            cols = slice(h * HG_DK, (h + 1) * HG_DK)
            lb = lb_all[:, cols]
            for direction, out, f_ref, b_ref in ((0, of, ff_ref, fbf_ref), (1, ob, fb_ref, fbb_ref)):
                St = st[direction, h]
                for u in range(un):
                    ci = i * un + u
                    c = (nc - 1 - ci) if direction else ci
                    r0 = pl.multiple_of(c * L, L)
                    f = lb + (1.0 - lb) * _sigmoid(f_ref[pl.ds(r0, L), cols] + b_ref[:, cols])
                    o, St = _hgrn_chunk(_silu(q_ref[pl.ds(r0, L), cols]), 1.0 - f, i_ref[pl.ds(r0, L), cols],
                                        jnp.log(f), St, backward=bool(direction))
                    out[pl.ds(r0, L), cols] = o
                st[direction, h] = St
        return carry

    lax.fori_loop(0, nc // un, body, 0)
    for h in range(nh):
        s_out[0, h] = st[0, h].T
        s_out[1, h] = st[1, h].T

    def finish(c, carry):
        r0 = pl.multiple_of(c * L, L)
        for h in range(nh):
            cols = slice(h * HG_DV, (h + 1) * HG_DV)
            tot = of[pl.ds(r0, L), cols] + ob[pl.ds(r0, L), cols]
            on = tot * lax.rsqrt(jnp.mean(tot * tot, axis=-1, keepdims=True) + EPS) * ng_ref[:, cols]
            a_ref[pl.ds(r0, L), cols] = (on * _silu(og_ref[pl.ds(r0, L), cols])).astype(BF16)
        return carry

    lax.fori_loop(0, nc, finish, 0)


def _hgrn_scan(z, f_b, lb_raw, norm_g, S0, T, row_blk0, nseq, lb_layer, nh):
    nhb = HG_H // nh
    w = nh * HG_DK
    rb = lambda s: row_blk0 + s
    zspec = lambda cb: pl.BlockSpec((T, w), lambda s, h: (rb(s), cb * nhb + h))
    sspec = pl.BlockSpec((None, 2, nh, HG_DK, HG_DV), lambda s, h: (s, 0, h, 0, 0))
    return pl.pallas_call(
        functools.partial(_hgrn_kernel, T=T, lb_layer=lb_layer, nh=nh),
        grid=(nseq, nhb),
        in_specs=[zspec(0), zspec(1), zspec(2), zspec(3), zspec(4),
                  pl.BlockSpec((1, w), lambda s, h: (0, h)),
                  pl.BlockSpec((1, w), lambda s, h: (0, nhb + h)),
                  pl.BlockSpec((DEPTH, w), lambda s, h: (0, h)),
                  pl.BlockSpec((1, w), lambda s, h: (0, h)),
                  sspec],
        out_specs=[pl.BlockSpec((T, w), lambda s, h: (s, h)), sspec],
        out_shape=[jax.ShapeDtypeStruct((nseq * T, HG_V), BF16),
                   jax.ShapeDtypeStruct((nseq, 2, HG_H, HG_DK, HG_DV), F32)],
        scratch_shapes=[pltpu.VMEM((T, w), F32), pltpu.VMEM((T, w), F32),
                        pltpu.VMEM((2, nh, HG_DV, HG_DK), F32)],
        compiler_params=_cparams(("arbitrary", "arbitrary")),
        name="hgrn_scan",
    )(z, z, z, z, z, f_b.reshape(1, 2 * HG_K), f_b.reshape(1, 2 * HG_K), lb_raw, norm_g.reshape(1, HG_V), S0)


def _hgrn_layer(x, mod4, layer, j, w_in, f_b, lb_raw, norm_g, S_lat):
    z = _proj(x, mod4, layer, w_in, j, 3 * HG_K + 2 * HG_V, F32)
    zS = jnp.zeros((BATCH, 2, HG_H, HG_DK, HG_DV), F32)
    a_p, Sn = _hgrn_scan(z, f_b[j], lb_raw, norm_g[j], zS, SEQ, 0, BATCH, layer, 4)
    a_s, _ = _hgrn_scan(z, f_b[j], lb_raw, norm_g[j], S_lat[:, j], DEC_SEQ, BATCH * SEQ // DEC_SEQ, DEC_BATCH,
                        layer, 1)
    return (a_p, a_s), Sn


def _head_rms(x, g_tiled, nheads):
    lane = lax.broadcasted_iota(I32, x.shape, 1)
    sq = x * x
    ms = jnp.zeros_like(x)
    for h in range(nheads):
        in_h = (lane >= h * AT_HD) & (lane < (h + 1) * AT_HD)
        tot = jnp.sum(jnp.where(in_h, sq, 0.0), axis=-1, keepdims=True)
        ms = jnp.where(in_h, tot, ms)
    return x * lax.rsqrt(ms * (1.0 / AT_HD) + EPS) * g_tiled


def _rope(x, cos, sin):
    w = x.shape[1]
    lane = lax.broadcasted_iota(I32, x.shape, 1)
    up = pltpu.roll(x, w - 16, axis=1)
    down = pltpu.roll(x, 16, axis=1)
    swapped = jnp.where((lane % 32) < 16, up, down)
    return x * cos + swapped * sin


def _attn_ctx_kernel(q_ref, k_ref, v_ref, qn_ref, kn_ref, o_ref, ko_ref, vo_ref):
    k = _head_rms(k_ref[...].astype(F32), kn_ref[...], AT_KV)
    ko_ref[...] = k
    v = v_ref[...].astype(F32)
    vo_ref[...] = v
    kb, vb = k.astype(BF16), v.astype(BF16)
    outs = []
    for kv in range(AT_KV):
        q = _head_rms(q_ref[:, kv * 256:(kv + 1) * 256].astype(F32), qn_ref[...], AT_G) * (AT_HD ** -0.5)
        kh = kb[:, kv * AT_HD:(kv + 1) * AT_HD]
        vh = vb[:, kv * AT_HD:(kv + 1) * AT_HD]
        for g in range(AT_G):
            s = _dot_nt(q[:, g * AT_HD:(g + 1) * AT_HD].astype(BF16), kh)
            p = jnp.exp(s - jnp.max(s, axis=-1, keepdims=True))
            p = p / jnp.sum(p, axis=-1, keepdims=True)
            outs.append(_dot(p.astype(BF16), vh))
    o_ref[...] = jnp.concatenate(outs, axis=1).astype(BF16)


def _attn_ctx(z, q_norm, k_norm):
    qn = jnp.tile(q_norm, AT_G).reshape(1, 256)
    kn = jnp.tile(k_norm, AT_KV).reshape(1, 256)
    return pl.pallas_call(
        _attn_ctx_kernel,
        grid=(BATCH,),
        in_specs=[pl.BlockSpec((SEQ, D), lambda b: (b, 0)),
                  pl.BlockSpec((SEQ, 256), lambda b: (b, 4)),
                  pl.BlockSpec((SEQ, 256), lambda b: (b, 5)),
                  pl.BlockSpec((1, 256), lambda b: (0, 0)),
                  pl.BlockSpec((1, 256), lambda b: (0, 0))],
        out_specs=[pl.BlockSpec((SEQ, D), lambda b: (b, 0)),
                   pl.BlockSpec((SEQ, 256), lambda b: (b, 0)),
                   pl.BlockSpec((SEQ, 256), lambda b: (b, 0))],
        out_shape=[jax.ShapeDtypeStruct((BATCH * SEQ, D), BF16),
                   jax.ShapeDtypeStruct((BATCH * SEQ, 256), F32),
                   jax.ShapeDtypeStruct((BATCH * SEQ, 256), F32)],
        compiler_params=_cparams(("arbitrary",)),
        name="attn_context",
    )(z, z, z, qn, kn)


def _attn_kv_kernel(k_ref, v_ref, kn_ref, cos_ref, sin_ref, ko_ref, vo_ref):
    k = _rope(_head_rms(k_ref[...].astype(F32), kn_ref[...], AT_KV), cos_ref[...], sin_ref[...])
    v = v_ref[...]
    for h in range(AT_KV):
        ko_ref[h] = k[:, h * AT_HD:(h + 1) * AT_HD].astype(BF16)
        vo_ref[h] = v[:, h * AT_HD:(h + 1) * AT_HD].astype(BF16)


def _attn_kv(z, k_norm, cos4, sin4, tt=512):
    kn = jnp.tile(k_norm, AT_KV).reshape(1, 256)
    nt = DEC_SEQ // tt
    row0 = BATCH * SEQ // tt
    return pl.pallas_call(
        _attn_kv_kernel,
        grid=(DEC_BATCH, nt),
        in_specs=[pl.BlockSpec((tt, 256), lambda b, i: (row0 + b * nt + i, 4)),
                  pl.BlockSpec((tt, 256), lambda b, i: (row0 + b * nt + i, 5)),
                  pl.BlockSpec((1, 256), lambda b, i: (0, 0)),
                  pl.BlockSpec((tt, 256), lambda b, i: (i, 0)),
                  pl.BlockSpec((tt, 256), lambda b, i: (i, 0))],
        out_specs=[pl.BlockSpec((None, AT_KV, tt, AT_HD), lambda b, i: (b, 0, i, 0)),
                   pl.BlockSpec((None, AT_KV, tt, AT_HD), lambda b, i: (b, 0, i, 0))],
        out_shape=[jax.ShapeDtypeStruct((DEC_BATCH, AT_KV, DEC_SEQ, AT_HD), BF16),
                   jax.ShapeDtypeStruct((DEC_BATCH, AT_KV, DEC_SEQ, AT_HD), BF16)],
        compiler_params=_cparams(("arbitrary", "arbitrary")),
        name="attn_kv_prep",
    )(z, z, kn, cos4, sin4)


def _attn_lat_kernel(q_ref, qn_ref, cos_ref, sin_ref, k_ref, v_ref, o_ref, *, tq, tk):
    q = _rope(_head_rms(q_ref[...].astype(F32), qn_ref[...], AT_G), cos_ref[...], sin_ref[...]) * (AT_HD ** -0.5)
    qs = jnp.concatenate([q[:, g * AT_HD:(g + 1) * AT_HD] for g in range(AT_G)], axis=0).astype(BF16)
    rows = AT_G * tq
    nk = k_ref.shape[0] // tk

    def body(j, carry):
        m, l, acc = carry
        k0 = pl.multiple_of(j * tk, tk)
        s = _dot_nt(qs, k_ref[pl.ds(k0, tk), :])
        m_new = jnp.maximum(m, jnp.max(s, axis=-1, keepdims=True))
        alpha = jnp.exp(m - m_new)
        p = jnp.exp(s - m_new)
        l = alpha * l + jnp.sum(p, axis=-1, keepdims=True)
        acc = alpha * acc + _dot(p.astype(BF16), v_ref[pl.ds(k0, tk), :])
        return m_new, l, acc

    init = (jnp.full((rows, 1), -jnp.inf, F32), jnp.zeros((rows, 1), F32), jnp.zeros((rows, AT_HD), F32))
    _, l, acc = lax.fori_loop(0, nk, body, init)
    o = acc / l
    o_ref[...] = jnp.concatenate([o[g * tq:(g + 1) * tq, :] for g in range(AT_G)], axis=1).astype(BF16)


def _attn_lat(z, q_norm, cos4, sin4, kk, vv, tq=256, tk=1536):
    qn = jnp.tile(q_norm, AT_G).reshape(1, 256)
    nq = DEC_SEQ // tq
    row0 = BATCH * SEQ // tq
    skv = kk.shape[2]
    return pl.pallas_call(
        functools.partial(_attn_lat_kernel, tq=tq, tk=tk),
        grid=(DEC_BATCH, AT_KV, nq),
        in_specs=[pl.BlockSpec((tq, 256), lambda b, h, i: (row0 + b * nq + i, h)),
                  pl.BlockSpec((1, 256), lambda b, h, i: (0, 0)),
                  pl.BlockSpec((tq, 256), lambda b, h, i: (i, 0)),
                  pl.BlockSpec((tq, 256), lambda b, h, i: (i, 0)),
                  pl.BlockSpec((None, None, skv, AT_HD), lambda b, h, i: (b, h, 0, 0)),
                  pl.BlockSpec((None, None, skv, AT_HD), lambda b, h, i: (b, h, 0, 0))],
        out_specs=pl.BlockSpec((tq, 256), lambda b, h, i: (b * nq + i, h)),
        out_shape=jax.ShapeDtypeStruct((DEC_BATCH * DEC_SEQ, D), BF16),
        compiler_params=_cparams(("arbitrary", "arbitrary", "arbitrary")),
        name="attn_latent",
    )(z, qn, cos4, sin4, kk, vv)


def _rope_tables():
    t = jnp.arange(DEC_SEQ)
    row = (t // GRID_W).astype(F32)
    col = (t % GRID_W).astype(F32)
    nf = AT_HD // 4
    inv = ROPE_THETA ** (-jnp.arange(nf, dtype=F32) / nf)
    ar, ac = row[:, None] * inv[None], col[:, None] * inv[None]
    cos = jnp.concatenate([jnp.cos(ar), jnp.cos(ar), jnp.cos(ac), jnp.cos(ac)], axis=1)
    sin = jnp.concatenate([-jnp.sin(ar), jnp.sin(ar), -jnp.sin(ac), jnp.sin(ac)], axis=1)
    return jnp.tile(cos, (1, 4)), jnp.tile(sin, (1, 4))


def _attn_layer(x, mod4, layer, j, w_in, q_norm, k_norm, cache_k, cache_v):
    z = _proj(x, mod4, layer, w_in, j, (AT_H + 2 * AT_KV) * AT_HD, BF16)
    a_p, k_new, v_new = _attn_ctx(z, q_norm[j], k_norm[j])
    cos4, sin4 = _rope_tables()
    k_lat, v_lat = _attn_kv(z, k_norm[j], cos4, sin4)
    kk = jnp.concatenate([cache_k[:, j].transpose(0, 2, 1, 3).astype(BF16), k_lat], axis=2)
    vv = jnp.concatenate([cache_v[:, j].transpose(0, 2, 1, 3).astype(BF16), v_lat], axis=2)
    a_s = _attn_lat(z, q_norm[j], cos4, sin4, kk, vv)
    return (a_p, a_s), (k_new.reshape(BATCH, SEQ, AT_KV, AT_HD), v_new.reshape(BATCH, SEQ, AT_KV, AT_HD))


def kernel(x_prompt, x_sample, state_mlstm_C, state_mlstm_n, state_mlstm_m, cache_attn_k, cache_attn_v, state_hgrn_S, c, c_ctx, mod_w, mod_b, ln_g, ln_b, mlstm_w_in, mlstm_gate_b, mlstm_conv, mlstm_norm, mlstm_w_out, attn_w_in, attn_q_norm, attn_k_norm, attn_w_out, hgrn_w_in, hgrn_f_b, hgrn_lower_bounds, hgrn_norm, hgrn_w_out, moe_router, moe_router_b, moe_w_gate, moe_w_up, moe_w_down, moe_sh_gate, moe_sh_up, moe_sh_down):
    x = jnp.concatenate([x_prompt.reshape(BATCH * SEQ, D), x_sample.reshape(DEC_BATCH * DEC_SEQ, D)], axis=0)
    cond8 = jnp.concatenate([c_ctx[None], c, jnp.zeros((8 - 1 - DEC_BATCH, D), F32)], axis=0)
    mod4 = _mod_all(cond8, mod_w, mod_b)

    new_C, new_n, new_m, new_k, new_v, new_S = [], [], [], [], [], []
    for l in range(DEPTH):
        kind, j = l % 3, l // 3
        if kind == 0:
            a, (Cn, nn, mn) = _mlstm_layer(x, mod4, l, j, mlstm_w_in, mlstm_gate_b, mlstm_conv, mlstm_norm,
                                           state_mlstm_C, state_mlstm_n, state_mlstm_m)
            new_C.append(Cn)
            new_n.append(nn)
            new_m.append(mn)
            w_out = mlstm_w_out[j]
        elif kind == 1:
            a, (kn, vn) = _attn_layer(x, mod4, l, j, attn_w_in, attn_q_norm, attn_k_norm, cache_attn_k, cache_attn_v)
            new_k.append(kn)
            new_v.append(vn)
            w_out = attn_w_out[j]
        else:
            a, Sn = _hgrn_layer(x, mod4, l, j, hgrn_w_in, hgrn_f_b, hgrn_lower_bounds, hgrn_norm, state_hgrn_S)
            new_S.append(Sn)
            w_out = hgrn_w_out[j]
        x, hp, idx128, w128 = _post(a[0], a[1], x, mod4, l, w_out, ln_g[l, 0], ln_b[l, 0],
                                    moe_router[l], moe_router_b[l])
        routed = _moe_routed(hp, _route_tables(idx128, w128), l, moe_w_gate, moe_w_up, moe_w_down)
        x = _moe_post(x, hp, routed, mod4, l, moe_sh_gate[l], moe_sh_up[l], moe_sh_down[l], ln_g[l, 1], ln_b[l, 1])

    xp = x[:BATCH * SEQ].reshape(BATCH, SEQ, D)
    xs = x[BATCH * SEQ:].reshape(DEC_BATCH, DEC_SEQ, D)
    return (xp, xs, jnp.stack(new_C, 1), jnp.stack(new_n, 1), jnp.stack(new_m, 1),
            jnp.stack(new_k, 1), jnp.stack(new_v, 1), jnp.stack(new_S, 1))
```

```python
import functools

import jax
import jax.numpy as jnp
from jax import lax
from jax.experimental import pallas as pl
from jax.experimental.pallas import tpu as pltpu

F32 = jnp.float32
BF16 = jnp.bfloat16
I32 = jnp.int32
HI = lax.Precision.HIGHEST

D = 1024
BATCH, SEQ = 16, 256
DEPTH = 4
DEC_BATCH, DEC_SEQ = 2, 4096
PAST = 512
GRID_W = 64
GROUP = 4096
N_GROUPS = 3
T_ALL = N_GROUPS * GROUP

ML_H, ML_DK, ML_DV, ML_L = 4, 128, 256, 128
ML_QK, ML_V = ML_H * ML_DK, ML_H * ML_DV
AT_H, AT_KV, AT_HD, AT_G = 16, 4, 64, 4
HG_H, HG_DK, HG_DV, HG_L = 8, 128, 128, 64
HG_K, HG_V = HG_H * HG_DK, HG_H * HG_DV
N_EXP, TOP_K, D_EXP, D_SH = 64, 8, 256, 256
ROUTE_SCALE = 2.5
ALPHA = (2 * DEPTH) ** 0.25
EPS = 1e-6
ROPE_THETA = 10000.0

VMEM_LIMIT = 56 * 1024 * 1024
LANES = 128
PACK_ROWS = D // (2 * LANES)
AT_GW = AT_G * AT_HD
AT_KBLK = AT_H * AT_HD // AT_GW

MOE_NS = 2
MOE_G = T_ALL // MOE_NS
MOE_M = 256
MOE_NB = MOE_G * TOP_K // MOE_M + N_EXP
MOE_ROWS = MOE_NB * MOE_M
MOE_KEY = 16384
MOE_S = MOE_M + 8
MOE_U = 8
SCAN_UNROLL = 4


def _cparams(sem):
    return pltpu.CompilerParams(dimension_semantics=sem, vmem_limit_bytes=VMEM_LIMIT)


def _sigmoid(x):
    return 1.0 / (1.0 + jnp.exp(-x))


def _silu(x):
    return x * _sigmoid(x)


def _log_sigmoid(x):
    return jnp.minimum(x, 0.0) - jnp.log(1.0 + jnp.exp(-jnp.abs(x)))


def _dot(a, b):
    return jnp.dot(a, b, preferred_element_type=F32)


def _dot_nt(a, b):
    return lax.dot_general(a, b, (((1,), (1,)), ((), ())), preferred_element_type=F32)


def _dot_tn(a, b):
    return lax.dot_general(a, b, (((0,), (0,)), ((), ())), preferred_element_type=F32)


def _dot_hi(a, b):
    return jnp.dot(a, b, preferred_element_type=F32, precision=HI)


def _split3(x):
    p0 = x.astype(BF16)
    r1 = x - p0.astype(F32)
    p1 = r1.astype(BF16)
    p2 = (r1 - p1.astype(F32)).astype(BF16)
    return p0, p1, p2


def _mask_dot(mask_bf16, x):
    p0, p1, p2 = _split3(x)
    return _dot(mask_bf16, p0) + _dot(mask_bf16, p1) + _dot(mask_bf16, p2)


def _dot_mask(x, mask_bf16):
    p0, p1, p2 = _split3(x)
    return _dot(p0, mask_bf16) + _dot(p1, mask_bf16) + _dot(p2, mask_bf16)


def _dot_3x(a, b):
    a0 = a.astype(BF16)
    a1 = (a - a0.astype(F32)).astype(BF16)
    b0 = b.astype(BF16)
    b1 = (b - b0.astype(F32)).astype(BF16)
    return _dot(a0, b0) + _dot(a0, b1) + _dot(a1, b0)


def _mod_kernel(cond_ref, w_ref, b_ref, o_ref):
    o_ref[...] = _dot_hi(_silu(cond_ref[...]), w_ref[...]) + b_ref[...]


def _mod_all(cond8, mod_w, mod_b):
    tn = 1024
    out = pl.pallas_call(
        _mod_kernel,
        grid=(DEPTH, 6 * D // tn),
        in_specs=[pl.BlockSpec((8, D), lambda l, j: (0, 0)),
                  pl.BlockSpec((None, D, tn), lambda l, j: (l, 0, j)),
                  pl.BlockSpec((None, 1, tn), lambda l, j: (l, 0, j))],
        out_specs=pl.BlockSpec((None, 8, tn), lambda l, j: (l, 0, j)),
        out_shape=jax.ShapeDtypeStruct((DEPTH, 8, 6 * D), F32),
        compiler_params=_cparams(("arbitrary", "arbitrary")),
        name="mod_rows",
    )(cond8, mod_w, mod_b.reshape(DEPTH, 1, 6 * D))
    return out.reshape(DEPTH, 8, 6, D)


def _proj_kernel(x_ref, mod_ref, w_ref, o_ref, h_scr):
    @pl.when(pl.program_id(1) == 0)
    def _():
        h_scr[...] = (x_ref[...] * (1.0 + mod_ref[1:2, :]) + mod_ref[0:1, :]).astype(BF16)

    o_ref[...] = _dot(h_scr[...], w_ref[...].astype(BF16)).astype(o_ref.dtype)


def _proj(x, mod4, layer, w3, widx, ncols, out_dtype, tm=2048, tn=512):
    T = x.shape[0]
    return pl.pallas_call(
        _proj_kernel,
        grid=(T // tm, ncols // tn),
        in_specs=[pl.BlockSpec((tm, D), lambda i, j: (i, 0)),
                  pl.BlockSpec((None, None, 6, D), lambda i, j: (layer, (i * tm) // GROUP, 0, 0)),
                  pl.BlockSpec((None, D, tn), lambda i, j: (widx, 0, j))],
        out_specs=pl.BlockSpec((tm, tn), lambda i, j: (i, j)),
        out_shape=jax.ShapeDtypeStruct((T, ncols), out_dtype),
        scratch_shapes=[pltpu.VMEM((tm, D), BF16)],
        compiler_params=_cparams(("arbitrary", "arbitrary")),
        name="in_proj",
    )(x, mod4, w3)


def _proj_small_kernel(x_ref, mod_ref, w_ref, b_ref, o_ref):
    h = x_ref[...] * (1.0 + mod_ref[1:2, :]) + mod_ref[0:1, :]
    o_ref[...] = _dot_3x(h, w_ref[...]) + b_ref[...]


def _proj_small(x, mod4, layer, w, b, tm=1024):
    T, n = x.shape[0], w.shape[1]
    return pl.pallas_call(
        _proj_small_kernel,
        grid=(T // tm,),
        in_specs=[pl.BlockSpec((tm, D), lambda i: (i, 0)),
                  pl.BlockSpec((None, None, 6, D), lambda i: (layer, (i * tm) // GROUP, 0, 0)),
                  pl.BlockSpec((D, n), lambda i: (0, 0)),
                  pl.BlockSpec((1, n), lambda i: (0, 0))],
        out_specs=pl.BlockSpec((tm, n), lambda i: (i, 0)),
        out_shape=jax.ShapeDtypeStruct((T, n), F32),
        compiler_params=_cparams(("arbitrary",)),
        name="gate_proj",
    )(x, mod4, w, b.reshape(1, n))


def _layer_norm_rows(r, g, b):
    mu = jnp.mean(r, axis=-1, keepdims=True)
    c = r - mu
    var = jnp.mean(c * c, axis=-1, keepdims=True)
    return c * lax.rsqrt(var + EPS) * g + b


def _pack_bf16_pairs(lo, hi):
    lo_b = lax.bitcast_convert_type(lo.astype(BF16).astype(F32), I32)
    hi_b = lax.bitcast_convert_type(hi.astype(BF16).astype(F32), I32)
    return lax.shift_right_logical(lo_b, 16) | (hi_b & jnp.int32(-65536))


def _unpack_bf16_pairs(v):
    lo = lax.bitcast_convert_type(lax.shift_left(v, 16), F32)
    hi = lax.bitcast_convert_type(v & jnp.int32(-65536), F32)
    return lo.astype(BF16), hi.astype(BF16)


def _unpack_rows(ref, rows, region_stride=None):
    chunks = []
    for r in range(PACK_ROWS):
        if region_stride is None:
            v = ref[pl.ds(r, rows, stride=PACK_ROWS), :]
        else:
            v = ref[pl.ds(r * region_stride, rows), :]
        chunks.extend(_unpack_bf16_pairs(v))
    return jnp.concatenate(chunks, axis=1)


def _post_kernel(ap_ref, as_ref, x_ref, mod_ref, w_ref, lng_ref, lnb_ref, rw_ref, rb_ref,
                 xo_ref, hp_ref, idx_ref, wt_ref, wb_scr, *, tm, n_ctx_tiles):
    @pl.when(pl.program_id(0) == 0)
    def _():
        wb_scr[...] = w_ref[...].astype(BF16)

    a = jnp.where(pl.program_id(0) < n_ctx_tiles, ap_ref[...], as_ref[...])
    y = _dot(a, wb_scr[...])
    r = ALPHA * x_ref[...] + mod_ref[2:3, :] * y
    xn = _layer_norm_rows(r, lng_ref[...], lnb_ref[...])
    xo_ref[...] = xn
    h1 = xn * (1.0 + mod_ref[4:5, :]) + mod_ref[3:4, :]
    for p in range(4):
        lo = h1[:, (2 * p) * LANES:(2 * p + 1) * LANES]
        hi = h1[:, (2 * p + 1) * LANES:(2 * p + 2) * LANES]
        hp_ref[pl.ds(p, tm, stride=4), :] = _pack_bf16_pairs(lo, hi)

    scores = _sigmoid(_dot_3x(h1, rw_ref[...]))
    sel = scores + rb_ref[...]
    e_iota = lax.broadcasted_iota(I32, (tm, N_EXP), 1).astype(F32)
    lane = lax.broadcasted_iota(I32, (tm, LANES), 1)
    idx_out = jnp.zeros((tm, LANES), F32)
    w_out = jnp.zeros((tm, LANES), F32)
    total = jnp.zeros((tm, 1), F32)
    for k in range(TOP_K):
        mx = jnp.max(sel, axis=-1, keepdims=True)
        ik = jnp.min(jnp.where(sel == mx, e_iota, float(N_EXP)), axis=-1, keepdims=True)
        hit = e_iota == ik
        wk = jnp.sum(jnp.where(hit, scores, 0.0), axis=-1, keepdims=True)
        sel = jnp.where(hit, -jnp.inf, sel)
        total = total + wk
        idx_out = jnp.where(lane == k, ik, idx_out)
        w_out = jnp.where(lane == k, wk, w_out)
    idx_ref[...] = idx_out.astype(I32)
    wt_ref[...] = w_out / total * ROUTE_SCALE


def _post(a_ctx, a_lat, x, mod4, layer, w_out, ln_g, ln_b, router_w, router_b, tm=512):
    T = x.shape[0]
    nc = a_ctx.shape[0] // tm
    return pl.pallas_call(
        functools.partial(_post_kernel, tm=tm, n_ctx_tiles=nc),
        grid=(T // tm,),
        in_specs=[pl.BlockSpec((tm, D), lambda i: (jnp.minimum(i, nc - 1), 0)),
                  pl.BlockSpec((tm, D), lambda i: (jnp.maximum(i - nc, 0), 0)),
                  pl.BlockSpec((tm, D), lambda i: (i, 0)),
                  pl.BlockSpec((None, None, 6, D), lambda i: (layer, (i * tm) // GROUP, 0, 0)),
                  pl.BlockSpec((D, D), lambda i: (0, 0)),
                  pl.BlockSpec((1, D), lambda i: (0, 0)),
                  pl.BlockSpec((1, D), lambda i: (0, 0)),
                  pl.BlockSpec((D, N_EXP), lambda i: (0, 0)),
                  pl.BlockSpec((1, N_EXP), lambda i: (0, 0))],
        out_specs=[pl.BlockSpec((tm, D), lambda i: (i, 0)),
                   pl.BlockSpec((tm * 4, LANES), lambda i: (i, 0)),
                   pl.BlockSpec((tm, LANES), lambda i: (i, 0)),
                   pl.BlockSpec((tm, LANES), lambda i: (i, 0))],
        out_shape=[jax.ShapeDtypeStruct((T, D), F32),
                   jax.ShapeDtypeStruct((T * 4, LANES), I32),
                   jax.ShapeDtypeStruct((T, LANES), I32),
                   jax.ShapeDtypeStruct((T, LANES), F32)],
        scratch_shapes=[pltpu.VMEM((D, D), BF16)],
        compiler_params=_cparams(("arbitrary",)),
        name="out_proj_ln_router",
    )(a_ctx, a_lat, x, mod4, w_out, ln_g.reshape(1, D), ln_b.reshape(1, D), router_w, router_b.reshape(1, N_EXP))


def _route_tables(idx128, w128):
    idx = idx128[:, :TOP_K].reshape(MOE_NS, MOE_G, TOP_K)
    w = w128[:, :TOP_K].reshape(MOE_NS, MOE_G, TOP_K)
    onehot = idx[..., None] == jnp.arange(N_EXP, dtype=I32)
    counts = jnp.sum(onehot.astype(I32), axis=(1, 2))
    padded = (counts + MOE_M - 1) // MOE_M * MOE_M
    pad_end = jnp.cumsum(padded, axis=-1)
    tok = jnp.arange(MOE_G, dtype=I32)[None, :, None]
    real_keys = (idx * MOE_KEY + tok).reshape(MOE_NS, MOE_G * TOP_K)
    fill = jnp.arange(MOE_M, dtype=I32)[None, None, :]
    e_ids = jnp.arange(N_EXP, dtype=I32)[None, :, None]
    fill_keys = jnp.where(fill < (padded - counts)[:, :, None], e_ids * MOE_KEY + MOE_KEY // 2 + fill,
                          N_EXP * MOE_KEY + e_ids * MOE_M + fill).reshape(MOE_NS, N_EXP * MOE_M)
    keys = jnp.concatenate([real_keys, fill_keys], axis=1)
    vals = jnp.concatenate([w.reshape(MOE_NS, MOE_G * TOP_K), jnp.zeros((MOE_NS, N_EXP * MOE_M), F32)], axis=1)
    sorted_rows = [lax.sort((keys[s], vals[s]), dimension=0, num_keys=1) for s in range(MOE_NS)]
    keys = jnp.stack([k for k, _ in sorted_rows])
    row_w = jnp.stack([v for _, v in sorted_rows])
    row_tok = jnp.where(keys < N_EXP * MOE_KEY, jnp.minimum(keys & (MOE_KEY - 1), MOE_G), MOE_G)
    row_tok = row_tok.reshape(-1)
    n_used = pad_end[:, -1] // MOE_M
    starts = jnp.arange(MOE_NB, dtype=I32) * MOE_M
    block_e = jnp.sum((starts[None, :, None] >= pad_end[:, None, :]).astype(I32), axis=-1)
    block_e = jnp.minimum(block_e, N_EXP - 1)
    last_e = jnp.take_along_axis(block_e, jnp.maximum(n_used - 1, 0)[:, None], axis=1)
    block_e = jnp.where(jnp.arange(MOE_NB, dtype=I32)[None, :] < n_used[:, None], block_e, last_e)
    return (block_e.reshape(-1).astype(I32), n_used.astype(I32), row_tok,
            row_w.reshape(MOE_NS * MOE_NB, 1, MOE_M))


def _cast_kernel(x_ref, o_ref):
    o_ref[...] = x_ref[...].astype(BF16)


def _cast_experts(w4, layer, eb=4):
    _, n_e, a, b = w4.shape
    return pl.pallas_call(
        _cast_kernel,
        grid=(n_e // eb,),
        in_specs=[pl.BlockSpec((None, eb, a, b), lambda i: (layer, i, 0, 0))],
        out_specs=pl.BlockSpec((eb, a, b), lambda i: (i, 0, 0)),
        out_shape=jax.ShapeDtypeStruct((n_e, a, b), BF16),
        compiler_params=_cparams(("arbitrary",)),
        name="cast_experts",
    )(w4)


def _moe_stage(tg_ref, ts_ref, x_v, acc, rw_ref, wg_ref, wu_ref, wd_ref, tile_g, tile_c, ys_c, ys_s):
    for mi in range(MOE_M):
        tile_g[pl.ds(mi, 4, stride=MOE_S), :] = x_v[pl.ds(pl.multiple_of(tg_ref[0, mi], 4), 4), :]
    xb = _unpack_rows(tile_c, MOE_M, MOE_S)
    g = _dot(xb, wg_ref[...])
    u = _dot(xb, wu_ref[...])
    ri = lax.broadcasted_iota(I32, (MOE_M, MOE_M), 0)
    ci = lax.broadcasted_iota(I32, (MOE_M, MOE_M), 1)
    rw_col = jnp.sum(jnp.where(ri == ci, jnp.broadcast_to(rw_ref[...], (MOE_M, MOE_M)), 0.0),
                     axis=1, keepdims=True)
    a = (_silu(g) * u) * rw_col
    y = _dot(a.astype(BF16), wd_ref[...])
    for c in range(D // LANES):
        ys_c[pl.ds(c * MOE_S, MOE_M), :] = y[:, c * LANES:(c + 1) * LANES]
    for m0 in range(0, MOE_M, MOE_U):
        offs = [pl.multiple_of(ts_ref[0, m0 + j], 8) for j in range(MOE_U)]
        vals = [acc[pl.ds(offs[j], 8), :] + ys_s[pl.ds(m0 + j, 8, stride=MOE_S), :] for j in range(MOE_U)]
        for j in range(MOE_U):
            acc[pl.ds(offs[j], 8), :] = vals[j]


def _moe_kernel(be_ref, nu_ref, tg_ref, ts_ref, x_hbm, rw_ref, wg_ref, wu_ref, wd_ref, out_hbm,
                x_v, acc, tile_a, tile_b, ys_a, ys_b, sem):
    s = pl.program_id(0)
    j = pl.program_id(1)

    @pl.when(j == 0)
    def _():
        cp = pltpu.make_async_copy(x_hbm.at[s], x_v.at[pl.ds(0, MOE_G * 4)], sem.at[0])
        cp.start()
        cp.wait()
        x_v[pl.ds(MOE_G * 4, 8), :] = jnp.zeros((8, LANES), I32)

        def clear(i, carry):
            acc[pl.ds(pl.multiple_of(i * 8, 8), 8), :] = jnp.zeros((8, LANES), F32)
            return carry

        lax.fori_loop(0, MOE_G + 1, clear, 0)
        for t in (tile_a, tile_b):
            t[...] = jnp.zeros(t.shape, I32)
        for y in (ys_a, ys_b):
            y[...] = jnp.zeros(y.shape, F32)

    live = j < nu_ref[s] + 2
    common = (tg_ref, ts_ref, x_v, acc, rw_ref, wg_ref, wu_ref, wd_ref)

    @pl.when(live & (j % 2 == 0))
    def _():
        _moe_stage(*common, tile_a, tile_b, ys_b, ys_a)

    @pl.when(live & (j % 2 == 1))
    def _():
        _moe_stage(*common, tile_b, tile_a, ys_a, ys_b)

    @pl.when(j == MOE_NB + 1)
    def _():
        cp = pltpu.make_async_copy(acc.at[pl.ds(0, MOE_G * 8)], out_hbm.at[s], sem.at[1])
        cp.start()
        cp.wait()


def _moe_routed(hp, tables, layer, w_gate, w_up, w_down):
    block_e, n_used, row_tok, row_w = tables
    x3 = hp.reshape(MOE_NS, MOE_G * 4, LANES)
    blk = lambda s, j, d: s * MOE_NB + jnp.clip(j - d, 0, MOE_NB - 1)
    wspec = lambda shape: pl.BlockSpec((None,) + shape, lambda s, j, be, nu: (be[blk(s, j, 1)], 0, 0))
    tspec = lambda d: pl.BlockSpec((None, 1, MOE_M), lambda s, j, be, nu: (blk(s, j, d), 0, 0),
                                   memory_space=pltpu.SMEM)
    tok3 = row_tok.reshape(MOE_NS * MOE_NB, 1, MOE_M)
    out = pl.pallas_call(
        _moe_kernel,
        grid_spec=pltpu.PrefetchScalarGridSpec(
            num_scalar_prefetch=2,
            grid=(MOE_NS, MOE_NB + 2),
            in_specs=[tspec(0), tspec(2),
                      pl.BlockSpec(memory_space=pl.ANY),
                      pl.BlockSpec((None, 1, MOE_M), lambda s, j, be, nu: (blk(s, j, 1), 0, 0)),
                      wspec((D, D_EXP)), wspec((D, D_EXP)), wspec((D_EXP, D))],
            out_specs=pl.BlockSpec(memory_space=pl.ANY),
            scratch_shapes=[pltpu.VMEM((MOE_G * 4 + 8, LANES), I32),
                            pltpu.VMEM(((MOE_G + 1) * 8, LANES), F32),
                            pltpu.VMEM((4 * MOE_S, LANES), I32),
                            pltpu.VMEM((4 * MOE_S, LANES), I32),
                            pltpu.VMEM((8 * MOE_S, LANES), F32),
                            pltpu.VMEM((8 * MOE_S, LANES), F32),
                            pltpu.SemaphoreType.DMA((2,))]),
        out_shape=jax.ShapeDtypeStruct((MOE_NS, MOE_G * 8, LANES), F32),
        compiler_params=_cparams(("arbitrary", "arbitrary")),
        name="moe_routed",
    )(block_e, n_used, tok3 * 4, tok3 * 8, x3, row_w,
      _cast_experts(w_gate, layer), _cast_experts(w_up, layer), _cast_experts(w_down, layer))
    return out.reshape(T_ALL * 8, LANES)


def _moe_post_kernel(x_ref, hp_ref, r_ref, mod_ref, sg_ref, su_ref, sd_ref, lng_ref, lnb_ref, o_ref,
                     sgb, sub, sdb, *, tm):
    @pl.when(pl.program_id(0) == 0)
    def _():
        sgb[...] = sg_ref[...].astype(BF16)
        sub[...] = su_ref[...].astype(BF16)
        sdb[...] = sd_ref[...].astype(BF16)

    hb = _unpack_rows(hp_ref, tm)
    g = _dot(hb, sgb[...])
    u = _dot(hb, sub[...])
    sh = _dot((_silu(g) * u).astype(BF16), sdb[...])
    routed = jnp.concatenate([r_ref[pl.ds(c, tm, stride=8), :] for c in range(D // LANES)], axis=1)
    r = ALPHA * x_ref[...] + mod_ref[5:6, :] * (routed + sh)
    o_ref[...] = _layer_norm_rows(r, lng_ref[...], lnb_ref[...])


def _moe_post(x, hp, routed, mod4, layer, sg, su, sd, ln_g, ln_b, tm=512):
    T = x.shape[0]
    return pl.pallas_call(
        functools.partial(_moe_post_kernel, tm=tm),
        grid=(T // tm,),
        in_specs=[pl.BlockSpec((tm, D), lambda i: (i, 0)),
                  pl.BlockSpec((tm * 4, LANES), lambda i: (i, 0)),
                  pl.BlockSpec((tm * 8, LANES), lambda i: (i, 0)),
                  pl.BlockSpec((None, None, 6, D), lambda i: (layer, (i * tm) // GROUP, 0, 0)),
                  pl.BlockSpec((D, D_SH), lambda i: (0, 0)),
                  pl.BlockSpec((D, D_SH), lambda i: (0, 0)),
                  pl.BlockSpec((D_SH, D), lambda i: (0, 0)),
                  pl.BlockSpec((1, D), lambda i: (0, 0)),
                  pl.BlockSpec((1, D), lambda i: (0, 0))],
        out_specs=pl.BlockSpec((tm, D), lambda i: (i, 0)),
        out_shape=jax.ShapeDtypeStruct((T, D), F32),
        scratch_shapes=[pltpu.VMEM((D, D_SH), BF16), pltpu.VMEM((D, D_SH), BF16), pltpu.VMEM((D_SH, D), BF16)],
        compiler_params=_cparams(("arbitrary",)),
        name="shared_expert_ln",
    )(x, hp, routed, mod4, sg, su, sd, ln_g.reshape(1, D), ln_b.reshape(1, D))


def _conv_silu(src_ref, w_ref, dst_ref, T, scale):
    L = ML_L
    nc = T // L
    width = src_ref.shape[1]
    w0, w1, w2 = w_ref[0:1, :], w_ref[1:2, :], w_ref[2:3, :]
    row = lax.broadcasted_iota(I32, (L, width), 0)

    def body(c, carry):
        r0 = pl.multiple_of(c * L, L)
        cur = src_ref[pl.ds(r0, L), :].astype(F32)
        p0 = pl.multiple_of(jnp.maximum(r0 - 16, 0), 16)
        n0 = pl.multiple_of(jnp.minimum(r0 + L, T - 16), 16)
        prev_row = src_ref[pl.ds(p0, 16), :].astype(F32)[15:16, :] * jnp.where(c > 0, 1.0, 0.0).astype(F32)
        next_row = src_ref[pl.ds(n0, 16), :].astype(F32)[0:1, :] * jnp.where(c < nc - 1, 1.0, 0.0).astype(F32)
        prev = jnp.where(row == 0, prev_row, pltpu.roll(cur, 1, axis=0))
        nxt = jnp.where(row == L - 1, next_row, pltpu.roll(cur, L - 1, axis=0))
        dst_ref[pl.ds(r0, L), :] = _silu(w0 * prev + w1 * cur + w2 * nxt) * scale
        return carry

    lax.fori_loop(0, nc, body, 0)


def _mlstm_chunk(q, k, v, G, GT, C, n, m, backward):
    L = ML_L
    ri = lax.broadcasted_iota(I32, (L, L), 0)
    ci = lax.broadcasted_iota(I32, (L, L), 1)
    keep = (ci >= ri) if backward else (ci <= ri)
    A = keep.astype(BF16)
    AT = ((ri >= ci) if backward else (ri <= ci)).astype(BF16)
    ic, fc = (2, 3) if backward else (0, 1)
    last = 0 if backward else L - 1
    b_col = _mask_dot(A, _log_sigmoid(G))[:, fc:fc + 1]
    b_row = _dot_mask(_log_sigmoid(GT), AT)[fc:fc + 1, :]
    ig_col = G[:, ic:ic + 1]
    ig_row = GT[ic:ic + 1, :]
    log_d = jnp.where(keep, b_col - b_row + ig_row, -jnp.inf)
    log_inter = b_col + m
    m_t = jnp.maximum(log_inter, jnp.max(log_d, axis=-1, keepdims=True))
    d = jnp.exp(log_d - m_t)
    w_inter = jnp.exp(log_inter - m_t)
    qb, kb, vb = q.astype(BF16), k.astype(BF16), v.astype(BF16)
    s = _dot_nt(qb, kb) * d
    num = _dot(s.astype(BF16), vb) + w_inter * _dot(qb, C.astype(BF16))
    den = jnp.sum(s, axis=-1, keepdims=True) + w_inter * jnp.sum(q * n, axis=-1, keepdims=True)
    h = num / jnp.maximum(jnp.abs(den), jnp.exp(-m_t))
    m_new = m_t[last:last + 1, :]
    w_last = jnp.exp(b_col[last:last + 1, :] - b_col + ig_col - m_new)
    decay = w_inter[last:last + 1, :]
    kw = k * w_last
    C_new = decay * C + _dot_tn(kw.astype(BF16), vb)
    n_new = decay * n + jnp.sum(kw, axis=0, keepdims=True)
    return h, C_new, n_new, m_new


def _mlstm_kernel(q_ref, k_ref, v_ref, og_ref, g_ref, gt_ref, cq_ref, ck_ref, ng_ref, c0_ref, n0_ref, m0_ref,
                  a_ref, c_out, n_out, m_out, qs, ks, hf, hb, cst, nst, mst, *, T, nh):
    L = ML_L
    nc = T // L
    _conv_silu(q_ref, cq_ref, qs, T, ML_DK ** -0.5)
    _conv_silu(k_ref, ck_ref, ks, T, 1.0)
    cst[...] = c0_ref[...]
    nst[...] = n0_ref[...]
    mst[...] = m0_ref[...]
    un = min(SCAN_UNROLL, nc)

    def body(i, carry):
        for h in range(nh):
            kcols = slice(h * ML_DK, (h + 1) * ML_DK)
            vcols = slice(h * ML_DV, (h + 1) * ML_DV)
            for direction, out in ((0, hf), (1, hb)):
                C, n, m = cst[direction, h], nst[direction, h], mst[direction, h]
                for u in range(un):
                    ci = i * un + u
                    c = (nc - 1 - ci) if direction else ci
                    r0 = pl.multiple_of(c * L, L)
                    hh, C, n, m = _mlstm_chunk(
                        qs[pl.ds(r0, L), kcols], ks[pl.ds(r0, L), kcols], v_ref[pl.ds(r0, L), vcols],
                        g_ref[h, pl.ds(r0, L), :], gt_ref[h, c], C, n, m, backward=bool(direction))
                    out[pl.ds(r0, L), vcols] = hh
                cst[direction, h] = C
                nst[direction, h] = n
                mst[direction, h] = m
        return carry

    lax.fori_loop(0, nc // un, body, 0)
    c_out[...] = cst[...]
    n_out[...] = nst[...]
    m_out[...] = mst[...]

    def finish(c, carry):
        r0 = pl.multiple_of(c * L, L)
        for h in range(nh):
            vcols = slice(h * ML_DV, (h + 1) * ML_DV)
            tot = hf[pl.ds(r0, L), vcols] + hb[pl.ds(r0, L), vcols]
            mu = jnp.mean(tot, axis=-1, keepdims=True)
            cen = tot - mu
            var = jnp.mean(cen * cen, axis=-1, keepdims=True)
            hn = cen * lax.rsqrt(var + EPS) * ng_ref[:, vcols]
            a_ref[pl.ds(r0, L), vcols] = (hn * _sigmoid(og_ref[pl.ds(r0, L), vcols].astype(F32))).astype(BF16)
        return carry

    lax.fori_loop(0, nc, finish, 0)


def _mlstm_scan(z, gh, ght, conv_w, norm_g, C0, n0, m0, T, row_blk0, nseq, nh):
    nc = T // ML_L
    kw, vw = nh * ML_DK, nh * ML_DV
    qcol, kcol = 0, ML_QK // kw
    vcol, ocol = 2 * ML_QK // vw, (2 * ML_QK + ML_V) // vw
    rb = lambda s: row_blk0 + s
    state = lambda *tail: pl.BlockSpec((None, 2, nh) + tail, lambda s, h: (s, 0, h) + (0,) * len(tail))
    return pl.pallas_call(
        functools.partial(_mlstm_kernel, T=T, nh=nh),
        grid=(nseq, ML_H // nh),
        in_specs=[pl.BlockSpec((T, kw), lambda s, h: (rb(s), qcol + h)),
                  pl.BlockSpec((T, kw), lambda s, h: (rb(s), kcol + h)),
                  pl.BlockSpec((T, vw), lambda s, h: (rb(s), vcol + h)),
                  pl.BlockSpec((T, vw), lambda s, h: (rb(s), ocol + h)),
                  pl.BlockSpec((nh, T, 4), lambda s, h: (h, rb(s), 0)),
                  pl.BlockSpec((nh, nc, 4, ML_L), lambda s, h: (h, rb(s), 0, 0)),
                  pl.BlockSpec((3, kw), lambda s, h: (0, qcol + h)),
                  pl.BlockSpec((3, kw), lambda s, h: (0, kcol + h)),
                  pl.BlockSpec((1, vw), lambda s, h: (0, h)),
                  state(ML_DK, ML_DV), state(1, ML_DK), state(1, 1)],
        out_specs=[pl.BlockSpec((T, vw), lambda s, h: (s, h)),
                   state(ML_DK, ML_DV), state(1, ML_DK), state(1, 1)],
        out_shape=[jax.ShapeDtypeStruct((nseq * T, ML_V), BF16),
                   jax.ShapeDtypeStruct((nseq, 2, ML_H, ML_DK, ML_DV), F32),
                   jax.ShapeDtypeStruct((nseq, 2, ML_H, 1, ML_DK), F32),
                   jax.ShapeDtypeStruct((nseq, 2, ML_H, 1, 1), F32)],
        scratch_shapes=[pltpu.VMEM((T, kw), F32), pltpu.VMEM((T, kw), F32),
                        pltpu.VMEM((T, vw), F32), pltpu.VMEM((T, vw), F32),
                        pltpu.VMEM((2, nh, ML_DK, ML_DV), F32), pltpu.VMEM((2, nh, 1, ML_DK), F32),
                        pltpu.VMEM((2, nh, 1, 1), F32)],
        compiler_params=_cparams(("arbitrary", "arbitrary")),
        name="mlstm_scan",
    )(z, z, z, z, gh, ght, conv_w, conv_w, norm_g.reshape(1, ML_V), C0, n0, m0)


def _mlstm_layer(x, mod4, layer, j, w_in, gate_b, conv_w, norm_g, C_lat, n_lat, m_lat):
    z = _proj(x, mod4, layer, w_in, j, 2 * ML_QK + 2 * ML_V, BF16)
    gates = _proj_small(x, mod4, layer, w_in[j][:, 2 * ML_QK + 2 * ML_V:], gate_b[j])
    gh = gates.reshape(T_ALL, 4, ML_H).transpose(2, 0, 1)
    ght = gh.reshape(ML_H, T_ALL // ML_L, ML_L, 4).transpose(0, 1, 3, 2)
    zC = jnp.zeros((BATCH, 2, ML_H, ML_DK, ML_DV), F32)
    zn = jnp.zeros((BATCH, 2, ML_H, 1, ML_DK), F32)
    zm = jnp.zeros((BATCH, 2, ML_H, 1, 1), F32)
    a_p, Cn, nn, mn = _mlstm_scan(z, gh, ght, conv_w[j], norm_g[j], zC, zn, zm, SEQ, 0, BATCH, ML_H)
    a_s, _, _, _ = _mlstm_scan(z, gh, ght, conv_w[j], norm_g[j], C_lat[:, j],
                               n_lat[:, j].reshape(DEC_BATCH, 2, ML_H, 1, ML_DK),
                               m_lat[:, j].reshape(DEC_BATCH, 2, ML_H, 1, 1),
                               DEC_SEQ, BATCH * SEQ // DEC_SEQ, DEC_BATCH, 1)
    return (a_p, a_s), (Cn, nn.reshape(BATCH, 2, ML_H, ML_DK), mn.reshape(BATCH, 2, ML_H))


def _hgrn_chunk(q, k, v, g, St, backward):
    L = HG_L
    ri = lax.broadcasted_iota(I32, (L, L), 0)
    ci = lax.broadcasted_iota(I32, (L, L), 1)
    keep = (ci >= ri) if backward else (ci <= ri)
    ref = L - 1 - L // 2 if backward else L // 2
    last = 0 if backward else L - 1
    b = _mask_dot(keep.astype(BF16), g)
    b_ref = b[ref:ref + 1, :]
    b_last = b[last:last + 1, :]
    qe = (q * jnp.exp(b - b_ref)).astype(BF16)
    ke = (k * jnp.exp(b_ref - b)).astype(BF16)
    vb = v.astype(BF16)
    a = jnp.where(keep, _dot_nt(qe, ke), 0.0)
    o = _dot(a.astype(BF16), vb) + _dot_nt((q * jnp.exp(b)).astype(BF16), St.astype(BF16))
    kd = (k * jnp.exp(b_last - b)).astype(BF16)
    St_new = jnp.exp(b_last) * St + _dot_tn(vb, kd)
    return o, St_new


def _hgrn_kernel(q_ref, i_ref, ff_ref, fb_ref, og_ref, fbf_ref, fbb_ref, lbr_ref, ng_ref, s0_ref,
                 a_ref, s_out, of, ob, st, *, T, lb_layer, nh):
    L = HG_L
    nc = T // L
    raw = lbr_ref[...]
    e = jnp.exp(raw - jnp.max(raw, axis=0, keepdims=True))
    p = e / jnp.sum(e, axis=0, keepdims=True)
    lb_all = jnp.sum(p[0:lb_layer + 1, :], axis=0, keepdims=True) - p[0:1, :]
    for h in range(nh):
        st[0, h] = s0_ref[0, h].T
        st[1, h] = s0_ref[1, h].T
    un = min(SCAN_UNROLL, nc)

    def body(i, carry):
        for h in range(nh):
            cols = slice(h * HG_DK, (h + 1) * HG_DK)
            lb = lb_all[:, cols]
            for direction, out, f_ref, b_ref in ((0, of, ff_ref, fbf_ref), (1, ob, fb_ref, fbb_ref)):
                St = st[direction, h]
                for u in range(un):
                    ci = i * un + u
                    c = (nc - 1 - ci) if direction else ci
                    r0 = pl.multiple_of(c * L, L)
                    f = lb + (1.0 - lb) * _sigmoid(f_ref[pl.ds(r0, L), cols] + b_ref[:, cols])
                    o, St = _hgrn_chunk(_silu(q_ref[pl.ds(r0, L), cols]), 1.0 - f, i_ref[pl.ds(r0, L), cols],
                                        jnp.log(f), St, backward=bool(direction))
                    out[pl.ds(r0, L), cols] = o
                st[direction, h] = St
        return carry

    lax.fori_loop(0, nc // un, body, 0)
    for h in range(nh):
        s_out[0, h] = st[0, h].T
        s_out[1, h] = st[1, h].T

    def finish(c, carry):
        r0 = pl.multiple_of(c * L, L)
        for h in range(nh):
            cols = slice(h * HG_DV, (h + 1) * HG_DV)
            tot = of[pl.ds(r0, L), cols] + ob[pl.ds(r0, L), cols]
            on = tot * lax.rsqrt(jnp.mean(tot * tot, axis=-1, keepdims=True) + EPS) * ng_ref[:, cols]
            a_ref[pl.ds(r0, L), cols] = (on * _silu(og_ref[pl.ds(r0, L), cols])).astype(BF16)
        return carry

    lax.fori_loop(0, nc, finish, 0)


def _hgrn_scan(z, f_b, lb_raw, norm_g, S0, T, row_blk0, nseq, lb_layer, nh):
    nhb = HG_H // nh
    w = nh * HG_DK
    rb = lambda s: row_blk0 + s
    zspec = lambda cb: pl.BlockSpec((T, w), lambda s, h: (rb(s), cb * nhb + h))
    sspec = pl.BlockSpec((None, 2, nh, HG_DK, HG_DV), lambda s, h: (s, 0, h, 0, 0))
    return pl.pallas_call(
        functools.partial(_hgrn_kernel, T=T, lb_layer=lb_layer, nh=nh),
        grid=(nseq, nhb),
        in_specs=[zspec(0), zspec(1), zspec(2), zspec(3), zspec(4),
                  pl.BlockSpec((1, w), lambda s, h: (0, h)),
                  pl.BlockSpec((1, w), lambda s, h: (0, nhb + h)),
                  pl.BlockSpec((DEPTH, w), lambda s, h: (0, h)),
                  pl.BlockSpec((1, w), lambda s, h: (0, h)),
                  sspec],
        out_specs=[pl.BlockSpec((T, w), lambda s, h: (s, h)), sspec],
        out_shape=[jax.ShapeDtypeStruct((nseq * T, HG_V), BF16),
                   jax.ShapeDtypeStruct((nseq, 2, HG_H, HG_DK, HG_DV), F32)],
        scratch_shapes=[pltpu.VMEM((T, w), F32), pltpu.VMEM((T, w), F32),
                        pltpu.VMEM((2, nh, HG_DV, HG_DK), F32)],
        compiler_params=_cparams(("arbitrary", "arbitrary")),
        name="hgrn_scan",
    )(z, z, z, z, z, f_b.reshape(1, 2 * HG_K), f_b.reshape(1, 2 * HG_K), lb_raw, norm_g.reshape(1, HG_V), S0)


def _hgrn_layer(x, mod4, layer, j, w_in, f_b, lb_raw, norm_g, S_lat):
    z = _proj(x, mod4, layer, w_in, j, 3 * HG_K + 2 * HG_V, F32)
    zS = jnp.zeros((BATCH, 2, HG_H, HG_DK, HG_DV), F32)
    a_p, Sn = _hgrn_scan(z, f_b[j], lb_raw, norm_g[j], zS, SEQ, 0, BATCH, layer, 4)
    a_s, _ = _hgrn_scan(z, f_b[j], lb_raw, norm_g[j], S_lat[:, j], DEC_SEQ, BATCH * SEQ // DEC_SEQ, DEC_BATCH,
                        layer, 1)
    return (a_p, a_s), Sn


def _head_rms(x, g_tiled, nheads):
    lane = lax.broadcasted_iota(I32, x.shape, 1)
    sq = x * x
    ms = jnp.zeros_like(x)
    for h in range(nheads):
        in_h = (lane >= h * AT_HD) & (lane < (h + 1) * AT_HD)
        tot = jnp.sum(jnp.where(in_h, sq, 0.0), axis=-1, keepdims=True)
        ms = jnp.where(in_h, tot, ms)
    return x * lax.rsqrt(ms * (1.0 / AT_HD) + EPS) * g_tiled


def _rope(x, cos, sin):
    w = x.shape[1]
    lane = lax.broadcasted_iota(I32, x.shape, 1)
    up = pltpu.roll(x, w - 16, axis=1)
    down = pltpu.roll(x, 16, axis=1)
    swapped = jnp.where((lane % 32) < 16, up, down)
    return x * cos + swapped * sin


def _attn_ctx_kernel(q_ref, k_ref, v_ref, qn_ref, kn_ref, o_ref, ko_ref, vo_ref):
    k = _head_rms(k_ref[...].astype(F32), kn_ref[...], AT_KV)
    ko_ref[...] = k
    v = v_ref[...].astype(F32)
    vo_ref[...] = v
    kb, vb = k.astype(BF16), v.astype(BF16)
    outs = []
    for kv in range(AT_KV):
        q = _head_rms(q_ref[:, kv * AT_GW:(kv + 1) * AT_GW].astype(F32), qn_ref[...], AT_G) * (AT_HD ** -0.5)
        kh = kb[:, kv * AT_HD:(kv + 1) * AT_HD]
        vh = vb[:, kv * AT_HD:(kv + 1) * AT_HD]
        for g in range(AT_G):
            s = _dot_nt(q[:, g * AT_HD:(g + 1) * AT_HD].astype(BF16), kh)
            p = jnp.exp(s - jnp.max(s, axis=-1, keepdims=True))
            p = p / jnp.sum(p, axis=-1, keepdims=True)
            outs.append(_dot(p.astype(BF16), vh))
    o_ref[...] = jnp.concatenate(outs, axis=1).astype(BF16)


def _attn_ctx(z, q_norm, k_norm):
    qn = jnp.tile(q_norm, AT_G).reshape(1, AT_GW)
    kn = jnp.tile(k_norm, AT_KV).reshape(1, AT_GW)
    return pl.pallas_call(
        _attn_ctx_kernel,
        grid=(BATCH,),
        in_specs=[pl.BlockSpec((SEQ, D), lambda b: (b, 0)),
                  pl.BlockSpec((SEQ, AT_GW), lambda b: (b, AT_KBLK)),
                  pl.BlockSpec((SEQ, AT_GW), lambda b: (b, AT_KBLK + 1)),
                  pl.BlockSpec((1, AT_GW), lambda b: (0, 0)),
                  pl.BlockSpec((1, AT_GW), lambda b: (0, 0))],
        out_specs=[pl.BlockSpec((SEQ, D), lambda b: (b, 0)),
                   pl.BlockSpec((SEQ, AT_GW), lambda b: (b, 0)),
                   pl.BlockSpec((SEQ, AT_GW), lambda b: (b, 0))],
        out_shape=[jax.ShapeDtypeStruct((BATCH * SEQ, D), BF16),
                   jax.ShapeDtypeStruct((BATCH * SEQ, AT_GW), F32),
                   jax.ShapeDtypeStruct((BATCH * SEQ, AT_GW), F32)],
        compiler_params=_cparams(("arbitrary",)),
        name="attn_context",
    )(z, z, z, qn, kn)


def _attn_kv_kernel(k_ref, v_ref, kn_ref, cos_ref, sin_ref, ko_ref, vo_ref):
    k = _rope(_head_rms(k_ref[...].astype(F32), kn_ref[...], AT_KV), cos_ref[...], sin_ref[...])
    v = v_ref[...]
    for h in range(AT_KV):
        ko_ref[h] = k[:, h * AT_HD:(h + 1) * AT_HD].astype(BF16)
        vo_ref[h] = v[:, h * AT_HD:(h + 1) * AT_HD].astype(BF16)


def _attn_kv(z, k_norm, cos4, sin4, tt=512):
    kn = jnp.tile(k_norm, AT_KV).reshape(1, AT_GW)
    nt = DEC_SEQ // tt
    row0 = BATCH * SEQ // tt
    return pl.pallas_call(
        _attn_kv_kernel,
        grid=(DEC_BATCH, nt),
        in_specs=[pl.BlockSpec((tt, AT_GW), lambda b, i: (row0 + b * nt + i, AT_KBLK)),
                  pl.BlockSpec((tt, AT_GW), lambda b, i: (row0 + b * nt + i, AT_KBLK + 1)),
                  pl.BlockSpec((1, AT_GW), lambda b, i: (0, 0)),
                  pl.BlockSpec((tt, AT_GW), lambda b, i: (i, 0)),
                  pl.BlockSpec((tt, AT_GW), lambda b, i: (i, 0))],
        out_specs=[pl.BlockSpec((None, AT_KV, tt, AT_HD), lambda b, i: (b, 0, i, 0)),
                   pl.BlockSpec((None, AT_KV, tt, AT_HD), lambda b, i: (b, 0, i, 0))],
        out_shape=[jax.ShapeDtypeStruct((DEC_BATCH, AT_KV, DEC_SEQ, AT_HD), BF16),
                   jax.ShapeDtypeStruct((DEC_BATCH, AT_KV, DEC_SEQ, AT_HD), BF16)],
        compiler_params=_cparams(("arbitrary", "arbitrary")),
        name="attn_kv_prep",
    )(z, z, kn, cos4, sin4)


def _attn_lat_kernel(q_ref, qn_ref, cos_ref, sin_ref, k_ref, v_ref, o_ref, *, tq, tk):
    q = _rope(_head_rms(q_ref[...].astype(F32), qn_ref[...], AT_G), cos_ref[...], sin_ref[...]) * (AT_HD ** -0.5)
    qs = jnp.concatenate([q[:, g * AT_HD:(g + 1) * AT_HD] for g in range(AT_G)], axis=0).astype(BF16)
    rows = AT_G * tq
    nk = k_ref.shape[0] // tk

    def body(j, carry):
        m, l, acc = carry
        k0 = pl.multiple_of(j * tk, tk)
        s = _dot_nt(qs, k_ref[pl.ds(k0, tk), :])
        m_new = jnp.maximum(m, jnp.max(s, axis=-1, keepdims=True))
        alpha = jnp.exp(m - m_new)
        p = jnp.exp(s - m_new)
        l = alpha * l + jnp.sum(p, axis=-1, keepdims=True)
        acc = alpha * acc + _dot(p.astype(BF16), v_ref[pl.ds(k0, tk), :])
        return m_new, l, acc

    init = (jnp.full((rows, 1), -jnp.inf, F32), jnp.zeros((rows, 1), F32), jnp.zeros((rows, AT_HD), F32))
    _, l, acc = lax.fori_loop(0, nk, body, init)
    o = acc / l
    o_ref[...] = jnp.concatenate([o[g * tq:(g + 1) * tq, :] for g in range(AT_G)], axis=1).astype(BF16)


def _attn_lat(z, q_norm, cos4, sin4, kk, vv, tq=256, tk=1536):
    qn = jnp.tile(q_norm, AT_G).reshape(1, AT_GW)
    nq = DEC_SEQ // tq
    row0 = BATCH * SEQ // tq
    skv = kk.shape[2]
    return pl.pallas_call(
        functools.partial(_attn_lat_kernel, tq=tq, tk=tk),
        grid=(DEC_BATCH, AT_KV, nq),
        in_specs=[pl.BlockSpec((tq, AT_GW), lambda b, h, i: (row0 + b * nq + i, h)),
                  pl.BlockSpec((1, AT_GW), lambda b, h, i: (0, 0)),
                  pl.BlockSpec((tq, AT_GW), lambda b, h, i: (i, 0)),
                  pl.BlockSpec((tq, AT_GW), lambda b, h, i: (i, 0)),
                  pl.BlockSpec((None, None, skv, AT_HD), lambda b, h, i: (b, h, 0, 0)),
                  pl.BlockSpec((None, None, skv, AT_HD), lambda b, h, i: (b, h, 0, 0))],
        out_specs=pl.BlockSpec((tq, AT_GW), lambda b, h, i: (b * nq + i, h)),
        out_shape=jax.ShapeDtypeStruct((DEC_BATCH * DEC_SEQ, D), BF16),
        compiler_params=_cparams(("arbitrary", "arbitrary", "arbitrary")),
        name="attn_latent",
    )(z, qn, cos4, sin4, kk, vv)


def _rope_tables():
    t = jnp.arange(DEC_SEQ)
    row = (t // GRID_W).astype(F32)
    col = (t % GRID_W).astype(F32)
    nf = AT_HD // 4
    inv = ROPE_THETA ** (-jnp.arange(nf, dtype=F32) / nf)
    ar, ac = row[:, None] * inv[None], col[:, None] * inv[None]
    cos = jnp.concatenate([jnp.cos(ar), jnp.cos(ar), jnp.cos(ac), jnp.cos(ac)], axis=1)
    sin = jnp.concatenate([-jnp.sin(ar), jnp.sin(ar), -jnp.sin(ac), jnp.sin(ac)], axis=1)
    return jnp.tile(cos, (1, 4)), jnp.tile(sin, (1, 4))


def _attn_layer(x, mod4, layer, j, w_in, q_norm, k_norm, cache_k, cache_v):
    z = _proj(x, mod4, layer, w_in, j, (AT_H + 2 * AT_KV) * AT_HD, BF16)
    a_p, k_new, v_new = _attn_ctx(z, q_norm[j], k_norm[j])
    cos4, sin4 = _rope_tables()
    k_lat, v_lat = _attn_kv(z, k_norm[j], cos4, sin4)
    kk = jnp.concatenate([cache_k[:, j].transpose(0, 2, 1, 3).astype(BF16), k_lat], axis=2)
    vv = jnp.concatenate([cache_v[:, j].transpose(0, 2, 1, 3).astype(BF16), v_lat], axis=2)
    a_s = _attn_lat(z, q_norm[j], cos4, sin4, kk, vv)
    return (a_p, a_s), (k_new.reshape(BATCH, SEQ, AT_KV, AT_HD), v_new.reshape(BATCH, SEQ, AT_KV, AT_HD))


def kernel(x_prompt, x_sample, state_mlstm_C, state_mlstm_n, state_mlstm_m, cache_attn_k, cache_attn_v, state_hgrn_S, c, c_ctx, mod_w, mod_b, ln_g, ln_b, mlstm_w_in, mlstm_gate_b, mlstm_conv, mlstm_norm, mlstm_w_out, attn_w_in, attn_q_norm, attn_k_norm, attn_w_out, hgrn_w_in, hgrn_f_b, hgrn_lower_bounds, hgrn_norm, hgrn_w_out, moe_router, moe_router_b, moe_w_gate, moe_w_up, moe_w_down, moe_sh_gate, moe_sh_up, moe_sh_down):
    x = jnp.concatenate([x_prompt.reshape(BATCH * SEQ, D), x_sample.reshape(DEC_BATCH * DEC_SEQ, D)], axis=0)
    cond8 = jnp.concatenate([c_ctx[None], c, jnp.zeros((8 - 1 - DEC_BATCH, D), F32)], axis=0)
    mod4 = _mod_all(cond8, mod_w, mod_b)

    new_C, new_n, new_m, new_k, new_v, new_S = [], [], [], [], [], []
    for l in range(DEPTH):
        kind, j = l % 3, l // 3
        if kind == 0:
            a, (Cn, nn, mn) = _mlstm_layer(x, mod4, l, j, mlstm_w_in, mlstm_gate_b, mlstm_conv, mlstm_norm,
                                           state_mlstm_C, state_mlstm_n, state_mlstm_m)
            new_C.append(Cn)
            new_n.append(nn)
            new_m.append(mn)
            w_out = mlstm_w_out[j]
        elif kind == 1:
            a, (kn, vn) = _attn_layer(x, mod4, l, j, attn_w_in, attn_q_norm, attn_k_norm, cache_attn_k, cache_attn_v)
            new_k.append(kn)
            new_v.append(vn)
            w_out = attn_w_out[j]
        else:
            a, Sn = _hgrn_layer(x, mod4, l, j, hgrn_w_in, hgrn_f_b, hgrn_lower_bounds, hgrn_norm, state_hgrn_S)
            new_S.append(Sn)
            w_out = hgrn_w_out[j]
        x, hp, idx128, w128 = _post(a[0], a[1], x, mod4, l, w_out, ln_g[l, 0], ln_b[l, 0],
                                    moe_router[l], moe_router_b[l])
        routed = _moe_routed(hp, _route_tables(idx128, w128), l, moe_w_gate, moe_w_up, moe_w_down)
        x = _moe_post(x, hp, routed, mod4, l, moe_sh_gate[l], moe_sh_up[l], moe_sh_down[l], ln_g[l, 1], ln_b[l, 1])

    xp = x[:BATCH * SEQ].reshape(BATCH, SEQ, D)
    xs = x[BATCH * SEQ:].reshape(DEC_BATCH, DEC_SEQ, D)
    return (xp, xs, jnp.stack(new_C, 1), jnp.stack(new_n, 1), jnp.stack(new_m, 1),
            jnp.stack(new_k, 1), jnp.stack(new_v, 1), jnp.stack(new_S, 1))
```

```python
import functools

import jax
import jax.numpy as jnp
from jax import lax
from jax.experimental import pallas as pl
from jax.experimental.pallas import tpu as pltpu

F32 = jnp.float32
BF16 = jnp.bfloat16
I32 = jnp.int32
HI = lax.Precision.HIGHEST

D = 1024
BATCH, SEQ = 16, 256
DEPTH = 4
DEC_BATCH, DEC_SEQ = 2, 4096
PAST = 512
GRID_W = 64
GROUP = 4096
N_GROUPS = 3
T_ALL = N_GROUPS * GROUP

ML_H, ML_DK, ML_DV, ML_L = 4, 128, 256, 128
ML_QK, ML_V = ML_H * ML_DK, ML_H * ML_DV
AT_H, AT_KV, AT_HD, AT_G = 16, 4, 64, 4
HG_H, HG_DK, HG_DV, HG_L = 8, 128, 128, 64
HG_K, HG_V = HG_H * HG_DK, HG_H * HG_DV
N_EXP, TOP_K, D_EXP, D_SH = 64, 8, 256, 256
ROUTE_SCALE = 2.5
ALPHA = (2 * DEPTH) ** 0.25
EPS = 1e-6
ROPE_THETA = 10000.0

VMEM_LIMIT = 56 * 1024 * 1024
LANES = 128
PACK_ROWS = D // (2 * LANES)
AT_GW = AT_G * AT_HD
AT_KBLK = AT_H * AT_HD // AT_GW

MOE_NS = 2
MOE_G = T_ALL // MOE_NS
MOE_M = 256
MOE_NB = MOE_G * TOP_K // MOE_M + N_EXP
MOE_ROWS = MOE_NB * MOE_M
MOE_KEY = 16384
MOE_S = MOE_M + 8
MOE_U = 8
MOE_CLEAR = 256
SCAN_UNROLL = 4


def _cparams(sem):
    return pltpu.CompilerParams(dimension_semantics=sem, vmem_limit_bytes=VMEM_LIMIT)


def _sigmoid(x):
    return 1.0 / (1.0 + jnp.exp(-x))


def _silu(x):
    return x * _sigmoid(x)


def _log_sigmoid(x):
    return jnp.minimum(x, 0.0) - jnp.log(1.0 + jnp.exp(-jnp.abs(x)))


def _dot(a, b):
    return jnp.dot(a, b, preferred_element_type=F32)


def _dot_nt(a, b):
    return lax.dot_general(a, b, (((1,), (1,)), ((), ())), preferred_element_type=F32)


def _dot_tn(a, b):
    return lax.dot_general(a, b, (((0,), (0,)), ((), ())), preferred_element_type=F32)


def _dot_hi(a, b):
    return jnp.dot(a, b, preferred_element_type=F32, precision=HI)


def _split3(x):
    p0 = x.astype(BF16)
    r1 = x - p0.astype(F32)
    p1 = r1.astype(BF16)
    p2 = (r1 - p1.astype(F32)).astype(BF16)
    return p0, p1, p2


def _mask_dot(mask_bf16, x):
    p0, p1, p2 = _split3(x)
    return _dot(mask_bf16, p0) + _dot(mask_bf16, p1) + _dot(mask_bf16, p2)


def _dot_mask(x, mask_bf16):
    p0, p1, p2 = _split3(x)
    return _dot(p0, mask_bf16) + _dot(p1, mask_bf16) + _dot(p2, mask_bf16)


def _dot_3x(a, b):
    a0 = a.astype(BF16)
    a1 = (a - a0.astype(F32)).astype(BF16)
    b0 = b.astype(BF16)
    b1 = (b - b0.astype(F32)).astype(BF16)
    return _dot(a0, b0) + _dot(a0, b1) + _dot(a1, b0)


def _mod_kernel(cond_ref, w_ref, b_ref, o_ref):
    o_ref[...] = _dot_hi(_silu(cond_ref[...]), w_ref[...]) + b_ref[...]


def _mod_all(cond8, mod_w, mod_b):
    tn = 1024
    out = pl.pallas_call(
        _mod_kernel,
        grid=(DEPTH, 6 * D // tn),
        in_specs=[pl.BlockSpec((8, D), lambda l, j: (0, 0)),
                  pl.BlockSpec((None, D, tn), lambda l, j: (l, 0, j)),
                  pl.BlockSpec((None, 1, tn), lambda l, j: (l, 0, j))],
        out_specs=pl.BlockSpec((None, 8, tn), lambda l, j: (l, 0, j)),
        out_shape=jax.ShapeDtypeStruct((DEPTH, 8, 6 * D), F32),
        compiler_params=_cparams(("arbitrary", "arbitrary")),
        name="mod_rows",
    )(cond8, mod_w, mod_b.reshape(DEPTH, 1, 6 * D))
    return out.reshape(DEPTH, 8, 6, D)


def _proj_kernel(x_ref, mod_ref, w_ref, o_ref, h_scr):
    @pl.when(pl.program_id(1) == 0)
    def _():
        h_scr[...] = (x_ref[...] * (1.0 + mod_ref[1:2, :]) + mod_ref[0:1, :]).astype(BF16)

    o_ref[...] = _dot(h_scr[...], w_ref[...].astype(BF16)).astype(o_ref.dtype)


def _proj(x, mod4, layer, w3, widx, ncols, out_dtype, tm=2048, tn=512):
    T = x.shape[0]
    return pl.pallas_call(
        _proj_kernel,
        grid=(T // tm, ncols // tn),
        in_specs=[pl.BlockSpec((tm, D), lambda i, j: (i, 0)),
                  pl.BlockSpec((None, None, 6, D), lambda i, j: (layer, (i * tm) // GROUP, 0, 0)),
                  pl.BlockSpec((None, D, tn), lambda i, j: (widx, 0, j))],
        out_specs=pl.BlockSpec((tm, tn), lambda i, j: (i, j)),
        out_shape=jax.ShapeDtypeStruct((T, ncols), out_dtype),
        scratch_shapes=[pltpu.VMEM((tm, D), BF16)],
        compiler_params=_cparams(("arbitrary", "arbitrary")),
        name="in_proj",
    )(x, mod4, w3)


def _proj_small_kernel(x_ref, mod_ref, w_ref, b_ref, o_ref):
    h = x_ref[...] * (1.0 + mod_ref[1:2, :]) + mod_ref[0:1, :]
    o_ref[...] = _dot_3x(h, w_ref[...]) + b_ref[...]


def _proj_small(x, mod4, layer, w, b, tm=1024):
    T, n = x.shape[0], w.shape[1]
    return pl.pallas_call(
        _proj_small_kernel,
        grid=(T // tm,),
        in_specs=[pl.BlockSpec((tm, D), lambda i: (i, 0)),
                  pl.BlockSpec((None, None, 6, D), lambda i: (layer, (i * tm) // GROUP, 0, 0)),
                  pl.BlockSpec((D, n), lambda i: (0, 0)),
                  pl.BlockSpec((1, n), lambda i: (0, 0))],
        out_specs=pl.BlockSpec((tm, n), lambda i: (i, 0)),
        out_shape=jax.ShapeDtypeStruct((T, n), F32),
        compiler_params=_cparams(("arbitrary",)),
        name="gate_proj",
    )(x, mod4, w, b.reshape(1, n))


def _layer_norm_rows(r, g, b):
    mu = jnp.mean(r, axis=-1, keepdims=True)
    c = r - mu
    var = jnp.mean(c * c, axis=-1, keepdims=True)
    return c * lax.rsqrt(var + EPS) * g + b


def _pack_bf16_pairs(lo, hi):
    lo_b = lax.bitcast_convert_type(lo.astype(BF16).astype(F32), I32)
    hi_b = lax.bitcast_convert_type(hi.astype(BF16).astype(F32), I32)
    return lax.shift_right_logical(lo_b, 16) | (hi_b & jnp.int32(-65536))


def _unpack_bf16_pairs(v):
    lo = lax.bitcast_convert_type(lax.shift_left(v, 16), F32)
    hi = lax.bitcast_convert_type(v & jnp.int32(-65536), F32)
    return lo.astype(BF16), hi.astype(BF16)


def _unpack_rows(ref, rows, region_stride=None):
    chunks = []
    for r in range(PACK_ROWS):
        if region_stride is None:
            v = ref[pl.ds(r, rows, stride=PACK_ROWS), :]
        else:
            v = ref[pl.ds(r * region_stride, rows), :]
        chunks.extend(_unpack_bf16_pairs(v))
    return jnp.concatenate(chunks, axis=1)


def _post_kernel(ap_ref, as_ref, x_ref, mod_ref, w_ref, lng_ref, lnb_ref, rw_ref, rb_ref,
                 xo_ref, hp_ref, idx_ref, wt_ref, wb_scr, *, tm, n_ctx_tiles):
    @pl.when(pl.program_id(0) == 0)
    def _():
        wb_scr[...] = w_ref[...].astype(BF16)

    a = jnp.where(pl.program_id(0) < n_ctx_tiles, ap_ref[...], as_ref[...])
    y = _dot(a, wb_scr[...])
    r = ALPHA * x_ref[...] + mod_ref[2:3, :] * y
    xn = _layer_norm_rows(r, lng_ref[...], lnb_ref[...])
    xo_ref[...] = xn
    h1 = xn * (1.0 + mod_ref[4:5, :]) + mod_ref[3:4, :]
    for p in range(4):
        lo = h1[:, (2 * p) * LANES:(2 * p + 1) * LANES]
        hi = h1[:, (2 * p + 1) * LANES:(2 * p + 2) * LANES]
        hp_ref[pl.ds(p, tm, stride=4), :] = _pack_bf16_pairs(lo, hi)

    scores = _sigmoid(_dot_3x(h1, rw_ref[...]))
    sel = scores + rb_ref[...]
    e_iota = lax.broadcasted_iota(I32, (tm, N_EXP), 1).astype(F32)
    lane = lax.broadcasted_iota(I32, (tm, LANES), 1)
    idx_out = jnp.zeros((tm, LANES), F32)
    w_out = jnp.zeros((tm, LANES), F32)
    total = jnp.zeros((tm, 1), F32)
    for k in range(TOP_K):
        mx = jnp.max(sel, axis=-1, keepdims=True)
        ik = jnp.min(jnp.where(sel == mx, e_iota, float(N_EXP)), axis=-1, keepdims=True)
        hit = e_iota == ik
        wk = jnp.sum(jnp.where(hit, scores, 0.0), axis=-1, keepdims=True)
        sel = jnp.where(hit, -jnp.inf, sel)
        total = total + wk
        idx_out = jnp.where(lane == k, ik, idx_out)
        w_out = jnp.where(lane == k, wk, w_out)
    idx_ref[...] = idx_out.astype(I32)
    wt_ref[...] = w_out / total * ROUTE_SCALE


def _post(a_ctx, a_lat, x, mod4, layer, w_out, ln_g, ln_b, router_w, router_b, tm=512):
    T = x.shape[0]
    nc = a_ctx.shape[0] // tm
    return pl.pallas_call(
        functools.partial(_post_kernel, tm=tm, n_ctx_tiles=nc),
        grid=(T // tm,),
        in_specs=[pl.BlockSpec((tm, D), lambda i: (jnp.minimum(i, nc - 1), 0)),
                  pl.BlockSpec((tm, D), lambda i: (jnp.maximum(i - nc, 0), 0)),
                  pl.BlockSpec((tm, D), lambda i: (i, 0)),
                  pl.BlockSpec((None, None, 6, D), lambda i: (layer, (i * tm) // GROUP, 0, 0)),
                  pl.BlockSpec((D, D), lambda i: (0, 0)),
                  pl.BlockSpec((1, D), lambda i: (0, 0)),
                  pl.BlockSpec((1, D), lambda i: (0, 0)),
                  pl.BlockSpec((D, N_EXP), lambda i: (0, 0)),
                  pl.BlockSpec((1, N_EXP), lambda i: (0, 0))],
        out_specs=[pl.BlockSpec((tm, D), lambda i: (i, 0)),
                   pl.BlockSpec((tm * 4, LANES), lambda i: (i, 0)),
                   pl.BlockSpec((tm, LANES), lambda i: (i, 0)),
                   pl.BlockSpec((tm, LANES), lambda i: (i, 0))],
        out_shape=[jax.ShapeDtypeStruct((T, D), F32),
                   jax.ShapeDtypeStruct((T * 4, LANES), I32),
                   jax.ShapeDtypeStruct((T, LANES), I32),
                   jax.ShapeDtypeStruct((T, LANES), F32)],
        scratch_shapes=[pltpu.VMEM((D, D), BF16)],
        compiler_params=_cparams(("arbitrary",)),
        name="out_proj_ln_router",
    )(a_ctx, a_lat, x, mod4, w_out, ln_g.reshape(1, D), ln_b.reshape(1, D), router_w, router_b.reshape(1, N_EXP))


def _route_tables(idx128, w128):
    idx = idx128[:, :TOP_K].reshape(MOE_NS, MOE_G, TOP_K)
    w = w128[:, :TOP_K].reshape(MOE_NS, MOE_G, TOP_K)
    onehot = idx[..., None] == jnp.arange(N_EXP, dtype=I32)
    counts = jnp.sum(onehot.astype(I32), axis=(1, 2))
    padded = (counts + MOE_M - 1) // MOE_M * MOE_M
    pad_end = jnp.cumsum(padded, axis=-1)
    tok = jnp.arange(MOE_G, dtype=I32)[None, :, None]
    real_keys = (idx * MOE_KEY + tok).reshape(MOE_NS, MOE_G * TOP_K)
    fill = jnp.arange(MOE_M, dtype=I32)[None, None, :]
    e_ids = jnp.arange(N_EXP, dtype=I32)[None, :, None]
    fill_keys = jnp.where(fill < (padded - counts)[:, :, None], e_ids * MOE_KEY + MOE_KEY // 2 + fill,
                          N_EXP * MOE_KEY + e_ids * MOE_M + fill).reshape(MOE_NS, N_EXP * MOE_M)
    keys = jnp.concatenate([real_keys, fill_keys], axis=1)
    vals = jnp.concatenate([w.reshape(MOE_NS, MOE_G * TOP_K), jnp.zeros((MOE_NS, N_EXP * MOE_M), F32)], axis=1)
    sorted_rows = [lax.sort((keys[s], vals[s]), dimension=0, num_keys=1) for s in range(MOE_NS)]
    keys = jnp.stack([k for k, _ in sorted_rows])
    row_w = jnp.stack([v for _, v in sorted_rows])
    row_tok = jnp.where(keys < N_EXP * MOE_KEY, jnp.minimum(keys & (MOE_KEY - 1), MOE_G), MOE_G)
    row_tok = row_tok.reshape(-1)
    n_used = pad_end[:, -1] // MOE_M
    starts = jnp.arange(MOE_NB, dtype=I32) * MOE_M
    block_e = jnp.sum((starts[None, :, None] >= pad_end[:, None, :]).astype(I32), axis=-1)
    block_e = jnp.minimum(block_e, N_EXP - 1)
    last_e = jnp.take_along_axis(block_e, jnp.maximum(n_used - 1, 0)[:, None], axis=1)
    block_e = jnp.where(jnp.arange(MOE_NB, dtype=I32)[None, :] < n_used[:, None], block_e, last_e)
    return (block_e.reshape(-1).astype(I32), n_used.astype(I32), row_tok,
            row_w.reshape(MOE_NS * MOE_NB, 1, MOE_M))


def _cast_kernel(x_ref, o_ref):
    o_ref[...] = x_ref[...].astype(BF16)


def _cast_experts(w4, layer, eb=4):
    _, n_e, a, b = w4.shape
    return pl.pallas_call(
        _cast_kernel,
        grid=(n_e // eb,),
        in_specs=[pl.BlockSpec((None, eb, a, b), lambda i: (layer, i, 0, 0))],
        out_specs=pl.BlockSpec((eb, a, b), lambda i: (i, 0, 0)),
        out_shape=jax.ShapeDtypeStruct((n_e, a, b), BF16),
        compiler_params=_cparams(("arbitrary",)),
        name="cast_experts",
    )(w4)


def _moe_stage(tg_ref, ts_ref, x_v, acc, rw_ref, wg_ref, wu_ref, wd_ref, tile_g, tile_c, ys_c, ys_s):
    for mi in range(MOE_M):
        tile_g[pl.ds(mi, 4, stride=MOE_S), :] = x_v[pl.ds(pl.multiple_of(tg_ref[0, mi], 4), 4), :]
    xb = _unpack_rows(tile_c, MOE_M, MOE_S)
    g = _dot(xb, wg_ref[...])
    u = _dot(xb, wu_ref[...])
    ri = lax.broadcasted_iota(I32, (MOE_M, MOE_M), 0)
    ci = lax.broadcasted_iota(I32, (MOE_M, MOE_M), 1)
    rw_col = jnp.sum(jnp.where(ri == ci, jnp.broadcast_to(rw_ref[...], (MOE_M, MOE_M)), 0.0),
                     axis=1, keepdims=True)
    a = (_silu(g) * u) * rw_col
    y = _dot(a.astype(BF16), wd_ref[...])
    for c in range(D // LANES):
        ys_c[pl.ds(c * MOE_S, MOE_M), :] = y[:, c * LANES:(c + 1) * LANES]
    for m0 in range(0, MOE_M, MOE_U):
        offs = [pl.multiple_of(ts_ref[0, m0 + j], 8) for j in range(MOE_U)]
        vals = [acc[pl.ds(offs[j], 8), :] + ys_s[pl.ds(m0 + j, 8, stride=MOE_S), :] for j in range(MOE_U)]
        for j in range(MOE_U):
            acc[pl.ds(offs[j], 8), :] = vals[j]


def _moe_kernel(be_ref, nu_ref, tg_ref, ts_ref, x_hbm, rw_ref, wg_ref, wu_ref, wd_ref, out_hbm,
                x_v, acc, tile_a, tile_b, ys_a, ys_b, sem):
    s = pl.program_id(0)
    j = pl.program_id(1)

    @pl.when(j == 0)
    def _():
        cp = pltpu.make_async_copy(x_hbm.at[s], x_v.at[pl.ds(0, MOE_G * 4)], sem.at[0])
        cp.start()
        cp.wait()
        x_v[pl.ds(MOE_G * 4, 8), :] = jnp.zeros((8, LANES), I32)

        def clear(i, carry):
            acc[pl.ds(pl.multiple_of(i * MOE_CLEAR, MOE_CLEAR), MOE_CLEAR), :] = jnp.zeros((MOE_CLEAR, LANES), F32)
            return carry

        lax.fori_loop(0, MOE_G * 8 // MOE_CLEAR, clear, 0)
        acc[pl.ds(MOE_G * 8, 8), :] = jnp.zeros((8, LANES), F32)
        for t in (tile_a, tile_b):
            t[...] = jnp.zeros(t.shape, I32)
        for y in (ys_a, ys_b):
            y[...] = jnp.zeros(y.shape, F32)

    live = j < nu_ref[s] + 2
    common = (tg_ref, ts_ref, x_v, acc, rw_ref, wg_ref, wu_ref, wd_ref)

    @pl.when(live & (j % 2 == 0))
    def _():
        _moe_stage(*common, tile_a, tile_b, ys_b, ys_a)

    @pl.when(live & (j % 2 == 1))
    def _():
        _moe_stage(*common, tile_b, tile_a, ys_a, ys_b)

    @pl.when(j == MOE_NB + 1)
    def _():
        cp = pltpu.make_async_copy(acc.at[pl.ds(0, MOE_G * 8)], out_hbm.at[s], sem.at[1])
        cp.start()
        cp.wait()


def _moe_routed(hp, tables, layer, w_gate, w_up, w_down):
    block_e, n_used, row_tok, row_w = tables
    x3 = hp.reshape(MOE_NS, MOE_G * 4, LANES)
    blk = lambda s, j, d: s * MOE_NB + jnp.clip(j - d, 0, MOE_NB - 1)
    wspec = lambda shape: pl.BlockSpec((None,) + shape, lambda s, j, be, nu: (be[blk(s, j, 1)], 0, 0))
    tspec = lambda d: pl.BlockSpec((None, 1, MOE_M), lambda s, j, be, nu: (blk(s, j, d), 0, 0),
                                   memory_space=pltpu.SMEM)
    tok3 = row_tok.reshape(MOE_NS * MOE_NB, 1, MOE_M)
    out = pl.pallas_call(
        _moe_kernel,
        grid_spec=pltpu.PrefetchScalarGridSpec(
            num_scalar_prefetch=2,
            grid=(MOE_NS, MOE_NB + 2),
            in_specs=[tspec(0), tspec(2),
                      pl.BlockSpec(memory_space=pl.ANY),
                      pl.BlockSpec((None, 1, MOE_M), lambda s, j, be, nu: (blk(s, j, 1), 0, 0)),
                      wspec((D, D_EXP)), wspec((D, D_EXP)), wspec((D_EXP, D))],
            out_specs=pl.BlockSpec(memory_space=pl.ANY),
            scratch_shapes=[pltpu.VMEM((MOE_G * 4 + 8, LANES), I32),
                            pltpu.VMEM(((MOE_G + 1) * 8, LANES), F32),
                            pltpu.VMEM((4 * MOE_S, LANES), I32),
                            pltpu.VMEM((4 * MOE_S, LANES), I32),
                            pltpu.VMEM((8 * MOE_S, LANES), F32),
                            pltpu.VMEM((8 * MOE_S, LANES), F32),
                            pltpu.SemaphoreType.DMA((2,))]),
        out_shape=jax.ShapeDtypeStruct((MOE_NS, MOE_G * 8, LANES), F32),
        compiler_params=_cparams(("arbitrary", "arbitrary")),
        name="moe_routed",
    )(block_e, n_used, tok3 * 4, tok3 * 8, x3, row_w,
      _cast_experts(w_gate, layer), _cast_experts(w_up, layer), _cast_experts(w_down, layer))
    return out.reshape(T_ALL * 8, LANES)


def _moe_post_kernel(x_ref, hp_ref, r_ref, mod_ref, sg_ref, su_ref, sd_ref, lng_ref, lnb_ref, o_ref,
                     sgb, sub, sdb, *, tm):
    @pl.when(pl.program_id(0) == 0)
    def _():
        sgb[...] = sg_ref[...].astype(BF16)
        sub[...] = su_ref[...].astype(BF16)
        sdb[...] = sd_ref[...].astype(BF16)

    hb = _unpack_rows(hp_ref, tm)
    g = _dot(hb, sgb[...])
    u = _dot(hb, sub[...])
    sh = _dot((_silu(g) * u).astype(BF16), sdb[...])
    routed = jnp.concatenate([r_ref[pl.ds(c, tm, stride=8), :] for c in range(D // LANES)], axis=1)
    r = ALPHA * x_ref[...] + mod_ref[5:6, :] * (routed + sh)
    o_ref[...] = _layer_norm_rows(r, lng_ref[...], lnb_ref[...])


def _moe_post(x, hp, routed, mod4, layer, sg, su, sd, ln_g, ln_b, tm=512):
    T = x.shape[0]
    return pl.pallas_call(
        functools.partial(_moe_post_kernel, tm=tm),
        grid=(T // tm,),
        in_specs=[pl.BlockSpec((tm, D), lambda i: (i, 0)),
                  pl.BlockSpec((tm * 4, LANES), lambda i: (i, 0)),
                  pl.BlockSpec((tm * 8, LANES), lambda i: (i, 0)),
                  pl.BlockSpec((None, None, 6, D), lambda i: (layer, (i * tm) // GROUP, 0, 0)),
                  pl.BlockSpec((D, D_SH), lambda i: (0, 0)),
                  pl.BlockSpec((D, D_SH), lambda i: (0, 0)),
                  pl.BlockSpec((D_SH, D), lambda i: (0, 0)),
                  pl.BlockSpec((1, D), lambda i: (0, 0)),
                  pl.BlockSpec((1, D), lambda i: (0, 0))],
        out_specs=pl.BlockSpec((tm, D), lambda i: (i, 0)),
        out_shape=jax.ShapeDtypeStruct((T, D), F32),
        scratch_shapes=[pltpu.VMEM((D, D_SH), BF16), pltpu.VMEM((D, D_SH), BF16), pltpu.VMEM((D_SH, D), BF16)],
        compiler_params=_cparams(("arbitrary",)),
        name="shared_expert_ln",
    )(x, hp, routed, mod4, sg, su, sd, ln_g.reshape(1, D), ln_b.reshape(1, D))


def _conv_silu(src_ref, w_ref, dst_ref, T, scale):
    L = ML_L
    nc = T // L
    width = src_ref.shape[1]
    w0, w1, w2 = w_ref[0:1, :], w_ref[1:2, :], w_ref[2:3, :]
    row = lax.broadcasted_iota(I32, (L, width), 0)

    def body(c, carry):
        r0 = pl.multiple_of(c * L, L)
        cur = src_ref[pl.ds(r0, L), :].astype(F32)
        p0 = pl.multiple_of(jnp.maximum(r0 - 16, 0), 16)
        n0 = pl.multiple_of(jnp.minimum(r0 + L, T - 16), 16)
        prev_row = src_ref[pl.ds(p0, 16), :].astype(F32)[15:16, :] * jnp.where(c > 0, 1.0, 0.0).astype(F32)
        next_row = src_ref[pl.ds(n0, 16), :].astype(F32)[0:1, :] * jnp.where(c < nc - 1, 1.0, 0.0).astype(F32)
        prev = jnp.where(row == 0, prev_row, pltpu.roll(cur, 1, axis=0))
        nxt = jnp.where(row == L - 1, next_row, pltpu.roll(cur, L - 1, axis=0))
        dst_ref[pl.ds(r0, L), :] = _silu(w0 * prev + w1 * cur + w2 * nxt) * scale
        return carry

    lax.fori_loop(0, nc, body, 0)


def _mlstm_chunk(q, k, v, G, GT, C, n, m, backward):
    L = ML_L
    ri = lax.broadcasted_iota(I32, (L, L), 0)
    ci = lax.broadcasted_iota(I32, (L, L), 1)
    keep = (ci >= ri) if backward else (ci <= ri)
    A = keep.astype(BF16)
    AT = ((ri >= ci) if backward else (ri <= ci)).astype(BF16)
    ic, fc = (2, 3) if backward else (0, 1)
    last = 0 if backward else L - 1
    b_col = _mask_dot(A, _log_sigmoid(G))[:, fc:fc + 1]
    b_row = _dot_mask(_log_sigmoid(GT), AT)[fc:fc + 1, :]
    ig_col = G[:, ic:ic + 1]
    ig_row = GT[ic:ic + 1, :]
    log_d = jnp.where(keep, b_col - b_row + ig_row, -jnp.inf)
    log_inter = b_col + m
    m_t = jnp.maximum(log_inter, jnp.max(log_d, axis=-1, keepdims=True))
    d = jnp.exp(log_d - m_t)
    w_inter = jnp.exp(log_inter - m_t)
    qb, kb, vb = q.astype(BF16), k.astype(BF16), v.astype(BF16)
    s = _dot_nt(qb, kb) * d
    num = _dot(s.astype(BF16), vb) + w_inter * _dot(qb, C.astype(BF16))
    den = jnp.sum(s, axis=-1, keepdims=True) + w_inter * jnp.sum(q * n, axis=-1, keepdims=True)
    h = num / jnp.maximum(jnp.abs(den), jnp.exp(-m_t))
    m_new = m_t[last:last + 1, :]
    w_last = jnp.exp(b_col[last:last + 1, :] - b_col + ig_col - m_new)
    decay = w_inter[last:last + 1, :]
    kw = k * w_last
    C_new = decay * C + _dot_tn(kw.astype(BF16), vb)
    n_new = decay * n + jnp.sum(kw, axis=0, keepdims=True)
    return h, C_new, n_new, m_new


def _mlstm_kernel(q_ref, k_ref, v_ref, og_ref, g_ref, gt_ref, cq_ref, ck_ref, ng_ref, c0_ref, n0_ref, m0_ref,
                  a_ref, c_out, n_out, m_out, qs, ks, hf, hb, cst, nst, mst, *, T, nh):
    L = ML_L
    nc = T // L
    _conv_silu(q_ref, cq_ref, qs, T, ML_DK ** -0.5)
    _conv_silu(k_ref, ck_ref, ks, T, 1.0)
    cst[...] = c0_ref[...]
    nst[...] = n0_ref[...]
    mst[...] = m0_ref[...]
    un = min(SCAN_UNROLL, nc)

    def body(i, carry):
        for h in range(nh):
            kcols = slice(h * ML_DK, (h + 1) * ML_DK)
            vcols = slice(h * ML_DV, (h + 1) * ML_DV)
            for direction, out in ((0, hf), (1, hb)):
                C, n, m = cst[direction, h], nst[direction, h], mst[direction, h]
                for u in range(un):
                    ci = i * un + u
                    c = (nc - 1 - ci) if direction else ci
                    r0 = pl.multiple_of(c * L, L)
                    hh, C, n, m = _mlstm_chunk(
                        qs[pl.ds(r0, L), kcols], ks[pl.ds(r0, L), kcols], v_ref[pl.ds(r0, L), vcols],
                        g_ref[h, pl.ds(r0, L), :], gt_ref[h, c], C, n, m, backward=bool(direction))
                    out[pl.ds(r0, L), vcols] = hh
                cst[direction, h] = C
                nst[direction, h] = n
                mst[direction, h] = m
        return carry

    lax.fori_loop(0, nc // un, body, 0)
    c_out[...] = cst[...]
    n_out[...] = nst[...]
    m_out[...] = mst[...]

    def finish(c, carry):
        r0 = pl.multiple_of(c * L, L)
        for h in range(nh):
            vcols = slice(h * ML_DV, (h + 1) * ML_DV)
            tot = hf[pl.ds(r0, L), vcols] + hb[pl.ds(r0, L), vcols]
            mu = jnp.mean(tot, axis=-1, keepdims=True)
            cen = tot - mu
            var = jnp.mean(cen * cen, axis=-1, keepdims=True)
            hn = cen * lax.rsqrt(var + EPS) * ng_ref[:, vcols]
            a_ref[pl.ds(r0, L), vcols] = (hn * _sigmoid(og_ref[pl.ds(r0, L), vcols].astype(F32))).astype(BF16)
        return carry

    lax.fori_loop(0, nc, finish, 0)


def _mlstm_scan(z, gh, ght, conv_w, norm_g, C0, n0, m0, T, row_blk0, nseq, nh):
    nc = T // ML_L
    kw, vw = nh * ML_DK, nh * ML_DV
    qcol, kcol = 0, ML_QK // kw
    vcol, ocol = 2 * ML_QK // vw, (2 * ML_QK + ML_V) // vw
    rb = lambda s: row_blk0 + s
    state = lambda *tail: pl.BlockSpec((None, 2, nh) + tail, lambda s, h: (s, 0, h) + (0,) * len(tail))
    return pl.pallas_call(
        functools.partial(_mlstm_kernel, T=T, nh=nh),
        grid=(nseq, ML_H // nh),
        in_specs=[pl.BlockSpec((T, kw), lambda s, h: (rb(s), qcol + h)),
                  pl.BlockSpec((T, kw), lambda s, h: (rb(s), kcol + h)),
                  pl.BlockSpec((T, vw), lambda s, h: (rb(s), vcol + h)),
                  pl.BlockSpec((T, vw), lambda s, h: (rb(s), ocol + h)),
                  pl.BlockSpec((nh, T, 4), lambda s, h: (h, rb(s), 0)),
                  pl.BlockSpec((nh, nc, 4, ML_L), lambda s, h: (h, rb(s), 0, 0)),
                  pl.BlockSpec((3, kw), lambda s, h: (0, qcol + h)),
                  pl.BlockSpec((3, kw), lambda s, h: (0, kcol + h)),
                  pl.BlockSpec((1, vw), lambda s, h: (0, h)),
                  state(ML_DK, ML_DV), state(1, ML_DK), state(1, 1)],
        out_specs=[pl.BlockSpec((T, vw), lambda s, h: (s, h)),
                   state(ML_DK, ML_DV), state(1, ML_DK), state(1, 1)],
        out_shape=[jax.ShapeDtypeStruct((nseq * T, ML_V), BF16),
                   jax.ShapeDtypeStruct((nseq, 2, ML_H, ML_DK, ML_DV), F32),
                   jax.ShapeDtypeStruct((nseq, 2, ML_H, 1, ML_DK), F32),
                   jax.ShapeDtypeStruct((nseq, 2, ML_H, 1, 1), F32)],
        scratch_shapes=[pltpu.VMEM((T, kw), F32), pltpu.VMEM((T, kw), F32),
                        pltpu.VMEM((T, vw), F32), pltpu.VMEM((T, vw), F32),
                        pltpu.VMEM((2, nh, ML_DK, ML_DV), F32), pltpu.VMEM((2, nh, 1, ML_DK), F32),
                        pltpu.VMEM((2, nh, 1, 1), F32)],
        compiler_params=_cparams(("arbitrary", "arbitrary")),
        name="mlstm_scan",
    )(z, z, z, z, gh, ght, conv_w, conv_w, norm_g.reshape(1, ML_V), C0, n0, m0)


def _mlstm_layer(x, mod4, layer, j, w_in, gate_b, conv_w, norm_g, C_lat, n_lat, m_lat):
    z = _proj(x, mod4, layer, w_in, j, 2 * ML_QK + 2 * ML_V, BF16)
    gates = _proj_small(x, mod4, layer, w_in[j][:, 2 * ML_QK + 2 * ML_V:], gate_b[j])
    gh = gates.reshape(T_ALL, 4, ML_H).transpose(2, 0, 1)
    ght = gh.reshape(ML_H, T_ALL // ML_L, ML_L, 4).transpose(0, 1, 3, 2)
    zC = jnp.zeros((BATCH, 2, ML_H, ML_DK, ML_DV), F32)
    zn = jnp.zeros((BATCH, 2, ML_H, 1, ML_DK), F32)
    zm = jnp.zeros((BATCH, 2, ML_H, 1, 1), F32)
    a_p, Cn, nn, mn = _mlstm_scan(z, gh, ght, conv_w[j], norm_g[j], zC, zn, zm, SEQ, 0, BATCH, ML_H)
    a_s, _, _, _ = _mlstm_scan(z, gh, ght, conv_w[j], norm_g[j], C_lat[:, j],
                               n_lat[:, j].reshape(DEC_BATCH, 2, ML_H, 1, ML_DK),
                               m_lat[:, j].reshape(DEC_BATCH, 2, ML_H, 1, 1),
                               DEC_SEQ, BATCH * SEQ // DEC_SEQ, DEC_BATCH, 1)
    return (a_p, a_s), (Cn, nn.reshape(BATCH, 2, ML_H, ML_DK), mn.reshape(BATCH, 2, ML_H))


def _hgrn_chunk(q, k, v, g, St, backward):
    L = HG_L
    ri = lax.broadcasted_iota(I32, (L, L), 0)
    ci = lax.broadcasted_iota(I32, (L, L), 1)
    keep = (ci >= ri) if backward else (ci <= ri)
    ref = L - 1 - L // 2 if backward else L // 2
    last = 0 if backward else L - 1
    b = _mask_dot(keep.astype(BF16), g)
    b_ref = b[ref:ref + 1, :]
    b_last = b[last:last + 1, :]
    qe = (q * jnp.exp(b - b_ref)).astype(BF16)
    ke = (k * jnp.exp(b_ref - b)).astype(BF16)
    vb = v.astype(BF16)
    a = jnp.where(keep, _dot_nt(qe, ke), 0.0)
    o = _dot(a.astype(BF16), vb) + _dot_nt((q * jnp.exp(b)).astype(BF16), St.astype(BF16))
    kd = (k * jnp.exp(b_last - b)).astype(BF16)
    St_new = jnp.exp(b_last) * St + _dot_tn(vb, kd)
    return o, St_new


def _hgrn_kernel(q_ref, i_ref, ff_ref, fb_ref, og_ref, fbf_ref, fbb_ref, lbr_ref, ng_ref, s0_ref,
                 a_ref, s_out, of, ob, st, *, T, lb_layer, nh):
    L = HG_L
    nc = T // L
    raw = lbr_ref[...]
    e = jnp.exp(raw - jnp.max(raw, axis=0, keepdims=True))
    p = e / jnp.sum(e, axis=0, keepdims=True)
    lb_all = jnp.sum(p[0:lb_layer + 1, :], axis=0, keepdims=True) - p[0:1, :]
    for h in range(nh):
        st[0, h] = s0_ref[0, h].T
        st[1, h] = s0_ref[1, h].T
    un = min(SCAN_UNROLL, nc)

    def body(i, carry):
        for h in range(nh):
            cols = slice(h * HG_DK, (h + 1) * HG_DK)
            lb = lb_all[:, cols]
            for direction, out, f_ref, b_ref in ((0, of, ff_ref, fbf_ref), (1, ob, fb_ref, fbb_ref)):
                St = st[direction, h]
                for u in range(un):
                    ci = i * un + u
                    c = (nc - 1 - ci) if direction else ci
                    r0 = pl.multiple_of(c * L, L)
                    f = lb + (1.0 - lb) * _sigmoid(f_ref[pl.ds(r0, L), cols] + b_ref[:, cols])
                    o, St = _hgrn_chunk(_silu(q_ref[pl.ds(r0, L), cols]), 1.0 - f, i_ref[pl.ds(r0, L), cols],
                                        jnp.log(f), St, backward=bool(direction))
                    out[pl.ds(r0, L), cols] = o
                st[direction, h] = St
        return carry

    lax.fori_loop(0, nc // un, body, 0)
    for h in range(nh):
        s_out[0, h] = st[0, h].T
        s_out[1, h] = st[1, h].T

    def finish(c, carry):
        r0 = pl.multiple_of(c * L, L)
        for h in range(nh):
            cols = slice(h * HG_DV, (h + 1) * HG_DV)
            tot = of[pl.ds(r0, L), cols] + ob[pl.ds(r0, L), cols]
            on = tot * lax.rsqrt(jnp.mean(tot * tot, axis=-1, keepdims=True) + EPS) * ng_ref[:, cols]
            a_ref[pl.ds(r0, L), cols] = (on * _silu(og_ref[pl.ds(r0, L), cols])).astype(BF16)
        return carry

    lax.fori_loop(0, nc, finish, 0)


def _hgrn_scan(z, f_b, lb_raw, norm_g, S0, T, row_blk0, nseq, lb_layer, nh):
    nhb = HG_H // nh
    w = nh * HG_DK
    rb = lambda s: row_blk0 + s
    zspec = lambda cb: pl.BlockSpec((T, w), lambda s, h: (rb(s), cb * nhb + h))
    sspec = pl.BlockSpec((None, 2, nh, HG_DK, HG_DV), lambda s, h: (s, 0, h, 0, 0))
    return pl.pallas_call(
        functools.partial(_hgrn_kernel, T=T, lb_layer=lb_layer, nh=nh),
        grid=(nseq, nhb),
        in_specs=[zspec(0), zspec(1), zspec(2), zspec(3), zspec(4),
                  pl.BlockSpec((1, w), lambda s, h: (0, h)),
                  pl.BlockSpec((1, w), lambda s, h: (0, nhb + h)),
                  pl.BlockSpec((DEPTH, w), lambda s, h: (0, h)),
                  pl.BlockSpec((1, w), lambda s, h: (0, h)),
                  sspec],
        out_specs=[pl.BlockSpec((T, w), lambda s, h: (s, h)), sspec],
        out_shape=[jax.ShapeDtypeStruct((nseq * T, HG_V), BF16),
                   jax.ShapeDtypeStruct((nseq, 2, HG_H, HG_DK, HG_DV), F32)],
        scratch_shapes=[pltpu.VMEM((T, w), F32), pltpu.VMEM((T, w), F32),
                        pltpu.VMEM((2, nh, HG_DV, HG_DK), F32)],
        compiler_params=_cparams(("arbitrary", "arbitrary")),
        name="hgrn_scan",
    )(z, z, z, z, z, f_b.reshape(1, 2 * HG_K), f_b.reshape(1, 2 * HG_K), lb_raw, norm_g.reshape(1, HG_V), S0)


def _hgrn_layer(x, mod4, layer, j, w_in, f_b, lb_raw, norm_g, S_lat):
    z = _proj(x, mod4, layer, w_in, j, 3 * HG_K + 2 * HG_V, F32)
    zS = jnp.zeros((BATCH, 2, HG_H, HG_DK, HG_DV), F32)
    a_p, Sn = _hgrn_scan(z, f_b[j], lb_raw, norm_g[j], zS, SEQ, 0, BATCH, layer, 4)
    a_s, _ = _hgrn_scan(z, f_b[j], lb_raw, norm_g[j], S_lat[:, j], DEC_SEQ, BATCH * SEQ // DEC_SEQ, DEC_BATCH,
                        layer, 1)
    return (a_p, a_s), Sn


def _head_rms(x, g_tiled, nheads):
    lane = lax.broadcasted_iota(I32, x.shape, 1)
    sq = x * x
    ms = jnp.zeros_like(x)
    for h in range(nheads):
        in_h = (lane >= h * AT_HD) & (lane < (h + 1) * AT_HD)
        tot = jnp.sum(jnp.where(in_h, sq, 0.0), axis=-1, keepdims=True)
        ms = jnp.where(in_h, tot, ms)
    return x * lax.rsqrt(ms * (1.0 / AT_HD) + EPS) * g_tiled


def _rope(x, cos, sin):
    w = x.shape[1]
    lane = lax.broadcasted_iota(I32, x.shape, 1)
    up = pltpu.roll(x, w - 16, axis=1)
    down = pltpu.roll(x, 16, axis=1)
    swapped = jnp.where((lane % 32) < 16, up, down)
    return x * cos + swapped * sin


def _attn_ctx_kernel(q_ref, k_ref, v_ref, qn_ref, kn_ref, o_ref, ko_ref, vo_ref):
    k = _head_rms(k_ref[...].astype(F32), kn_ref[...], AT_KV)
    ko_ref[...] = k
    v = v_ref[...].astype(F32)
    vo_ref[...] = v
    kb, vb = k.astype(BF16), v.astype(BF16)
    outs = []
    for kv in range(AT_KV):
        q = _head_rms(q_ref[:, kv * AT_GW:(kv + 1) * AT_GW].astype(F32), qn_ref[...], AT_G) * (AT_HD ** -0.5)
        kh = kb[:, kv * AT_HD:(kv + 1) * AT_HD]
        vh = vb[:, kv * AT_HD:(kv + 1) * AT_HD]
        for g in range(AT_G):
            s = _dot_nt(q[:, g * AT_HD:(g + 1) * AT_HD].astype(BF16), kh)
            p = jnp.exp(s - jnp.max(s, axis=-1, keepdims=True))
            p = p / jnp.sum(p, axis=-1, keepdims=True)
            outs.append(_dot(p.astype(BF16), vh))
    o_ref[...] = jnp.concatenate(outs, axis=1).astype(BF16)


def _attn_ctx(z, q_norm, k_norm):
    qn = jnp.tile(q_norm, AT_G).reshape(1, AT_GW)
    kn = jnp.tile(k_norm, AT_KV).reshape(1, AT_GW)
    return pl.pallas_call(
        _attn_ctx_kernel,
        grid=(BATCH,),
        in_specs=[pl.BlockSpec((SEQ, D), lambda b: (b, 0)),
                  pl.BlockSpec((SEQ, AT_GW), lambda b: (b, AT_KBLK)),
                  pl.BlockSpec((SEQ, AT_GW), lambda b: (b, AT_KBLK + 1)),
                  pl.BlockSpec((1, AT_GW), lambda b: (0, 0)),
                  pl.BlockSpec((1, AT_GW), lambda b: (0, 0))],
        out_specs=[pl.BlockSpec((SEQ, D), lambda b: (b, 0)),
                   pl.BlockSpec((SEQ, AT_GW), lambda b: (b, 0)),
                   pl.BlockSpec((SEQ, AT_GW), lambda b: (b, 0))],
        out_shape=[jax.ShapeDtypeStruct((BATCH * SEQ, D), BF16),
                   jax.ShapeDtypeStruct((BATCH * SEQ, AT_GW), F32),
                   jax.ShapeDtypeStruct((BATCH * SEQ, AT_GW), F32)],
        compiler_params=_cparams(("arbitrary",)),
        name="attn_context",
    )(z, z, z, qn, kn)


def _attn_kv_kernel(k_ref, v_ref, kn_ref, cos_ref, sin_ref, ko_ref, vo_ref):
    k = _rope(_head_rms(k_ref[...].astype(F32), kn_ref[...], AT_KV), cos_ref[...], sin_ref[...])
    v = v_ref[...]
    for h in range(AT_KV):
        ko_ref[h] = k[:, h * AT_HD:(h + 1) * AT_HD].astype(BF16)
        vo_ref[h] = v[:, h * AT_HD:(h + 1) * AT_HD].astype(BF16)


def _attn_kv(z, k_norm, cos4, sin4, tt=512):
    kn = jnp.tile(k_norm, AT_KV).reshape(1, AT_GW)
    nt = DEC_SEQ // tt
    row0 = BATCH * SEQ // tt
    return pl.pallas_call(
        _attn_kv_kernel,
        grid=(DEC_BATCH, nt),
        in_specs=[pl.BlockSpec((tt, AT_GW), lambda b, i: (row0 + b * nt + i, AT_KBLK)),
                  pl.BlockSpec((tt, AT_GW), lambda b, i: (row0 + b * nt + i, AT_KBLK + 1)),
                  pl.BlockSpec((1, AT_GW), lambda b, i: (0, 0)),
                  pl.BlockSpec((tt, AT_GW), lambda b, i: (i, 0)),
                  pl.BlockSpec((tt, AT_GW), lambda b, i: (i, 0))],
        out_specs=[pl.BlockSpec((None, AT_KV, tt, AT_HD), lambda b, i: (b, 0, i, 0)),
                   pl.BlockSpec((None, AT_KV, tt, AT_HD), lambda b, i: (b, 0, i, 0))],
        out_shape=[jax.ShapeDtypeStruct((DEC_BATCH, AT_KV, DEC_SEQ, AT_HD), BF16),
                   jax.ShapeDtypeStruct((DEC_BATCH, AT_KV, DEC_SEQ, AT_HD), BF16)],
        compiler_params=_cparams(("arbitrary", "arbitrary")),
        name="attn_kv_prep",
    )(z, z, kn, cos4, sin4)


def _attn_lat_kernel(q_ref, qn_ref, cos_ref, sin_ref, k_ref, v_ref, o_ref, *, tq, tk):
    q = _rope(_head_rms(q_ref[...].astype(F32), qn_ref[...], AT_G), cos_ref[...], sin_ref[...]) * (AT_HD ** -0.5)
    qs = jnp.concatenate([q[:, g * AT_HD:(g + 1) * AT_HD] for g in range(AT_G)], axis=0).astype(BF16)
    rows = AT_G * tq
    nk = k_ref.shape[0] // tk

    def body(j, carry):
        m, l, acc = carry
        k0 = pl.multiple_of(j * tk, tk)
        s = _dot_nt(qs, k_ref[pl.ds(k0, tk), :])
        m_new = jnp.maximum(m, jnp.max(s, axis=-1, keepdims=True))
        alpha = jnp.exp(m - m_new)
        p = jnp.exp(s - m_new)
        l = alpha * l + jnp.sum(p, axis=-1, keepdims=True)
        acc = alpha * acc + _dot(p.astype(BF16), v_ref[pl.ds(k0, tk), :])
        return m_new, l, acc

    init = (jnp.full((rows, 1), -jnp.inf, F32), jnp.zeros((rows, 1), F32), jnp.zeros((rows, AT_HD), F32))
    _, l, acc = lax.fori_loop(0, nk, body, init)
    o = acc / l
    o_ref[...] = jnp.concatenate([o[g * tq:(g + 1) * tq, :] for g in range(AT_G)], axis=1).astype(BF16)


def _attn_lat(z, q_norm, cos4, sin4, kk, vv, tq=256, tk=1536):
    qn = jnp.tile(q_norm, AT_G).reshape(1, AT_GW)
    nq = DEC_SEQ // tq
    row0 = BATCH * SEQ // tq
    skv = kk.shape[2]
    return pl.pallas_call(
        functools.partial(_attn_lat_kernel, tq=tq, tk=tk),
        grid=(DEC_BATCH, AT_KV, nq),
        in_specs=[pl.BlockSpec((tq, AT_GW), lambda b, h, i: (row0 + b * nq + i, h)),
                  pl.BlockSpec((1, AT_GW), lambda b, h, i: (0, 0)),
                  pl.BlockSpec((tq, AT_GW), lambda b, h, i: (i, 0)),
                  pl.BlockSpec((tq, AT_GW), lambda b, h, i: (i, 0)),
                  pl.BlockSpec((None, None, skv, AT_HD), lambda b, h, i: (b, h, 0, 0)),
                  pl.BlockSpec((None, None, skv, AT_HD), lambda b, h, i: (b, h, 0, 0))],
        out_specs=pl.BlockSpec((tq, AT_GW), lambda b, h, i: (b * nq + i, h)),
        out_shape=jax.ShapeDtypeStruct((DEC_BATCH * DEC_SEQ, D), BF16),
        compiler_params=_cparams(("arbitrary", "arbitrary", "arbitrary")),
        name="attn_latent",
    )(z, qn, cos4, sin4, kk, vv)


def _rope_tables():
    t = jnp.arange(DEC_SEQ)
    row = (t // GRID_W).astype(F32)
    col = (t % GRID_W).astype(F32)
    nf = AT_HD // 4
    inv = ROPE_THETA ** (-jnp.arange(nf, dtype=F32) / nf)
    ar, ac = row[:, None] * inv[None], col[:, None] * inv[None]
    cos = jnp.concatenate([jnp.cos(ar), jnp.cos(ar), jnp.cos(ac), jnp.cos(ac)], axis=1)
    sin = jnp.concatenate([-jnp.sin(ar), jnp.sin(ar), -jnp.sin(ac), jnp.sin(ac)], axis=1)
    return jnp.tile(cos, (1, 4)), jnp.tile(sin, (1, 4))


def _attn_layer(x, mod4, layer, j, w_in, q_norm, k_norm, cache_k, cache_v):
    z = _proj(x, mod4, layer, w_in, j, (AT_H + 2 * AT_KV) * AT_HD, BF16)
    a_p, k_new, v_new = _attn_ctx(z, q_norm[j], k_norm[j])
    cos4, sin4 = _rope_tables()
    k_lat, v_lat = _attn_kv(z, k_norm[j], cos4, sin4)
    kk = jnp.concatenate([cache_k[:, j].transpose(0, 2, 1, 3).astype(BF16), k_lat], axis=2)
    vv = jnp.concatenate([cache_v[:, j].transpose(0, 2, 1, 3).astype(BF16), v_lat], axis=2)
    a_s = _attn_lat(z, q_norm[j], cos4, sin4, kk, vv)
    return (a_p, a_s), (k_new.reshape(BATCH, SEQ, AT_KV, AT_HD), v_new.reshape(BATCH, SEQ, AT_KV, AT_HD))


def kernel(x_prompt, x_sample, state_mlstm_C, state_mlstm_n, state_mlstm_m, cache_attn_k, cache_attn_v, state_hgrn_S, c, c_ctx, mod_w, mod_b, ln_g, ln_b, mlstm_w_in, mlstm_gate_b, mlstm_conv, mlstm_norm, mlstm_w_out, attn_w_in, attn_q_norm, attn_k_norm, attn_w_out, hgrn_w_in, hgrn_f_b, hgrn_lower_bounds, hgrn_norm, hgrn_w_out, moe_router, moe_router_b, moe_w_gate, moe_w_up, moe_w_down, moe_sh_gate, moe_sh_up, moe_sh_down):
    x = jnp.concatenate([x_prompt.reshape(BATCH * SEQ, D), x_sample.reshape(DEC_BATCH * DEC_SEQ, D)], axis=0)
    cond8 = jnp.concatenate([c_ctx[None], c, jnp.zeros((8 - 1 - DEC_BATCH, D), F32)], axis=0)
    mod4 = _mod_all(cond8, mod_w, mod_b)

    new_C, new_n, new_m, new_k, new_v, new_S = [], [], [], [], [], []
    for l in range(DEPTH):
        kind, j = l % 3, l // 3
        if kind == 0:
            a, (Cn, nn, mn) = _mlstm_layer(x, mod4, l, j, mlstm_w_in, mlstm_gate_b, mlstm_conv, mlstm_norm,
                                           state_mlstm_C, state_mlstm_n, state_mlstm_m)
            new_C.append(Cn)
            new_n.append(nn)
            new_m.append(mn)
            w_out = mlstm_w_out[j]
        elif kind == 1:
            a, (kn, vn) = _attn_layer(x, mod4, l, j, attn_w_in, attn_q_norm, attn_k_norm, cache_attn_k, cache_attn_v)
            new_k.append(kn)
            new_v.append(vn)
            w_out = attn_w_out[j]
        else:
            a, Sn = _hgrn_layer(x, mod4, l, j, hgrn_w_in, hgrn_f_b, hgrn_lower_bounds, hgrn_norm, state_hgrn_S)
            new_S.append(Sn)
            w_out = hgrn_w_out[j]
        x, hp, idx128, w128 = _post(a[0], a[1], x, mod4, l, w_out, ln_g[l, 0], ln_b[l, 0],
                                    moe_router[l], moe_router_b[l])
        routed = _moe_routed(hp, _route_tables(idx128, w128), l, moe_w_gate, moe_w_up, moe_w_down)
        x = _moe_post(x, hp, routed, mod4, l, moe_sh_gate[l], moe_sh_up[l], moe_sh_down[l], ln_g[l, 1], ln_b[l, 1])

    xp = x[:BATCH * SEQ].reshape(BATCH, SEQ, D)
    xs = x[BATCH * SEQ:].reshape(DEC_BATCH, DEC_SEQ, D)
    return (xp, xs, jnp.stack(new_C, 1), jnp.stack(new_n, 1), jnp.stack(new_m, 1),
            jnp.stack(new_k, 1), jnp.stack(new_v, 1), jnp.stack(new_S, 1))
```

```python
import functools

import jax
import jax.numpy as jnp
from jax import lax
from jax.experimental import pallas as pl
from jax.experimental.pallas import tpu as pltpu

F32 = jnp.float32
BF16 = jnp.bfloat16
I32 = jnp.int32
HI = lax.Precision.HIGHEST

D = 1024
BATCH, SEQ = 16, 256
DEPTH = 4
DEC_BATCH, DEC_SEQ = 2, 4096
PAST = 512
GRID_W = 64
GROUP = 4096
N_GROUPS = 3
T_ALL = N_GROUPS * GROUP

ML_H, ML_DK, ML_DV, ML_L = 4, 128, 256, 128
ML_QK, ML_V = ML_H * ML_DK, ML_H * ML_DV
AT_H, AT_KV, AT_HD, AT_G = 16, 4, 64, 4
HG_H, HG_DK, HG_DV, HG_L = 8, 128, 128, 64
HG_K, HG_V = HG_H * HG_DK, HG_H * HG_DV
N_EXP, TOP_K, D_EXP, D_SH = 64, 8, 256, 256
ROUTE_SCALE = 2.5
ALPHA = (2 * DEPTH) ** 0.25
EPS = 1e-6
ROPE_THETA = 10000.0

VMEM_LIMIT = 56 * 1024 * 1024
LANES = 128
PACK_ROWS = D // (2 * LANES)
AT_GW = AT_G * AT_HD
AT_KBLK = AT_H * AT_HD // AT_GW

MOE_NS = 2
MOE_G = T_ALL // MOE_NS
MOE_M = 256
MOE_NB = MOE_G * TOP_K // MOE_M + N_EXP
MOE_ROWS = MOE_NB * MOE_M
MOE_KEY = 16384
MOE_S = MOE_M + 8
MOE_U = 8
MOE_CLEAR = 256
SCAN_UNROLL = 4


def _cparams(sem):
    return pltpu.CompilerParams(dimension_semantics=sem, vmem_limit_bytes=VMEM_LIMIT)


def _sigmoid(x):
    return 1.0 / (1.0 + jnp.exp(-x))


def _silu(x):
    return x * _sigmoid(x)


def _log_sigmoid(x):
    return jnp.minimum(x, 0.0) - jnp.log(1.0 + jnp.exp(-jnp.abs(x)))


def _dot(a, b):
    return jnp.dot(a, b, preferred_element_type=F32)


def _dot_nt(a, b):
    return lax.dot_general(a, b, (((1,), (1,)), ((), ())), preferred_element_type=F32)


def _dot_tn(a, b):
    return lax.dot_general(a, b, (((0,), (0,)), ((), ())), preferred_element_type=F32)


def _dot_hi(a, b):
    return jnp.dot(a, b, preferred_element_type=F32, precision=HI)


def _split3(x):
    p0 = x.astype(BF16)
    r1 = x - p0.astype(F32)
    p1 = r1.astype(BF16)
    p2 = (r1 - p1.astype(F32)).astype(BF16)
    return p0, p1, p2


def _mask_dot(mask_bf16, x):
    p0, p1, p2 = _split3(x)
    return _dot(mask_bf16, p0) + _dot(mask_bf16, p1) + _dot(mask_bf16, p2)


def _dot_mask(x, mask_bf16):
    p0, p1, p2 = _split3(x)
    return _dot(p0, mask_bf16) + _dot(p1, mask_bf16) + _dot(p2, mask_bf16)


def _dot_3x(a, b):
    a0 = a.astype(BF16)
    a1 = (a - a0.astype(F32)).astype(BF16)
    b0 = b.astype(BF16)
    b1 = (b - b0.astype(F32)).astype(BF16)
    return _dot(a0, b0) + _dot(a0, b1) + _dot(a1, b0)


def _mod_kernel(cond_ref, w_ref, b_ref, o_ref):
    o_ref[...] = _dot_hi(_silu(cond_ref[...]), w_ref[...]) + b_ref[...]


def _mod_all(cond8, mod_w, mod_b):
    tn = 1024
    out = pl.pallas_call(
        _mod_kernel,
        grid=(DEPTH, 6 * D // tn),
        in_specs=[pl.BlockSpec((8, D), lambda l, j: (0, 0)),
                  pl.BlockSpec((None, D, tn), lambda l, j: (l, 0, j)),
                  pl.BlockSpec((None, 1, tn), lambda l, j: (l, 0, j))],
        out_specs=pl.BlockSpec((None, 8, tn), lambda l, j: (l, 0, j)),
        out_shape=jax.ShapeDtypeStruct((DEPTH, 8, 6 * D), F32),
        compiler_params=_cparams(("arbitrary", "arbitrary")),
        name="mod_rows",
    )(cond8, mod_w, mod_b.reshape(DEPTH, 1, 6 * D))
    return out.reshape(DEPTH, 8, 6, D)


def _proj_kernel(x_ref, mod_ref, w_ref, o_ref, h_scr):
    @pl.when(pl.program_id(1) == 0)
    def _():
        h_scr[...] = (x_ref[...] * (1.0 + mod_ref[1:2, :]) + mod_ref[0:1, :]).astype(BF16)

    o_ref[...] = _dot(h_scr[...], w_ref[...].astype(BF16)).astype(o_ref.dtype)


def _proj(x, mod4, layer, w3, widx, ncols, out_dtype, tm=2048, tn=512):
    T = x.shape[0]
    return pl.pallas_call(
        _proj_kernel,
        grid=(T // tm, ncols // tn),
        in_specs=[pl.BlockSpec((tm, D), lambda i, j: (i, 0)),
                  pl.BlockSpec((None, None, 6, D), lambda i, j: (layer, (i * tm) // GROUP, 0, 0)),
                  pl.BlockSpec((None, D, tn), lambda i, j: (widx, 0, j))],
        out_specs=pl.BlockSpec((tm, tn), lambda i, j: (i, j)),
        out_shape=jax.ShapeDtypeStruct((T, ncols), out_dtype),
        scratch_shapes=[pltpu.VMEM((tm, D), BF16)],
        compiler_params=_cparams(("arbitrary", "arbitrary")),
        name="in_proj",
    )(x, mod4, w3)


def _proj_small_kernel(x_ref, mod_ref, w_ref, b_ref, o_ref):
    h = x_ref[...] * (1.0 + mod_ref[1:2, :]) + mod_ref[0:1, :]
    o_ref[...] = _dot_3x(h, w_ref[...]) + b_ref[...]


def _proj_small(x, mod4, layer, w, b, tm=1024):
    T, n = x.shape[0], w.shape[1]
    return pl.pallas_call(
        _proj_small_kernel,
        grid=(T // tm,),
        in_specs=[pl.BlockSpec((tm, D), lambda i: (i, 0)),
                  pl.BlockSpec((None, None, 6, D), lambda i: (layer, (i * tm) // GROUP, 0, 0)),
                  pl.BlockSpec((D, n), lambda i: (0, 0)),
                  pl.BlockSpec((1, n), lambda i: (0, 0))],
        out_specs=pl.BlockSpec((tm, n), lambda i: (i, 0)),
        out_shape=jax.ShapeDtypeStruct((T, n), F32),
        compiler_params=_cparams(("arbitrary",)),
        name="gate_proj",
    )(x, mod4, w, b.reshape(1, n))


def _layer_norm_rows(r, g, b):
    mu = jnp.mean(r, axis=-1, keepdims=True)
    c = r - mu
    var = jnp.mean(c * c, axis=-1, keepdims=True)
    return c * lax.rsqrt(var + EPS) * g + b


def _pack_bf16_pairs(lo, hi):
    lo_b = lax.bitcast_convert_type(lo.astype(BF16).astype(F32), I32)
    hi_b = lax.bitcast_convert_type(hi.astype(BF16).astype(F32), I32)
    return lax.shift_right_logical(lo_b, 16) | (hi_b & jnp.int32(-65536))


def _unpack_bf16_pairs(v):
    lo = lax.bitcast_convert_type(lax.shift_left(v, 16), F32)
    hi = lax.bitcast_convert_type(v & jnp.int32(-65536), F32)
    return lo.astype(BF16), hi.astype(BF16)


def _unpack_rows(ref, rows, region_stride=None):
    chunks = []
    for r in range(PACK_ROWS):
        if region_stride is None:
            v = ref[pl.ds(r, rows, stride=PACK_ROWS), :]
        else:
            v = ref[pl.ds(r * region_stride, rows), :]
        chunks.extend(_unpack_bf16_pairs(v))
    return jnp.concatenate(chunks, axis=1)


def _post_kernel(ap_ref, as_ref, x_ref, mod_ref, w_ref, lng_ref, lnb_ref, rw_ref, rb_ref,
                 xo_ref, hp_ref, idx_ref, wt_ref, wb_scr, *, tm, n_ctx_tiles):
    @pl.when(pl.program_id(0) == 0)
    def _():
        wb_scr[...] = w_ref[...].astype(BF16)

    a = jnp.where(pl.program_id(0) < n_ctx_tiles, ap_ref[...], as_ref[...])
    y = _dot(a, wb_scr[...])
    r = ALPHA * x_ref[...] + mod_ref[2:3, :] * y
    xn = _layer_norm_rows(r, lng_ref[...], lnb_ref[...])
    xo_ref[...] = xn
    h1 = xn * (1.0 + mod_ref[4:5, :]) + mod_ref[3:4, :]
    for p in range(4):
        lo = h1[:, (2 * p) * LANES:(2 * p + 1) * LANES]
        hi = h1[:, (2 * p + 1) * LANES:(2 * p + 2) * LANES]
        hp_ref[pl.ds(p, tm, stride=4), :] = _pack_bf16_pairs(lo, hi)

    scores = _sigmoid(_dot_3x(h1, rw_ref[...]))
    sel = scores + rb_ref[...]
    e_iota = lax.broadcasted_iota(I32, (tm, N_EXP), 1).astype(F32)
    lane = lax.broadcasted_iota(I32, (tm, LANES), 1)
    idx_out = jnp.zeros((tm, LANES), F32)
    w_out = jnp.zeros((tm, LANES), F32)
    total = jnp.zeros((tm, 1), F32)
    for k in range(TOP_K):
        mx = jnp.max(sel, axis=-1, keepdims=True)
        ik = jnp.min(jnp.where(sel == mx, e_iota, float(N_EXP)), axis=-1, keepdims=True)
        hit = e_iota == ik
        wk = jnp.sum(jnp.where(hit, scores, 0.0), axis=-1, keepdims=True)
        sel = jnp.where(hit, -jnp.inf, sel)
        total = total + wk
        idx_out = jnp.where(lane == k, ik, idx_out)
        w_out = jnp.where(lane == k, wk, w_out)
    idx_ref[...] = idx_out.astype(I32)
    wt_ref[...] = w_out / total * ROUTE_SCALE


def _post(a_ctx, a_lat, x, mod4, layer, w_out, ln_g, ln_b, router_w, router_b, tm=512):
    T = x.shape[0]
    nc = a_ctx.shape[0] // tm
    return pl.pallas_call(
        functools.partial(_post_kernel, tm=tm, n_ctx_tiles=nc),
        grid=(T // tm,),
        in_specs=[pl.BlockSpec((tm, D), lambda i: (jnp.minimum(i, nc - 1), 0)),
                  pl.BlockSpec((tm, D), lambda i: (jnp.maximum(i - nc, 0), 0)),
                  pl.BlockSpec((tm, D), lambda i: (i, 0)),
                  pl.BlockSpec((None, None, 6, D), lambda i: (layer, (i * tm) // GROUP, 0, 0)),
                  pl.BlockSpec((D, D), lambda i: (0, 0)),
                  pl.BlockSpec((1, D), lambda i: (0, 0)),
                  pl.BlockSpec((1, D), lambda i: (0, 0)),
                  pl.BlockSpec((D, N_EXP), lambda i: (0, 0)),
                  pl.BlockSpec((1, N_EXP), lambda i: (0, 0))],
        out_specs=[pl.BlockSpec((tm, D), lambda i: (i, 0)),
                   pl.BlockSpec((tm * 4, LANES), lambda i: (i, 0)),
                   pl.BlockSpec((tm, LANES), lambda i: (i, 0)),
                   pl.BlockSpec((tm, LANES), lambda i: (i, 0))],
        out_shape=[jax.ShapeDtypeStruct((T, D), F32),
                   jax.ShapeDtypeStruct((T * 4, LANES), I32),
                   jax.ShapeDtypeStruct((T, LANES), I32),
                   jax.ShapeDtypeStruct((T, LANES), F32)],
        scratch_shapes=[pltpu.VMEM((D, D), BF16)],
        compiler_params=_cparams(("arbitrary",)),
        name="out_proj_ln_router",
    )(a_ctx, a_lat, x, mod4, w_out, ln_g.reshape(1, D), ln_b.reshape(1, D), router_w, router_b.reshape(1, N_EXP))


def _route_tables(idx128, w128):
    idx = idx128[:, :TOP_K].reshape(MOE_NS, MOE_G, TOP_K)
    w = w128[:, :TOP_K].reshape(MOE_NS, MOE_G, TOP_K)
    onehot = idx[..., None] == jnp.arange(N_EXP, dtype=I32)
    counts = jnp.sum(onehot.astype(I32), axis=(1, 2))
    padded = (counts + MOE_M - 1) // MOE_M * MOE_M
    pad_end = jnp.cumsum(padded, axis=-1)
    tok = jnp.arange(MOE_G, dtype=I32)[None, :, None]
    real_keys = (idx * MOE_KEY + tok).reshape(MOE_NS, MOE_G * TOP_K)
    fill = jnp.arange(MOE_M, dtype=I32)[None, None, :]
    e_ids = jnp.arange(N_EXP, dtype=I32)[None, :, None]
    fill_keys = jnp.where(fill < (padded - counts)[:, :, None], e_ids * MOE_KEY + MOE_KEY // 2 + fill,
                          N_EXP * MOE_KEY + e_ids * MOE_M + fill).reshape(MOE_NS, N_EXP * MOE_M)
    keys = jnp.concatenate([real_keys, fill_keys], axis=1)
    vals = jnp.concatenate([w.reshape(MOE_NS, MOE_G * TOP_K), jnp.zeros((MOE_NS, N_EXP * MOE_M), F32)], axis=1)
    sorted_rows = [lax.sort((keys[s], vals[s]), dimension=0, num_keys=1) for s in range(MOE_NS)]
    keys = jnp.stack([k for k, _ in sorted_rows])
    row_w = jnp.stack([v for _, v in sorted_rows])
    row_tok = jnp.where(keys < N_EXP * MOE_KEY, jnp.minimum(keys & (MOE_KEY - 1), MOE_G), MOE_G)
    row_tok = row_tok.reshape(-1)
    n_used = pad_end[:, -1] // MOE_M
    starts = jnp.arange(MOE_NB, dtype=I32) * MOE_M
    block_e = jnp.sum((starts[None, :, None] >= pad_end[:, None, :]).astype(I32), axis=-1)
    block_e = jnp.minimum(block_e, N_EXP - 1)
    last_e = jnp.take_along_axis(block_e, jnp.maximum(n_used - 1, 0)[:, None], axis=1)
    block_e = jnp.where(jnp.arange(MOE_NB, dtype=I32)[None, :] < n_used[:, None], block_e, last_e)
    return (block_e.reshape(-1).astype(I32), n_used.astype(I32), row_tok,
            row_w.reshape(MOE_NS * MOE_NB, 1, MOE_M))


def _cast_kernel(x_ref, o_ref):
    o_ref[...] = x_ref[...].astype(BF16)


def _cast_experts(w4, layer, eb=8):
    _, n_e, a, b = w4.shape
    return pl.pallas_call(
        _cast_kernel,
        grid=(n_e // eb,),
        in_specs=[pl.BlockSpec((None, eb, a, b), lambda i: (layer, i, 0, 0))],
        out_specs=pl.BlockSpec((eb, a, b), lambda i: (i, 0, 0)),
        out_shape=jax.ShapeDtypeStruct((n_e, a, b), BF16),
        compiler_params=_cparams(("arbitrary",)),
        name="cast_experts",
    )(w4)


def _moe_stage(tg_ref, ts_ref, x_v, acc, rw_ref, wg_ref, wu_ref, wd_ref, tile_g, tile_c, ys_c, ys_s):
    for mi in range(MOE_M):
        tile_g[pl.ds(mi, 4, stride=MOE_S), :] = x_v[pl.ds(pl.multiple_of(tg_ref[0, mi], 4), 4), :]
    xb = _unpack_rows(tile_c, MOE_M, MOE_S)
    g = _dot(xb, wg_ref[...])
    u = _dot(xb, wu_ref[...])
    ri = lax.broadcasted_iota(I32, (MOE_M, MOE_M), 0)
    ci = lax.broadcasted_iota(I32, (MOE_M, MOE_M), 1)
    rw_col = jnp.sum(jnp.where(ri == ci, jnp.broadcast_to(rw_ref[...], (MOE_M, MOE_M)), 0.0),
                     axis=1, keepdims=True)
    a = (_silu(g) * u) * rw_col
    y = _dot(a.astype(BF16), wd_ref[...])
    for c in range(D // LANES):
        ys_c[pl.ds(c * MOE_S, MOE_M), :] = y[:, c * LANES:(c + 1) * LANES]
    for m0 in range(0, MOE_M, MOE_U):
        offs = [pl.multiple_of(ts_ref[0, m0 + j], 8) for j in range(MOE_U)]
        vals = [acc[pl.ds(offs[j], 8), :] + ys_s[pl.ds(m0 + j, 8, stride=MOE_S), :] for j in range(MOE_U)]
        for j in range(MOE_U):
            acc[pl.ds(offs[j], 8), :] = vals[j]


def _moe_kernel(be_ref, nu_ref, tg_ref, ts_ref, x_hbm, rw_ref, wg_ref, wu_ref, wd_ref, out_hbm,
                x_v, acc, tile_a, tile_b, ys_a, ys_b, sem):
    s = pl.program_id(0)
    j = pl.program_id(1)

    @pl.when(j == 0)
    def _():
        cp = pltpu.make_async_copy(x_hbm.at[s], x_v.at[pl.ds(0, MOE_G * 4)], sem.at[0])
        cp.start()

        def clear(i, carry):
            acc[pl.ds(pl.multiple_of(i * MOE_CLEAR, MOE_CLEAR), MOE_CLEAR), :] = jnp.zeros((MOE_CLEAR, LANES), F32)
            return carry

        lax.fori_loop(0, MOE_G * 8 // MOE_CLEAR, clear, 0)
        acc[pl.ds(MOE_G * 8, 8), :] = jnp.zeros((8, LANES), F32)
        cp.wait()
        x_v[pl.ds(MOE_G * 4, 8), :] = jnp.zeros((8, LANES), I32)
        for t in (tile_a, tile_b):
            t[...] = jnp.zeros(t.shape, I32)
        for y in (ys_a, ys_b):
            y[...] = jnp.zeros(y.shape, F32)

    live = j < nu_ref[s] + 2
    common = (tg_ref, ts_ref, x_v, acc, rw_ref, wg_ref, wu_ref, wd_ref)

    @pl.when(live & (j % 2 == 0))
    def _():
        _moe_stage(*common, tile_a, tile_b, ys_b, ys_a)

    @pl.when(live & (j % 2 == 1))
    def _():
        _moe_stage(*common, tile_b, tile_a, ys_a, ys_b)

    @pl.when(j == MOE_NB + 1)
    def _():
        cp = pltpu.make_async_copy(acc.at[pl.ds(0, MOE_G * 8)], out_hbm.at[s], sem.at[1])
        cp.start()
        cp.wait()


def _moe_routed(hp, tables, layer, w_gate, w_up, w_down):
    block_e, n_used, row_tok, row_w = tables
    x3 = hp.reshape(MOE_NS, MOE_G * 4, LANES)
    blk = lambda s, j, d: s * MOE_NB + jnp.clip(j - d, 0, MOE_NB - 1)
    wspec = lambda shape: pl.BlockSpec((None,) + shape, lambda s, j, be, nu: (be[blk(s, j, 1)], 0, 0))
    tspec = lambda d: pl.BlockSpec((None, 1, MOE_M), lambda s, j, be, nu: (blk(s, j, d), 0, 0),
                                   memory_space=pltpu.SMEM)
    tok3 = row_tok.reshape(MOE_NS * MOE_NB, 1, MOE_M)
    out = pl.pallas_call(
        _moe_kernel,
        grid_spec=pltpu.PrefetchScalarGridSpec(
            num_scalar_prefetch=2,
            grid=(MOE_NS, MOE_NB + 2),
            in_specs=[tspec(0), tspec(2),
                      pl.BlockSpec(memory_space=pl.ANY),
                      pl.BlockSpec((None, 1, MOE_M), lambda s, j, be, nu: (blk(s, j, 1), 0, 0)),
                      wspec((D, D_EXP)), wspec((D, D_EXP)), wspec((D_EXP, D))],
            out_specs=pl.BlockSpec(memory_space=pl.ANY),
            scratch_shapes=[pltpu.VMEM((MOE_G * 4 + 8, LANES), I32),
                            pltpu.VMEM(((MOE_G + 1) * 8, LANES), F32),
                            pltpu.VMEM((4 * MOE_S, LANES), I32),
                            pltpu.VMEM((4 * MOE_S, LANES), I32),
                            pltpu.VMEM((8 * MOE_S, LANES), F32),
                            pltpu.VMEM((8 * MOE_S, LANES), F32),
                            pltpu.SemaphoreType.DMA((2,))]),
        out_shape=jax.ShapeDtypeStruct((MOE_NS, MOE_G * 8, LANES), F32),
        compiler_params=_cparams(("arbitrary", "arbitrary")),
        name="moe_routed",
    )(block_e, n_used, tok3 * 4, tok3 * 8, x3, row_w,
      _cast_experts(w_gate, layer), _cast_experts(w_up, layer), _cast_experts(w_down, layer))
    return out.reshape(T_ALL * 8, LANES)


def _moe_post_kernel(x_ref, hp_ref, r_ref, mod_ref, sg_ref, su_ref, sd_ref, lng_ref, lnb_ref, o_ref,
                     sgb, sub, sdb, *, tm):
    @pl.when(pl.program_id(0) == 0)
    def _():
        sgb[...] = sg_ref[...].astype(BF16)
        sub[...] = su_ref[...].astype(BF16)
        sdb[...] = sd_ref[...].astype(BF16)

    hb = _unpack_rows(hp_ref, tm)
    g = _dot(hb, sgb[...])
    u = _dot(hb, sub[...])
    sh = _dot((_silu(g) * u).astype(BF16), sdb[...])
    routed = jnp.concatenate([r_ref[pl.ds(c, tm, stride=8), :] for c in range(D // LANES)], axis=1)
    r = ALPHA * x_ref[...] + mod_ref[5:6, :] * (routed + sh)
    o_ref[...] = _layer_norm_rows(r, lng_ref[...], lnb_ref[...])


def _moe_post(x, hp, routed, mod4, layer, sg, su, sd, ln_g, ln_b, tm=512):
    T = x.shape[0]
    return pl.pallas_call(
        functools.partial(_moe_post_kernel, tm=tm),
        grid=(T // tm,),
        in_specs=[pl.BlockSpec((tm, D), lambda i: (i, 0)),
                  pl.BlockSpec((tm * 4, LANES), lambda i: (i, 0)),
                  pl.BlockSpec((tm * 8, LANES), lambda i: (i, 0)),
                  pl.BlockSpec((None, None, 6, D), lambda i: (layer, (i * tm) // GROUP, 0, 0)),
                  pl.BlockSpec((D, D_SH), lambda i: (0, 0)),
                  pl.BlockSpec((D, D_SH), lambda i: (0, 0)),
                  pl.BlockSpec((D_SH, D), lambda i: (0, 0)),
                  pl.BlockSpec((1, D), lambda i: (0, 0)),
                  pl.BlockSpec((1, D), lambda i: (0, 0))],
        out_specs=pl.BlockSpec((tm, D), lambda i: (i, 0)),
        out_shape=jax.ShapeDtypeStruct((T, D), F32),
        scratch_shapes=[pltpu.VMEM((D, D_SH), BF16), pltpu.VMEM((D, D_SH), BF16), pltpu.VMEM((D_SH, D), BF16)],
        compiler_params=_cparams(("arbitrary",)),
        name="shared_expert_ln",
    )(x, hp, routed, mod4, sg, su, sd, ln_g.reshape(1, D), ln_b.reshape(1, D))


def _conv_silu(src_ref, w_ref, dst_ref, T, scale):
    L = ML_L
    nc = T // L
    width = src_ref.shape[1]
    w0, w1, w2 = w_ref[0:1, :], w_ref[1:2, :], w_ref[2:3, :]
    row = lax.broadcasted_iota(I32, (L, width), 0)

    def body(c, carry):
        r0 = pl.multiple_of(c * L, L)
        cur = src_ref[pl.ds(r0, L), :].astype(F32)
        p0 = pl.multiple_of(jnp.maximum(r0 - 16, 0), 16)
        n0 = pl.multiple_of(jnp.minimum(r0 + L, T - 16), 16)
        prev_row = src_ref[pl.ds(p0, 16), :].astype(F32)[15:16, :] * jnp.where(c > 0, 1.0, 0.0).astype(F32)
        next_row = src_ref[pl.ds(n0, 16), :].astype(F32)[0:1, :] * jnp.where(c < nc - 1, 1.0, 0.0).astype(F32)
        prev = jnp.where(row == 0, prev_row, pltpu.roll(cur, 1, axis=0))
        nxt = jnp.where(row == L - 1, next_row, pltpu.roll(cur, L - 1, axis=0))
        dst_ref[pl.ds(r0, L), :] = _silu(w0 * prev + w1 * cur + w2 * nxt) * scale
        return carry

    lax.fori_loop(0, nc, body, 0)


def _mlstm_chunk(q, k, v, G, GT, C, n, m, backward):
    L = ML_L
    ri = lax.broadcasted_iota(I32, (L, L), 0)
    ci = lax.broadcasted_iota(I32, (L, L), 1)
    keep = (ci >= ri) if backward else (ci <= ri)
    A = keep.astype(BF16)
    AT = ((ri >= ci) if backward else (ri <= ci)).astype(BF16)
    ic, fc = (2, 3) if backward else (0, 1)
    last = 0 if backward else L - 1
    b_col = _mask_dot(A, _log_sigmoid(G))[:, fc:fc + 1]
    b_row = _dot_mask(_log_sigmoid(GT), AT)[fc:fc + 1, :]
    ig_col = G[:, ic:ic + 1]
    ig_row = GT[ic:ic + 1, :]
    log_d = jnp.where(keep, b_col - b_row + ig_row, -jnp.inf)
    log_inter = b_col + m
    m_t = jnp.maximum(log_inter, jnp.max(log_d, axis=-1, keepdims=True))
    d = jnp.exp(log_d - m_t)
    w_inter = jnp.exp(log_inter - m_t)
    qb, kb, vb = q.astype(BF16), k.astype(BF16), v.astype(BF16)
    s = _dot_nt(qb, kb) * d
    num = _dot(s.astype(BF16), vb) + w_inter * _dot(qb, C.astype(BF16))
    den = jnp.sum(s, axis=-1, keepdims=True) + w_inter * jnp.sum(q * n, axis=-1, keepdims=True)
    h = num / jnp.maximum(jnp.abs(den), jnp.exp(-m_t))
    m_new = m_t[last:last + 1, :]
    w_last = jnp.exp(b_col[last:last + 1, :] - b_col + ig_col - m_new)
    decay = w_inter[last:last + 1, :]
    kw = k * w_last
    C_new = decay * C + _dot_tn(kw.astype(BF16), vb)
    n_new = decay * n + jnp.sum(kw, axis=0, keepdims=True)
    return h, C_new, n_new, m_new


def _mlstm_kernel(q_ref, k_ref, v_ref, og_ref, g_ref, gt_ref, cq_ref, ck_ref, ng_ref, c0_ref, n0_ref, m0_ref,
                  a_ref, c_out, n_out, m_out, qs, ks, hf, hb, cst, nst, mst, *, T, nh):
    L = ML_L
    nc = T // L
    _conv_silu(q_ref, cq_ref, qs, T, ML_DK ** -0.5)
    _conv_silu(k_ref, ck_ref, ks, T, 1.0)
    cst[...] = c0_ref[...]
    nst[...] = n0_ref[...]
    mst[...] = m0_ref[...]
    un = min(SCAN_UNROLL, nc)

    def body(i, carry):
        for h in range(nh):
            kcols = slice(h * ML_DK, (h + 1) * ML_DK)
            vcols = slice(h * ML_DV, (h + 1) * ML_DV)
            for direction, out in ((0, hf), (1, hb)):
                C, n, m = cst[direction, h], nst[direction, h], mst[direction, h]
                for u in range(un):
                    ci = i * un + u
                    c = (nc - 1 - ci) if direction else ci
                    r0 = pl.multiple_of(c * L, L)
                    hh, C, n, m = _mlstm_chunk(
                        qs[pl.ds(r0, L), kcols], ks[pl.ds(r0, L), kcols], v_ref[pl.ds(r0, L), vcols],
                        g_ref[h, pl.ds(r0, L), :], gt_ref[h, c], C, n, m, backward=bool(direction))
                    out[pl.ds(r0, L), vcols] = hh
                cst[direction, h] = C
                nst[direction, h] = n
                mst[direction, h] = m
        return carry

    lax.fori_loop(0, nc // un, body, 0)
    c_out[...] = cst[...]
    n_out[...] = nst[...]
    m_out[...] = mst[...]

    def finish(c, carry):
        r0 = pl.multiple_of(c * L, L)
        for h in range(nh):
            vcols = slice(h * ML_DV, (h + 1) * ML_DV)
            tot = hf[pl.ds(r0, L), vcols] + hb[pl.ds(r0, L), vcols]
            mu = jnp.mean(tot, axis=-1, keepdims=True)
            cen = tot - mu
            var = jnp.mean(cen * cen, axis=-1, keepdims=True)
            hn = cen * lax.rsqrt(var + EPS) * ng_ref[:, vcols]
            a_ref[pl.ds(r0, L), vcols] = (hn * _sigmoid(og_ref[pl.ds(r0, L), vcols].astype(F32))).astype(BF16)
        return carry

    lax.fori_loop(0, nc, finish, 0)


def _mlstm_scan(z, gh, ght, conv_w, norm_g, C0, n0, m0, T, row_blk0, nseq, nh):
    nc = T // ML_L
    kw, vw = nh * ML_DK, nh * ML_DV
    qcol, kcol = 0, ML_QK // kw
    vcol, ocol = 2 * ML_QK // vw, (2 * ML_QK + ML_V) // vw
    rb = lambda s: row_blk0 + s
    state = lambda *tail: pl.BlockSpec((None, 2, nh) + tail, lambda s, h: (s, 0, h) + (0,) * len(tail))
    return pl.pallas_call(
        functools.partial(_mlstm_kernel, T=T, nh=nh),
        grid=(nseq, ML_H // nh),
        in_specs=[pl.BlockSpec((T, kw), lambda s, h: (rb(s), qcol + h)),
                  pl.BlockSpec((T, kw), lambda s, h: (rb(s), kcol + h)),
                  pl.BlockSpec((T, vw), lambda s, h: (rb(s), vcol + h)),
                  pl.BlockSpec((T, vw), lambda s, h: (rb(s), ocol + h)),
                  pl.BlockSpec((nh, T, 4), lambda s, h: (h, rb(s), 0)),
                  pl.BlockSpec((nh, nc, 4, ML_L), lambda s, h: (h, rb(s), 0, 0)),
                  pl.BlockSpec((3, kw), lambda s, h: (0, qcol + h)),
                  pl.BlockSpec((3, kw), lambda s, h: (0, kcol + h)),
                  pl.BlockSpec((1, vw), lambda s, h: (0, h)),
                  state(ML_DK, ML_DV), state(1, ML_DK), state(1, 1)],
        out_specs=[pl.BlockSpec((T, vw), lambda s, h: (s, h)),
                   state(ML_DK, ML_DV), state(1, ML_DK), state(1, 1)],
        out_shape=[jax.ShapeDtypeStruct((nseq * T, ML_V), BF16),
                   jax.ShapeDtypeStruct((nseq, 2, ML_H, ML_DK, ML_DV), F32),
                   jax.ShapeDtypeStruct((nseq, 2, ML_H, 1, ML_DK), F32),
                   jax.ShapeDtypeStruct((nseq, 2, ML_H, 1, 1), F32)],
        scratch_shapes=[pltpu.VMEM((T, kw), F32), pltpu.VMEM((T, kw), F32),
                        pltpu.VMEM((T, vw), F32), pltpu.VMEM((T, vw), F32),
                        pltpu.VMEM((2, nh, ML_DK, ML_DV), F32), pltpu.VMEM((2, nh, 1, ML_DK), F32),
                        pltpu.VMEM((2, nh, 1, 1), F32)],
        compiler_params=_cparams(("arbitrary", "arbitrary")),
        name="mlstm_scan",
    )(z, z, z, z, gh, ght, conv_w, conv_w, norm_g.reshape(1, ML_V), C0, n0, m0)


def _mlstm_layer(x, mod4, layer, j, w_in, gate_b, conv_w, norm_g, C_lat, n_lat, m_lat):
    z = _proj(x, mod4, layer, w_in, j, 2 * ML_QK + 2 * ML_V, BF16)
    gates = _proj_small(x, mod4, layer, w_in[j][:, 2 * ML_QK + 2 * ML_V:], gate_b[j])
    gh = gates.reshape(T_ALL, 4, ML_H).transpose(2, 0, 1)
    ght = gh.reshape(ML_H, T_ALL // ML_L, ML_L, 4).transpose(0, 1, 3, 2)
    zC = jnp.zeros((BATCH, 2, ML_H, ML_DK, ML_DV), F32)
    zn = jnp.zeros((BATCH, 2, ML_H, 1, ML_DK), F32)
    zm = jnp.zeros((BATCH, 2, ML_H, 1, 1), F32)
    a_p, Cn, nn, mn = _mlstm_scan(z, gh, ght, conv_w[j], norm_g[j], zC, zn, zm, SEQ, 0, BATCH, ML_H)
    a_s, _, _, _ = _mlstm_scan(z, gh, ght, conv_w[j], norm_g[j], C_lat[:, j],
                               n_lat[:, j].reshape(DEC_BATCH, 2, ML_H, 1, ML_DK),
                               m_lat[:, j].reshape(DEC_BATCH, 2, ML_H, 1, 1),
                               DEC_SEQ, BATCH * SEQ // DEC_SEQ, DEC_BATCH, 1)
    return (a_p, a_s), (Cn, nn.reshape(BATCH, 2, ML_H, ML_DK), mn.reshape(BATCH, 2, ML_H))


def _hgrn_chunk(q, k, v, g, St, backward):
    L = HG_L
    ri = lax.broadcasted_iota(I32, (L, L), 0)
    ci = lax.broadcasted_iota(I32, (L, L), 1)
    keep = (ci >= ri) if backward else (ci <= ri)
    ref = L - 1 - L // 2 if backward else L // 2
    last = 0 if backward else L - 1
    b = _mask_dot(keep.astype(BF16), g)
    b_ref = b[ref:ref + 1, :]
    b_last = b[last:last + 1, :]
    qe = (q * jnp.exp(b - b_ref)).astype(BF16)
    ke = (k * jnp.exp(b_ref - b)).astype(BF16)
    vb = v.astype(BF16)
    a = jnp.where(keep, _dot_nt(qe, ke), 0.0)
    o = _dot(a.astype(BF16), vb) + _dot_nt((q * jnp.exp(b)).astype(BF16), St.astype(BF16))
    kd = (k * jnp.exp(b_last - b)).astype(BF16)
    St_new = jnp.exp(b_last) * St + _dot_tn(vb, kd)
    return o, St_new


def _hgrn_kernel(q_ref, i_ref, ff_ref, fb_ref, og_ref, fbf_ref, fbb_ref, lbr_ref, ng_ref, s0_ref,
                 a_ref, s_out, of, ob, st, *, T, lb_layer, nh):
    L = HG_L
    nc = T // L
    raw = lbr_ref[...]
    e = jnp.exp(raw - jnp.max(raw, axis=0, keepdims=True))
    p = e / jnp.sum(e, axis=0, keepdims=True)
    lb_all = jnp.sum(p[0:lb_layer + 1, :], axis=0, keepdims=True) - p[0:1, :]
    for h in range(nh):
        st[0, h] = s0_ref[0, h].T
        st[1, h] = s0_ref[1, h].T
    un = min(SCAN_UNROLL, nc)

    def body(i, carry):
        for h in range(nh):
            cols = slice(h * HG_DK, (h + 1) * HG_DK)
            lb = lb_all[:, cols]
            for direction, out, f_ref, b_ref in ((0, of, ff_ref, fbf_ref), (1, ob, fb_ref, fbb_ref)):
                St = st[direction, h]
                for u in range(un):
                    ci = i * un + u
                    c = (nc - 1 - ci) if direction else ci
                    r0 = pl.multiple_of(c * L, L)
                    f = lb + (1.0 - lb) * _sigmoid(f_ref[pl.ds(r0, L), cols] + b_ref[:, cols])
                    o, St = _hgrn_chunk(_silu(q_ref[pl.ds(r0, L), cols]), 1.0 - f, i_ref[pl.ds(r0, L), cols],
                                        jnp.log(f), St, backward=bool(direction))
                    out[pl.ds(r0, L), cols] = o
                st[direction, h] = St
        return carry

    lax.fori_loop(0, nc // un, body, 0)
    for h in range(nh):
        s_out[0, h] = st[0, h].T
        s_out[1, h] = st[1, h].T

    def finish(c, carry):
        r0 = pl.multiple_of(c * L, L)
        for h in range(nh):
            cols = slice(h * HG_DV, (h + 1) * HG_DV)
            tot = of[pl.ds(r0, L), cols] + ob[pl.ds(r0, L), cols]
            on = tot * lax.rsqrt(jnp.mean(tot * tot, axis=-1, keepdims=True) + EPS) * ng_ref[:, cols]
            a_ref[pl.ds(r0, L), cols] = (on * _silu(og_ref[pl.ds(r0, L), cols])).astype(BF16)
        return carry

    lax.fori_loop(0, nc, finish, 0)


def _hgrn_scan(z, f_b, lb_raw, norm_g, S0, T, row_blk0, nseq, lb_layer, nh):
    nhb = HG_H // nh
    w = nh * HG_DK
    rb = lambda s: row_blk0 + s
    zspec = lambda cb: pl.BlockSpec((T, w), lambda s, h: (rb(s), cb * nhb + h))
    sspec = pl.BlockSpec((None, 2, nh, HG_DK, HG_DV), lambda s, h: (s, 0, h, 0, 0))
    return pl.pallas_call(
        functools.partial(_hgrn_kernel, T=T, lb_layer=lb_layer, nh=nh),
        grid=(nseq, nhb),
        in_specs=[zspec(0), zspec(1), zspec(2), zspec(3), zspec(4),
                  pl.BlockSpec((1, w), lambda s, h: (0, h)),
                  pl.BlockSpec((1, w), lambda s, h: (0, nhb + h)),
                  pl.BlockSpec((DEPTH, w), lambda s, h: (0, h)),
                  pl.BlockSpec((1, w), lambda s, h: (0, h)),
                  sspec],
        out_specs=[pl.BlockSpec((T, w), lambda s, h: (s, h)), sspec],
        out_shape=[jax.ShapeDtypeStruct((nseq * T, HG_V), BF16),
                   jax.ShapeDtypeStruct((nseq, 2, HG_H, HG_DK, HG_DV), F32)],
        scratch_shapes=[pltpu.VMEM((T, w), F32), pltpu.VMEM((T, w), F32),
                        pltpu.VMEM((2, nh, HG_DV, HG_DK), F32)],
        compiler_params=_cparams(("arbitrary", "arbitrary")),
        name="hgrn_scan",
    )(z, z, z, z, z, f_b.reshape(1, 2 * HG_K), f_b.reshape(1, 2 * HG_K), lb_raw, norm_g.reshape(1, HG_V), S0)


def _hgrn_layer(x, mod4, layer, j, w_in, f_b, lb_raw, norm_g, S_lat):
    z = _proj(x, mod4, layer, w_in, j, 3 * HG_K + 2 * HG_V, F32)
    zS = jnp.zeros((BATCH, 2, HG_H, HG_DK, HG_DV), F32)
    a_p, Sn = _hgrn_scan(z, f_b[j], lb_raw, norm_g[j], zS, SEQ, 0, BATCH, layer, 4)
    a_s, _ = _hgrn_scan(z, f_b[j], lb_raw, norm_g[j], S_lat[:, j], DEC_SEQ, BATCH * SEQ // DEC_SEQ, DEC_BATCH,
                        layer, 1)
    return (a_p, a_s), Sn


def _head_rms(x, g_tiled, nheads):
    lane = lax.broadcasted_iota(I32, x.shape, 1)
    sq = x * x
    ms = jnp.zeros_like(x)
    for h in range(nheads):
        in_h = (lane >= h * AT_HD) & (lane < (h + 1) * AT_HD)
        tot = jnp.sum(jnp.where(in_h, sq, 0.0), axis=-1, keepdims=True)
        ms = jnp.where(in_h, tot, ms)
    return x * lax.rsqrt(ms * (1.0 / AT_HD) + EPS) * g_tiled


def _rope(x, cos, sin):
    w = x.shape[1]
    lane = lax.broadcasted_iota(I32, x.shape, 1)
    up = pltpu.roll(x, w - 16, axis=1)
    down = pltpu.roll(x, 16, axis=1)
    swapped = jnp.where((lane % 32) < 16, up, down)
    return x * cos + swapped * sin


def _attn_ctx_kernel(q_ref, k_ref, v_ref, qn_ref, kn_ref, o_ref, ko_ref, vo_ref):
    k = _head_rms(k_ref[...].astype(F32), kn_ref[...], AT_KV)
    ko_ref[...] = k
    v = v_ref[...].astype(F32)
    vo_ref[...] = v
    kb, vb = k.astype(BF16), v.astype(BF16)
    outs = []
    for kv in range(AT_KV):
        q = _head_rms(q_ref[:, kv * AT_GW:(kv + 1) * AT_GW].astype(F32), qn_ref[...], AT_G) * (AT_HD ** -0.5)
        kh = kb[:, kv * AT_HD:(kv + 1) * AT_HD]
        vh = vb[:, kv * AT_HD:(kv + 1) * AT_HD]
        for g in range(AT_G):
            s = _dot_nt(q[:, g * AT_HD:(g + 1) * AT_HD].astype(BF16), kh)
            p = jnp.exp(s - jnp.max(s, axis=-1, keepdims=True))
            p = p / jnp.sum(p, axis=-1, keepdims=True)
            outs.append(_dot(p.astype(BF16), vh))
    o_ref[...] = jnp.concatenate(outs, axis=1).astype(BF16)


def _attn_ctx(z, q_norm, k_norm):
    qn = jnp.tile(q_norm, AT_G).reshape(1, AT_GW)
    kn = jnp.tile(k_norm, AT_KV).reshape(1, AT_GW)
    return pl.pallas_call(
        _attn_ctx_kernel,
        grid=(BATCH,),
        in_specs=[pl.BlockSpec((SEQ, D), lambda b: (b, 0)),
                  pl.BlockSpec((SEQ, AT_GW), lambda b: (b, AT_KBLK)),
                  pl.BlockSpec((SEQ, AT_GW), lambda b: (b, AT_KBLK + 1)),
                  pl.BlockSpec((1, AT_GW), lambda b: (0, 0)),
                  pl.BlockSpec((1, AT_GW), lambda b: (0, 0))],
        out_specs=[pl.BlockSpec((SEQ, D), lambda b: (b, 0)),
                   pl.BlockSpec((SEQ, AT_GW), lambda b: (b, 0)),
                   pl.BlockSpec((SEQ, AT_GW), lambda b: (b, 0))],
        out_shape=[jax.ShapeDtypeStruct((BATCH * SEQ, D), BF16),
                   jax.ShapeDtypeStruct((BATCH * SEQ, AT_GW), F32),
                   jax.ShapeDtypeStruct((BATCH * SEQ, AT_GW), F32)],
        compiler_params=_cparams(("arbitrary",)),
        name="attn_context",
    )(z, z, z, qn, kn)


def _attn_kv_kernel(k_ref, v_ref, kn_ref, cos_ref, sin_ref, ko_ref, vo_ref):
    k = _rope(_head_rms(k_ref[...].astype(F32), kn_ref[...], AT_KV), cos_ref[...], sin_ref[...])
    v = v_ref[...]
    for h in range(AT_KV):
        ko_ref[h] = k[:, h * AT_HD:(h + 1) * AT_HD].astype(BF16)
        vo_ref[h] = v[:, h * AT_HD:(h + 1) * AT_HD].astype(BF16)


def _attn_kv(z, k_norm, cos4, sin4, tt=512):
    kn = jnp.tile(k_norm, AT_KV).reshape(1, AT_GW)
    nt = DEC_SEQ // tt
    row0 = BATCH * SEQ // tt
    return pl.pallas_call(
        _attn_kv_kernel,
        grid=(DEC_BATCH, nt),
        in_specs=[pl.BlockSpec((tt, AT_GW), lambda b, i: (row0 + b * nt + i, AT_KBLK)),
                  pl.BlockSpec((tt, AT_GW), lambda b, i: (row0 + b * nt + i, AT_KBLK + 1)),
                  pl.BlockSpec((1, AT_GW), lambda b, i: (0, 0)),
                  pl.BlockSpec((tt, AT_GW), lambda b, i: (i, 0)),
                  pl.BlockSpec((tt, AT_GW), lambda b, i: (i, 0))],
        out_specs=[pl.BlockSpec((None, AT_KV, tt, AT_HD), lambda b, i: (b, 0, i, 0)),
                   pl.BlockSpec((None, AT_KV, tt, AT_HD), lambda b, i: (b, 0, i, 0))],
        out_shape=[jax.ShapeDtypeStruct((DEC_BATCH, AT_KV, DEC_SEQ, AT_HD), BF16),
                   jax.ShapeDtypeStruct((DEC_BATCH, AT_KV, DEC_SEQ, AT_HD), BF16)],
        compiler_params=_cparams(("arbitrary", "arbitrary")),
        name="attn_kv_prep",
    )(z, z, kn, cos4, sin4)


def _attn_lat_kernel(q_ref, qn_ref, cos_ref, sin_ref, k_ref, v_ref, o_ref, *, tq, tk):
    q = _rope(_head_rms(q_ref[...].astype(F32), qn_ref[...], AT_G), cos_ref[...], sin_ref[...]) * (AT_HD ** -0.5)
    qs = jnp.concatenate([q[:, g * AT_HD:(g + 1) * AT_HD] for g in range(AT_G)], axis=0).astype(BF16)
    rows = AT_G * tq
    nk = k_ref.shape[0] // tk

    def body(j, carry):
        m, l, acc = carry
        k0 = pl.multiple_of(j * tk, tk)
        s = _dot_nt(qs, k_ref[pl.ds(k0, tk), :])
        m_new = jnp.maximum(m, jnp.max(s, axis=-1, keepdims=True))
        alpha = jnp.exp(m - m_new)
        p = jnp.exp(s - m_new)
        l = alpha * l + jnp.sum(p, axis=-1, keepdims=True)
        acc = alpha * acc + _dot(p.astype(BF16), v_ref[pl.ds(k0, tk), :])
        return m_new, l, acc

    init = (jnp.full((rows, 1), -jnp.inf, F32), jnp.zeros((rows, 1), F32), jnp.zeros((rows, AT_HD), F32))
    _, l, acc = lax.fori_loop(0, nk, body, init)
    o = acc / l
    o_ref[...] = jnp.concatenate([o[g * tq:(g + 1) * tq, :] for g in range(AT_G)], axis=1).astype(BF16)


def _attn_lat(z, q_norm, cos4, sin4, kk, vv, tq=256, tk=1536):
    qn = jnp.tile(q_norm, AT_G).reshape(1, AT_GW)
    nq = DEC_SEQ // tq
    row0 = BATCH * SEQ // tq
    skv = kk.shape[2]
    return pl.pallas_call(
        functools.partial(_attn_lat_kernel, tq=tq, tk=tk),
        grid=(DEC_BATCH, AT_KV, nq),
        in_specs=[pl.BlockSpec((tq, AT_GW), lambda b, h, i: (row0 + b * nq + i, h)),
                  pl.BlockSpec((1, AT_GW), lambda b, h, i: (0, 0)),
                  pl.BlockSpec((tq, AT_GW), lambda b, h, i: (i, 0)),
                  pl.BlockSpec((tq, AT_GW), lambda b, h, i: (i, 0)),
                  pl.BlockSpec((None, None, skv, AT_HD), lambda b, h, i: (b, h, 0, 0)),
                  pl.BlockSpec((None, None, skv, AT_HD), lambda b, h, i: (b, h, 0, 0))],
        out_specs=pl.BlockSpec((tq, AT_GW), lambda b, h, i: (b * nq + i, h)),
        out_shape=jax.ShapeDtypeStruct((DEC_BATCH * DEC_SEQ, D), BF16),
        compiler_params=_cparams(("arbitrary", "arbitrary", "arbitrary")),
        name="attn_latent",
    )(z, qn, cos4, sin4, kk, vv)


def _rope_tables():
    t = jnp.arange(DEC_SEQ)
    row = (t // GRID_W).astype(F32)
    col = (t % GRID_W).astype(F32)
    nf = AT_HD // 4
    inv = ROPE_THETA ** (-jnp.arange(nf, dtype=F32) / nf)
    ar, ac = row[:, None] * inv[None], col[:, None] * inv[None]
    cos = jnp.concatenate([jnp.cos(ar), jnp.cos(ar), jnp.cos(ac), jnp.cos(ac)], axis=1)
    sin = jnp.concatenate([-jnp.sin(ar), jnp.sin(ar), -jnp.sin(ac), jnp.sin(ac)], axis=1)
    return jnp.tile(cos, (1, 4)), jnp.tile(sin, (1, 4))


def _attn_layer(x, mod4, layer, j, w_in, q_norm, k_norm, cache_k, cache_v):
    z = _proj(x, mod4, layer, w_in, j, (AT_H + 2 * AT_KV) * AT_HD, BF16)
    a_p, k_new, v_new = _attn_ctx(z, q_norm[j], k_norm[j])
    cos4, sin4 = _rope_tables()
    k_lat, v_lat = _attn_kv(z, k_norm[j], cos4, sin4)
    kk = jnp.concatenate([cache_k[:, j].transpose(0, 2, 1, 3).astype(BF16), k_lat], axis=2)
    vv = jnp.concatenate([cache_v[:, j].transpose(0, 2, 1, 3).astype(BF16), v_lat], axis=2)
    a_s = _attn_lat(z, q_norm[j], cos4, sin4, kk, vv)
    return (a_p, a_s), (k_new.reshape(BATCH, SEQ, AT_KV, AT_HD), v_new.reshape(BATCH, SEQ, AT_KV, AT_HD))


def kernel(x_prompt, x_sample, state_mlstm_C, state_mlstm_n, state_mlstm_m, cache_attn_k, cache_attn_v, state_hgrn_S, c, c_ctx, mod_w, mod_b, ln_g, ln_b, mlstm_w_in, mlstm_gate_b, mlstm_conv, mlstm_norm, mlstm_w_out, attn_w_in, attn_q_norm, attn_k_norm, attn_w_out, hgrn_w_in, hgrn_f_b, hgrn_lower_bounds, hgrn_norm, hgrn_w_out, moe_router, moe_router_b, moe_w_gate, moe_w_up, moe_w_down, moe_sh_gate, moe_sh_up, moe_sh_down):
    x = jnp.concatenate([x_prompt.reshape(BATCH * SEQ, D), x_sample.reshape(DEC_BATCH * DEC_SEQ, D)], axis=0)
    cond8 = jnp.concatenate([c_ctx[None], c, jnp.zeros((8 - 1 - DEC_BATCH, D), F32)], axis=0)
    mod4 = _mod_all(cond8, mod_w, mod_b)

    new_C, new_n, new_m, new_k, new_v, new_S = [], [], [], [], [], []
    for l in range(DEPTH):
        kind, j = l % 3, l // 3
        if kind == 0:
            a, (Cn, nn, mn) = _mlstm_layer(x, mod4, l, j, mlstm_w_in, mlstm_gate_b, mlstm_conv, mlstm_norm,
                                           state_mlstm_C, state_mlstm_n, state_mlstm_m)
            new_C.append(Cn)
            new_n.append(nn)
            new_m.append(mn)
            w_out = mlstm_w_out[j]
        elif kind == 1:
            a, (kn, vn) = _attn_layer(x, mod4, l, j, attn_w_in, attn_q_norm, attn_k_norm, cache_attn_k, cache_attn_v)
            new_k.append(kn)
            new_v.append(vn)
            w_out = attn_w_out[j]
        else:
            a, Sn = _hgrn_layer(x, mod4, l, j, hgrn_w_in, hgrn_f_b, hgrn_lower_bounds, hgrn_norm, state_hgrn_S)
            new_S.append(Sn)
            w_out = hgrn_w_out[j]
        x, hp, idx128, w128 = _post(a[0], a[1], x, mod4, l, w_out, ln_g[l, 0], ln_b[l, 0],
                                    moe_router[l], moe_router_b[l])
        routed = _moe_routed(hp, _route_tables(idx128, w128), l, moe_w_gate, moe_w_up, moe_w_down)
        x = _moe_post(x, hp, routed, mod4, l, moe_sh_gate[l], moe_sh_up[l], moe_sh_down[l], ln_g[l, 1], ln_b[l, 1])

    xp = x[:BATCH * SEQ].reshape(BATCH, SEQ, D)
    xs = x[BATCH * SEQ:].reshape(DEC_BATCH, DEC_SEQ, D)
    return (xp, xs, jnp.stack(new_C, 1), jnp.stack(new_n, 1), jnp.stack(new_m, 1),
            jnp.stack(new_k, 1), jnp.stack(new_v, 1), jnp.stack(new_S, 1))
```
